```python
import jax
import jax.numpy as jnp
from jax import lax
import numpy as np

D_MODEL = 1024
BATCH = 8
SEQ = 4096
DEPTH = 1

GRID_W = 64
CTX_LEN = 256
NORM_EPS = 1e-6

RWKV_WIDTH = 1024
RWKV_HEAD = 64
RWKV_HEADS = RWKV_WIDTH // RWKV_HEAD
DECAY_LORA = 64
AAA_LORA = 64
GATE_LORA = 160
LNX_EPS = 64e-5

LRU_WIDTH = 1024
LRU_BLOCKS = 16
LRU_BLOCK = LRU_WIDTH // LRU_BLOCKS
CONV_W = 4
CONV_LEFT = 2
LRU_C = 8.0

N_EXPERTS = 64
TOP_K = 6
N_GROUPS = 8
TOPK_GROUPS = 4
EXPERT_FF = 256
SHARED_FF = 256
ROUTED_SCALE = 2.5
MOE_BLOCK = 128

OFF_R = 0
OFF_K = OFF_R + RWKV_WIDTH
OFF_V = OFF_K + RWKV_WIDTH
OFF_WD = OFF_V + RWKV_WIDTH
OFF_AD = OFF_WD + 2 * DECAY_LORA
OFF_GD = OFF_AD + 2 * AAA_LORA
RWKV_COLS = OFF_GD + GATE_LORA
OFF_LX = RWKV_COLS
OFF_LG = OFF_LX + LRU_WIDTH
OFF_GATE = OFF_LG + LRU_WIDTH
IN_COLS = OFF_GATE + 2 * D_MODEL

kernel_name = 'hybrid_rwkv7_rglru_moe_flow_block'


def _rmsnorm(x, g):
    xf = x.astype(jnp.float32)
    y = xf * lax.rsqrt(jnp.mean(xf * xf, axis=-1, keepdims=True) + NORM_EPS)
    return (y * g.astype(jnp.float32)).astype(x.dtype)


def _modulate(x, shift, scale):
    return x * (1.0 + scale) + shift


def _shift_centred(z, mu):
    pad = [(0, 0)] * (z.ndim - 2) + [(1, 1), (0, 0)]
    zp = jnp.pad(z, pad)
    return z + mu[0] * (zp[..., :-2, :] - z) + mu[1] * (zp[..., 2:, :] - z)


def _dwconv(z, w, b):
    lead, (t, ch) = z.shape[:-2], z.shape[-2:]
    y = lax.conv_general_dilated(z.reshape((-1, t, ch)), w[:, None, :].astype(z.dtype),
                                 window_strides=(1,), padding=[(CONV_LEFT, CONV_W - 1 - CONV_LEFT)],
                                 dimension_numbers=('NWC', 'WIO', 'NWC'), feature_group_count=ch)
    return (y + b).reshape(lead + (t, ch))


def _swiglu(x, w_gu, w_down):
    g, u = jnp.split(x @ w_gu, 2, axis=-1)
    return (jax.nn.silu(g) * u) @ w_down


def _dirs_to_scan(t):
    s = jnp.stack([t[:, :, 0], jnp.flip(t[:, :, 1], axis=1)], axis=0)
    return jnp.moveaxis(s, 2, 0)


def _scan_to_dirs(t):
    t = jnp.moveaxis(t, 0, 2)
    return jnp.stack([t[0], jnp.flip(t[1], axis=1)], axis=2)


def _rwkv_scan(r, decay, k, v, kk, b, s0):
    def step(s, inp):
        r_t, w_t, k_t, v_t, kk_t, b_t = inp
        s_a = jnp.einsum('dbhvk,dbhk->dbhv', s, -kk_t)
        s = s * w_t[..., None, :] + s_a[..., None] * b_t[..., None, :] + v_t[..., None] * k_t[..., None, :]
        return s, jnp.einsum('dbhvk,dbhk->dbhv', s, r_t)
    s_fin, y = lax.scan(step, s0, (r, decay, k, v, kk, b))
    return y, s_fin


def _linear_scan(a, bx, h0):
    def comb(lhs, rhs):
        return lhs[0] * rhs[0], rhs[0] * lhs[1] + rhs[1]
    a_cum, b_cum = lax.associative_scan(comb, (a, bx), axis=0)
    h = a_cum * h0 + b_cum
    return h, h[-1]


def _rwkv_branch(zr, s0, with_output, w0, w_up, a0, a_up, g_up, k_k, k_a, r_k, lnx):
    bsz, t, _ = zr.shape
    f32 = jnp.float32
    r = zr[..., OFF_R:OFF_K]
    k = zr[..., OFF_K:OFF_V]
    v = zr[..., OFF_V:OFF_WD]
    wd = zr[..., OFF_WD:OFF_AD].reshape(bsz, t, 2, DECAY_LORA)
    ad = zr[..., OFF_AD:OFF_GD].reshape(bsz, t, 2, AAA_LORA)
    w_log = -jax.nn.softplus(-(w0 + jnp.einsum('btdr,drc->btdc', jnp.tanh(wd), w_up)).astype(f32)) - 0.5
    decay = jnp.exp(-jnp.exp(w_log))
    a = jax.nn.sigmoid(a0 + jnp.einsum('btdr,drc->btdc', ad, a_up)).astype(f32)
    heads = lambda u: u.reshape(u.shape[:-1] + (RWKV_HEADS, RWKV_HEAD))
    both = lambda u: jnp.broadcast_to(u[:, :, None], (bsz, t, 2) + u.shape[2:])
    kk = heads((k * k_k).astype(f32))
    kk = kk / jnp.maximum(jnp.linalg.norm(kk, axis=-1, keepdims=True), 1e-12)
    k_dir = heads(k[:, :, None].astype(f32) * (1.0 + (a - 1.0) * k_a))
    r_h = heads(r.astype(f32))
    v_h = heads(v.astype(f32))
    if s0 is None:
        s0 = jnp.zeros((2, bsz, RWKV_HEADS, RWKV_HEAD, RWKV_HEAD), f32)
    seq = [both(r_h), heads(decay), k_dir, both(v_h), both(kk), both(kk) * heads(a)]
    y, s_fin = _rwkv_scan(*[_dirs_to_scan(u) for u in seq], s0)
    if not with_output:
        return None, s_fin
    y = _scan_to_dirs(y).sum(axis=2)
    mu = jnp.mean(y, axis=-1, keepdims=True)
    var = jnp.mean(jnp.square(y - mu), axis=-1, keepdims=True)
    y = ((y - mu) * lax.rsqrt(var + LNX_EPS)).reshape(bsz, t, RWKV_WIDTH) * lnx[0] + lnx[1]
    bonus = jnp.sum(r_h[:, :, None] * k_dir * r_k, axis=(2, 4))
    y = y + (bonus[..., None] * v_h).reshape(bsz, t, RWKV_WIDTH)
    g = jax.nn.sigmoid(zr[..., OFF_GD:RWKV_COLS]) @ g_up
    return (y * g).astype(zr.dtype), s_fin


def _rglru_branch(xc, zg, h0, with_output, gate_w, gate_b, lru_l):
    bsz, t, ch = xc.shape
    f32 = jnp.float32
    xb = xc.reshape(bsz, t, LRU_BLOCKS, LRU_BLOCK)
    gates = jax.nn.sigmoid((jnp.einsum('btnj,dgnjk->btdgnk', xb, gate_w) + gate_b).astype(f32))
    gates = gates.reshape(bsz, t, 2, 2, ch)
    log_a = -LRU_C * gates[:, :, :, 0] * jax.nn.softplus(-lru_l.astype(f32))
    bx = jnp.sqrt(-jnp.expm1(2.0 * log_a)) * gates[:, :, :, 1] * xc[:, :, None].astype(f32)
    if h0 is None:
        h0 = jnp.zeros((2, bsz, ch), f32)
    h, h_fin = _linear_scan(_dirs_to_scan(jnp.exp(log_a)), _dirs_to_scan(bx), h0)
    if not with_output:
        return None, h_fin
    h = _scan_to_dirs(h).sum(axis=2)
    return (jax.nn.gelu(zg.astype(f32)) * h).astype(xc.dtype), h_fin


def _token_mixers(z, rows, s_rwkv0, h_lru0, with_output, shift_mu, rw_w0, rw_w_up, rw_a0, rw_a_up,
                  rw_g_up, rw_k_k, rw_k_a, rw_r_k, rw_lnx, lru_conv_w, lru_conv_b, lru_gate_w,
                  lru_gate_b, lru_l, w_branch_a, w_branch_b, w_out):
    bsz, t, _ = z.shape
    zr = z[..., :RWKV_COLS]
    zx = z[..., OFF_LX:OFF_LG]
    zg = None
    if rows is None:
        zr = _shift_centred(zr, shift_mu)
        xc = _dwconv(zx, lru_conv_w, lru_conv_b)
        if with_output:
            zg = z[..., OFF_LG:OFF_GATE]
    else:
        zr = _shift_centred(zr.reshape(bsz, rows, GRID_W, RWKV_COLS), shift_mu).reshape(bsz, t, RWKV_COLS)
        to_cols = lambda u: jnp.swapaxes(u.reshape(bsz, rows, GRID_W, u.shape[-1]), 1, 2)
        xc = _dwconv(to_cols(zx), lru_conv_w, lru_conv_b).reshape(bsz, t, LRU_WIDTH)
        zg = to_cols(z[..., OFF_LG:OFF_GATE]).reshape(bsz, t, LRU_WIDTH)
    y_a, s_rwkv = _rwkv_branch(zr, s_rwkv0, with_output, rw_w0, rw_w_up, rw_a0, rw_a_up, rw_g_up,
                               rw_k_k, rw_k_a, rw_r_k, rw_lnx)
    y_b, h_lru = _rglru_branch(xc, zg, h_lru0, with_output, lru_gate_w, lru_gate_b, lru_l)
    if not with_output:
        return None, s_rwkv, h_lru
    if rows is not None:
        y_b = jnp.swapaxes(y_b.reshape(bsz, GRID_W, rows, LRU_WIDTH), 1, 2).reshape(bsz, t, LRU_WIDTH)
    gates = jax.nn.sigmoid(z[..., OFF_GATE:IN_COLS])
    mixed = gates[..., :D_MODEL] * (y_a @ w_branch_a) + gates[..., D_MODEL:] * (y_b @ w_branch_b)
    return mixed @ w_out, s_rwkv, h_lru


def _grouped_experts(x2, top_idx, gate, w_gu, w_down):
    m, d = x2.shape
    n_assign = m * TOP_K
    flat_e = top_idx.reshape(-1)
    flat_tok = jnp.arange(n_assign, dtype=jnp.int32) // TOP_K
    order = jnp.argsort(flat_e)
    sorted_e = flat_e[order]
    counts = jnp.bincount(flat_e, length=N_EXPERTS)
    padded = (counts + MOE_BLOCK - 1) // MOE_BLOCK * MOE_BLOCK
    pad_end = jnp.cumsum(padded)
    pad_start = pad_end - padded
    start = jnp.cumsum(counts) - counts
    dest = pad_start[sorted_e] + jnp.arange(n_assign, dtype=jnp.int32) - start[sorted_e]
    n_blocks = (n_assign + N_EXPERTS * (MOE_BLOCK - 1) + MOE_BLOCK - 1) // MOE_BLOCK
    p_rows = n_blocks * MOE_BLOCK
    buf_tok = jnp.full((p_rows,), m, jnp.int32).at[dest].set(flat_tok[order])
    buf_gate = jnp.zeros((p_rows,), gate.dtype).at[dest].set(gate.reshape(-1)[order])
    blk_e = jnp.minimum(jnp.searchsorted(pad_end, jnp.arange(n_blocks) * MOE_BLOCK, side='right'),
                        N_EXPERTS - 1)
    x_pad = jnp.concatenate([x2, jnp.zeros((1, d), x2.dtype)], axis=0)

    def block(args):
        tok, g, e = args
        return _swiglu(x_pad[tok], w_gu[e], w_down[e]) * g[:, None]

    y = lax.map(block, (buf_tok.reshape(n_blocks, MOE_BLOCK), buf_gate.reshape(n_blocks, MOE_BLOCK), blk_e))
    out = jnp.zeros((m + 1, d), y.dtype).at[buf_tok].add(y.reshape(p_rows, d))
    return out[:m]


def _moe_ffn(u, router_w, router_b, ex_w_gu, ex_w_down, sh_w_gu, sh_w_down):
    shape = u.shape
    x2 = u.reshape(-1, D_MODEL)
    m = x2.shape[0]
    scores = jax.nn.sigmoid((x2 @ router_w).astype(jnp.float32))
    biased = scores + router_b
    grp_score = lax.top_k(biased.reshape(m, N_GROUPS, N_EXPERTS // N_GROUPS), 2)[0].sum(-1)
    _, grp_idx = lax.top_k(grp_score, TOPK_GROUPS)
    grp_mask = jnp.any(grp_idx[:, :, None] == jnp.arange(N_GROUPS)[None, None, :], axis=1)
    masked = jnp.where(jnp.repeat(grp_mask, N_EXPERTS // N_GROUPS, axis=1), biased, -jnp.inf)
    _, top_idx = lax.top_k(masked, TOP_K)
    top_s = jnp.take_along_axis(scores, top_idx, axis=1)
    gate = (top_s / jnp.sum(top_s, axis=-1, keepdims=True) * ROUTED_SCALE).astype(u.dtype)
    routed = _grouped_experts(x2, top_idx, gate, ex_w_gu, ex_w_down)
    return (routed + _swiglu(x2, sh_w_gu, sh_w_down)).reshape(shape)


def setup_inputs(seed: int = 0) -> dict:
    key = jax.random.key(seed)
    ks = iter(jax.random.split(key, 48))
    f32 = jnp.float32
    nrm = lambda shape, s: jax.random.normal(next(ks), shape, f32) * s
    uni = lambda shape, lo, hi: jax.random.uniform(next(ks), shape, f32, lo, hi)
    d = D_MODEL
    x = nrm((BATCH, SEQ, d), 1.0)
    c = nrm((BATCH, d), 1.0)
    ctx = nrm((BATCH, CTX_LEN, d), 1.0)
    c_ctx = nrm((d,), 1.0)
    w_mod = nrm((DEPTH, d, 6 * d), 0.5 * d ** -0.5)
    b_mod = nrm((DEPTH, 6 * d), 0.02)
    norm_g = 1.0 + nrm((DEPTH, 4, d), 0.05)
    w_in = nrm((DEPTH, d, IN_COLS), d ** -0.5)
    shift_mu = uni((DEPTH, 2, RWKV_COLS), 0.0, 0.5)
    rw_w0 = uni((DEPTH, 2, RWKV_WIDTH), -6.0, -1.0)
    rw_w_up = nrm((DEPTH, 2, DECAY_LORA, RWKV_WIDTH), 0.1 * DECAY_LORA ** -0.5)
    rw_a0 = nrm((DEPTH, 2, RWKV_WIDTH), 0.1)
    rw_a_up = nrm((DEPTH, 2, AAA_LORA, RWKV_WIDTH), 0.5 * AAA_LORA ** -0.5)
    rw_g_up = nrm((DEPTH, GATE_LORA, RWKV_WIDTH), GATE_LORA ** -0.5)
    rw_k_k = 0.85 + nrm((DEPTH, RWKV_WIDTH), 0.05)
    rw_k_a = 1.0 + nrm((DEPTH, RWKV_WIDTH), 0.05)
    rw_r_k = nrm((DEPTH, RWKV_HEADS, RWKV_HEAD), 0.1)
    rw_lnx = jnp.stack([1.0 + nrm((DEPTH, RWKV_WIDTH), 0.05), nrm((DEPTH, RWKV_WIDTH), 0.02)], axis=1)
    lru_conv_w = nrm((DEPTH, CONV_W, LRU_WIDTH), 0.5)
    lru_conv_b = nrm((DEPTH, LRU_WIDTH), 0.02)
    lru_gate_w = nrm((DEPTH, 2, 2, LRU_BLOCKS, LRU_BLOCK, LRU_BLOCK), LRU_BLOCK ** -0.5)
    lru_gate_b = nrm((DEPTH, 2, 2, LRU_BLOCKS, LRU_BLOCK), 0.02)
    sig = uni((DEPTH, 2, LRU_WIDTH), 0.9, 0.999) ** (1.0 / LRU_C)
    lru_l = jnp.log(sig) - jnp.log1p(-sig)
    w_branch_a = nrm((DEPTH, RWKV_WIDTH, d), RWKV_WIDTH ** -0.5)
    w_branch_b = nrm((DEPTH, LRU_WIDTH, d), LRU_WIDTH ** -0.5)
    w_out = nrm((DEPTH, d, d), d ** -0.5)
    router_w = nrm((DEPTH, d, N_EXPERTS), d ** -0.5)
    router_b = nrm((DEPTH, N_EXPERTS), 0.01)
    ex_w_gu = nrm((DEPTH, N_EXPERTS, d, 2 * EXPERT_FF), d ** -0.5)
    ex_w_down = nrm((DEPTH, N_EXPERTS, EXPERT_FF, d), EXPERT_FF ** -0.5)
    sh_w_gu = nrm((DEPTH, d, 2 * SHARED_FF), d ** -0.5)
    sh_w_down = nrm((DEPTH, SHARED_FF, d), SHARED_FF ** -0.5)
    return {'x': x, 'c': c, 'ctx': ctx, 'c_ctx': c_ctx, 'w_mod': w_mod, 'b_mod': b_mod,
            'norm_g': norm_g, 'w_in': w_in, 'shift_mu': shift_mu, 'rw_w0': rw_w0, 'rw_w_up': rw_w_up,
            'rw_a0': rw_a0, 'rw_a_up': rw_a_up, 'rw_g_up': rw_g_up, 'rw_k_k': rw_k_k, 'rw_k_a': rw_k_a,
            'rw_r_k': rw_r_k, 'rw_lnx': rw_lnx, 'lru_conv_w': lru_conv_w, 'lru_conv_b': lru_conv_b,
            'lru_gate_w': lru_gate_w, 'lru_gate_b': lru_gate_b, 'lru_l': lru_l,
            'w_branch_a': w_branch_a, 'w_branch_b': w_branch_b, 'w_out': w_out,
            'router_w': router_w, 'router_b': router_b, 'ex_w_gu': ex_w_gu, 'ex_w_down': ex_w_down,
            'sh_w_gu': sh_w_gu, 'sh_w_down': sh_w_down}


def reference(x, c, ctx, c_ctx, w_mod, b_mod, norm_g, w_in, shift_mu, rw_w0, rw_w_up, rw_a0, rw_a_up,
              rw_g_up, rw_k_k, rw_k_a, rw_r_k, rw_lnx, lru_conv_w, lru_conv_b, lru_gate_w, lru_gate_b,
              lru_l, w_branch_a, w_branch_b, w_out, router_w, router_b, ex_w_gu, ex_w_down,
              sh_w_gu, sh_w_down):
    rows = x.shape[1] // GRID_W
    h = x
    hc = ctx
    for l in range(DEPTH):
        last = l == DEPTH - 1
        mod = (jax.nn.silu(c) @ w_mod[l] + b_mod[l])[:, None, :]
        mod_c = jax.nn.silu(c_ctx) @ w_mod[l] + b_mod[l]
        sh1, sc1, gt1, sh2, sc2, gt2 = jnp.split(mod, 6, axis=-1)
        csh1, csc1, cgt1, csh2, csc2, cgt2 = jnp.split(mod_c, 6, axis=-1)
        g_pre1, g_post1, g_pre2, g_post2 = norm_g[l]
        mix_p = (shift_mu[l], rw_w0[l], rw_w_up[l], rw_a0[l], rw_a_up[l], rw_g_up[l], rw_k_k[l],
                 rw_k_a[l], rw_r_k[l], rw_lnx[l], lru_conv_w[l], lru_conv_b[l], lru_gate_w[l],
                 lru_gate_b[l], lru_l[l], w_branch_a[l], w_branch_b[l], w_out[l])
        ffn_p = (router_w[l], router_b[l], ex_w_gu[l], ex_w_down[l], sh_w_gu[l], sh_w_down[l])
        zc = _modulate(_rmsnorm(hc, g_pre1), csh1, csc1) @ w_in[l]
        yc, s_rwkv, h_lru = _token_mixers(zc, None, None, None, not last, *mix_p)
        z = _modulate(_rmsnorm(h, g_pre1), sh1, sc1) @ w_in[l]
        y, _, _ = _token_mixers(z, rows, s_rwkv, h_lru, True, *mix_p)
        h = h + gt1 * _rmsnorm(y, g_post1)
        u = _modulate(_rmsnorm(h, g_pre2), sh2, sc2)
        h = h + gt2 * _rmsnorm(_moe_ffn(u, *ffn_p), g_post2)
        if not last:
            hc = hc + cgt1 * _rmsnorm(yc, g_post1)
            uc = _modulate(_rmsnorm(hc, g_pre2), csh2, csc2)
            hc = hc + cgt2 * _rmsnorm(_moe_ffn(uc, *ffn_p), g_post2)
    return h
```

```python
import functools

import jax
import jax.numpy as jnp
from jax import lax
from jax.experimental import pallas as pl
from jax.experimental.pallas import tpu as pltpu

GRID_W = 64
NORM_EPS = 1e-6
RWKV_HEAD = 64
DECAY_LORA = 64
AAA_LORA = 64
GATE_LORA = 160
LNX_EPS = 64e-5
LRU_BLOCKS = 16
CONV_LEFT = 2
LRU_C = 8.0
N_GROUPS = 8
TOPK_GROUPS = 4
TOP_K = 6
ROUTED_SCALE = 2.5

LANE = 128
MOE_ROWS = 256
VMEM_LIMIT = 48 * 1024 * 1024


def _mm_kernel(a_ref, b_ref, o_ref):
    o_ref[...] = jnp.dot(a_ref[...], b_ref[...],
                         preferred_element_type=jnp.float32).astype(o_ref.dtype)


def _mm(a, b, tm=512, tn=512, out_dtype=jnp.float32):
    m, k = a.shape
    _, n = b.shape
    tm = min(tm, m)
    tn = min(tn, n)
    assert m % tm == 0 and n % tn == 0, (m, n, tm, tn)
    return pl.pallas_call(
        _mm_kernel,
        grid=(m // tm, n // tn),
        in_specs=[pl.BlockSpec((tm, k), lambda i, j: (i, 0)),
                  pl.BlockSpec((k, tn), lambda i, j: (0, j))],
        out_specs=pl.BlockSpec((tm, tn), lambda i, j: (i, j)),
        out_shape=jax.ShapeDtypeStruct((m, n), out_dtype),
        compiler_params=pltpu.CompilerParams(
            dimension_semantics=("parallel", "parallel"), vmem_limit_bytes=VMEM_LIMIT),
        name="dense_mm",
    )(a.astype(jnp.bfloat16), b.astype(jnp.bfloat16))


def _rwkv_scan_kernel(r_ref, w_ref, k_ref, v_ref, nkk_ref, b_ref, y_ref, s_ref, sa_ref):
    n = s_ref.shape[0]
    tb = r_ref.shape[0]

    @pl.when(pl.program_id(1) == 0)
    def _():
        s_ref[...] = jnp.zeros_like(s_ref)
        sa_ref[...] = jnp.zeros_like(sa_ref)

    def step(t, sa):
        vv = v_ref[t]
        y = jnp.zeros_like(vv)
        acc = jnp.zeros_like(vv)
        for k in range(n):
            row = pl.ds(k, 1)
            sk = s_ref[k] * w_ref[t, row, :] + (sa * b_ref[t, row, :] + vv * k_ref[t, row, :])
            s_ref[k] = sk
            y = y + sk * r_ref[t, row, :]
            acc = acc + sk * nkk_ref[t, row, :]
        y_ref[t] = y
        return acc

    sa_ref[...] = lax.fori_loop(0, tb, step, sa_ref[...])


def _rwkv_scan(r, w, k, v, nkk_next, b, tb=16):
    s, n, lanes = r.shape
    assert s % tb == 0 and lanes % LANE == 0
    spec = pl.BlockSpec((tb, n, LANE), lambda j, i: (i, 0, j))
    return pl.pallas_call(
        _rwkv_scan_kernel,
        grid=(lanes // LANE, s // tb),
        in_specs=[spec] * 6,
        out_specs=spec,
        out_shape=jax.ShapeDtypeStruct((s, n, lanes), jnp.float32),
        scratch_shapes=[pltpu.VMEM((n, n, LANE), jnp.float32),
                        pltpu.VMEM((n, LANE), jnp.float32)],
        compiler_params=pltpu.CompilerParams(
            dimension_semantics=("parallel", "arbitrary"), vmem_limit_bytes=VMEM_LIMIT),
        name="rwkv_scan",
    )(r, w, k, v, nkk_next, b)


def _lru_scan_kernel(a_ref, bx_ref, h_ref, st_ref):
    tb = a_ref.shape[0]

    @pl.when(pl.program_id(0) == 0)
    def _():
        st_ref[...] = jnp.zeros_like(st_ref)

    def step(t, h):
        h = a_ref[t] * h + bx_ref[t]
        h_ref[t] = h
        return h

    st_ref[...] = lax.fori_loop(0, tb, step, st_ref[...])


def _lru_scan(a, bx, tb=32):
    s, rws, c = a.shape
    assert s % tb == 0
    spec = pl.BlockSpec((tb, rws, c), lambda i: (i, 0, 0))
    return pl.pallas_call(
        _lru_scan_kernel,
        grid=(s // tb,),
        in_specs=[spec, spec],
        out_specs=spec,
        out_shape=jax.ShapeDtypeStruct((s, rws, c), jnp.float32),
        scratch_shapes=[pltpu.VMEM((rws, c), jnp.float32)],
        compiler_params=pltpu.CompilerParams(
            dimension_semantics=("arbitrary",), vmem_limit_bytes=VMEM_LIMIT),
        name="lru_scan",
    )(a, bx)


def _experts_kernel(blk_e_ref, x_ref, wgu_ref, wdn_ref, o_ref):
    del blk_e_ref
    ff = wdn_ref.shape[1]
    gu = jnp.dot(x_ref[...], wgu_ref[0], preferred_element_type=jnp.float32)
    g = gu[:, :ff]
    u = gu[:, ff:]
    act = (g * jax.nn.sigmoid(g) * u).astype(jnp.bfloat16)
    o_ref[...] = jnp.dot(act, wdn_ref[0], preferred_element_type=jnp.float32)


def _grouped_experts(xs, blk_e, w_gu, w_down):
    p_rows, d = xs.shape
    n_e, _, ff2 = w_gu.shape
    ff = ff2 // 2
    n_blocks = p_rows // MOE_ROWS
    grid_spec = pltpu.PrefetchScalarGridSpec(
        num_scalar_prefetch=1,
        grid=(n_blocks,),
        in_specs=[pl.BlockSpec((MOE_ROWS, d), lambda i, be: (i, 0)),
                  pl.BlockSpec((1, d, ff2), lambda i, be: (be[i], 0, 0)),
                  pl.BlockSpec((1, ff, d), lambda i, be: (be[i], 0, 0))],
        out_specs=pl.BlockSpec((MOE_ROWS, d), lambda i, be: (i, 0)),
    )
    return pl.pallas_call(
        _experts_kernel,
        grid_spec=grid_spec,
        out_shape=jax.ShapeDtypeStruct((p_rows, d), jnp.float32),
        compiler_params=pltpu.CompilerParams(
            dimension_semantics=("arbitrary",), vmem_limit_bytes=VMEM_LIMIT),
        name="moe_experts",
    )(blk_e, xs.astype(jnp.bfloat16), w_gu.astype(jnp.bfloat16), w_down.astype(jnp.bfloat16))


def _rmsnorm(x, g):
    return x * lax.rsqrt(jnp.mean(x * x, axis=-1, keepdims=True) + NORM_EPS) * g


def _shift_centred(z, mu):
    pad = [(0, 0)] * (z.ndim - 2) + [(1, 1), (0, 0)]
    zp = jnp.pad(z, pad)
    return z + mu[0] * (zp[..., :-2, :] - z) + mu[1] * (zp[..., 2:, :] - z)


def _dwconv(z, w, b):
    taps = w.shape[0]
    t = z.shape[-2]
    pad = [(0, 0)] * (z.ndim - 2) + [(CONV_LEFT, taps - 1 - CONV_LEFT), (0, 0)]
    zp = jnp.pad(z, pad)
    y = b
    for j in range(taps):
        y = y + w[j] * lax.slice_in_dim(zp, j, j + t, axis=z.ndim - 2)
    return y


def _rwkv_prep(zr, w0, w_up, a0, a_up, k_k, k_a):
    bsz, t, _ = zr.shape
    cw = w0.shape[-1]
    nh = cw // RWKV_HEAD
    r = zr[..., 0:cw]
    k = zr[..., cw:2 * cw]
    v = zr[..., 2 * cw:3 * cw]
    off = 3 * cw
    wd = zr[..., off:off + 2 * DECAY_LORA].reshape(bsz, t, 2, DECAY_LORA)
    ad = zr[..., off + 2 * DECAY_LORA:off + 2 * DECAY_LORA + 2 * AAA_LORA].reshape(bsz, t, 2, AAA_LORA)
    hp = lax.Precision.HIGHEST
    w_log = -jax.nn.softplus(-(w0 + jnp.einsum('btdr,drc->btdc', jnp.tanh(wd), w_up, precision=hp))) - 0.5
    decay = jnp.exp(-jnp.exp(w_log))
    a = jax.nn.sigmoid(a0 + jnp.einsum('btdr,drc->btdc', ad, a_up, precision=hp))
    kk = (k * k_k).reshape(bsz, t, nh, RWKV_HEAD)
    kk = kk / jnp.maximum(jnp.sqrt(jnp.sum(kk * kk, axis=-1, keepdims=True)), 1e-12)
    kk = kk.reshape(bsz, t, cw)
    k_dir = k[:, :, None] * (1.0 + (a - 1.0) * k_a)
    b = kk[:, :, None] * a
    return r, decay, k_dir, v, kk, b


def _seq_shared(u_ctx, u_lat):
    s0 = jnp.concatenate([u_ctx, u_lat], axis=1)
    s1 = jnp.concatenate([jnp.flip(u_ctx, 1), jnp.flip(u_lat, 1)], axis=1)
    return jnp.stack([s0, s1], axis=0)


def _seq_dirs(u_ctx, u_lat):
    s0 = jnp.concatenate([u_ctx[:, :, 0], u_lat[:, :, 0]], axis=1)
    s1 = jnp.concatenate([jnp.flip(u_ctx[:, :, 1], 1), jnp.flip(u_lat[:, :, 1], 1)], axis=1)
    return jnp.stack([s0, s1], axis=0)


def _to_scan_lanes(u):
    d, bsz, s, cw = u.shape
    nh = cw // RWKV_HEAD
    u = u.reshape(d, bsz, s, nh, RWKV_HEAD)
    return jnp.transpose(u, (2, 4, 0, 1, 3)).reshape(s, RWKV_HEAD, d * bsz * nh)


def kernel(x, c, ctx, c_ctx, w_mod, b_mod, norm_g, w_in, shift_mu, rw_w0, rw_w_up, rw_a0, rw_a_up,
           rw_g_up, rw_k_k, rw_k_a, rw_r_k, rw_lnx, lru_conv_w, lru_conv_b, lru_gate_w, lru_gate_b,
           lru_l, w_branch_a, w_branch_b, w_out, router_w, router_b, ex_w_gu, ex_w_down,
           sh_w_gu, sh_w_down):
    l = 0
    f32 = jnp.float32
    hp = lax.Precision.HIGHEST
    bsz, t, d = x.shape
    tc = ctx.shape[1]
    rows = t // GRID_W
    m = bsz * t
    cw = rw_w0.shape[-1]
    cl = lru_l.shape[-1]
    nh = cw // RWKV_HEAD
    rwkv_cols = 3 * cw + 2 * DECAY_LORA + 2 * AAA_LORA + GATE_LORA
    lora_cols = rwkv_cols - 3 * cw
    lora_pad = -lora_cols % LANE
    off_lx = rwkv_cols
    off_lg = off_lx + cl
    off_gate = off_lg + cl

    mod = jnp.dot(jax.nn.silu(c), w_mod[l], precision=hp) + b_mod[l]
    mod_c = jnp.dot(jax.nn.silu(c_ctx), w_mod[l], precision=hp) + b_mod[l]
    sh1, sc1, gt1, sh2, sc2, gt2 = [u[:, None, :] for u in jnp.split(mod, 6, axis=-1)]
    csh1, csc1 = mod_c[0:d], mod_c[d:2 * d]
    g_pre1, g_post1, g_pre2, g_post2 = norm_g[l]

    wi = w_in[l]
    w_al = jnp.concatenate([wi[:, :rwkv_cols], jnp.zeros((d, lora_pad), f32), wi[:, off_lx:]], axis=1)
    mu_al = jnp.concatenate([shift_mu[l], jnp.zeros((2, lora_pad), f32)], axis=1)
    zr_w = rwkv_cols + lora_pad
    a_lx = zr_w
    a_lg = a_lx + cl
    a_gate = a_lg + cl

    xm_c = (_rmsnorm(ctx, g_pre1) * (1.0 + csc1) + csh1).reshape(bsz * tc, d)
    xm = (_rmsnorm(x, g_pre1) * (1.0 + sc1) + sh1).reshape(m, d)
    zc = _mm(xm_c, w_al[:, :a_lg]).reshape(bsz, tc, a_lg)
    z = _mm(xm, w_al, tn=768).reshape(bsz, t, -1)

    zr_c = _shift_centred(zc[..., :zr_w], mu_al)
    zr = _shift_centred(z[..., :zr_w].reshape(bsz, rows, GRID_W, zr_w), mu_al).reshape(bsz, t, zr_w)
    prm = (rw_w0[l], rw_w_up[l], rw_a0[l], rw_a_up[l], rw_k_k[l], rw_k_a[l])
    pc = _rwkv_prep(zr_c, *prm)
    pz = _rwkv_prep(zr, *prm)
    r_lat, _, kdir_lat, v_lat = pz[0], pz[1], pz[2], pz[3]
    seq = [_seq_shared(pc[0], pz[0]), _seq_dirs(pc[1], pz[1]), _seq_dirs(pc[2], pz[2]),
           _seq_shared(pc[3], pz[3]), _seq_shared(pc[4], pz[4]), _seq_dirs(pc[5], pz[5])]
    r_s, w_s, k_s, v_s, kk_s, b_s = [_to_scan_lanes(u) for u in seq]
    nkk_next = jnp.concatenate([-kk_s[1:], jnp.zeros_like(kk_s[:1])], axis=0)
    y_s = _rwkv_scan(r_s, w_s, k_s, v_s, nkk_next, b_s)
    y_s = y_s[tc:].reshape(t, RWKV_HEAD, 2, bsz, nh)
    y_h = y_s[:, :, 0] + jnp.flip(y_s[:, :, 1], axis=0)
    y_h = jnp.transpose(y_h, (2, 0, 3, 1))
    mu_y = jnp.mean(y_h, axis=-1, keepdims=True)
    var_y = jnp.mean(jnp.square(y_h - mu_y), axis=-1, keepdims=True)
    lnx = rw_lnx[l]
    y_a = ((y_h - mu_y) * lax.rsqrt(var_y + LNX_EPS)).reshape(bsz, t, cw) * lnx[0] + lnx[1]
    heads = lambda u: u.reshape(u.shape[:-1] + (nh, RWKV_HEAD))
    bonus = jnp.sum(heads(r_lat)[:, :, None] * heads(kdir_lat) * rw_r_k[l], axis=(2, 4))
    y_a = y_a + (bonus[..., None] * heads(v_lat)).reshape(bsz, t, cw)
    gd = zr[..., 3 * cw + 2 * DECAY_LORA + 2 * AAA_LORA:rwkv_cols]
    g_out = jnp.dot(jax.nn.sigmoid(gd), rw_g_up[l], precision=hp)
    y_a = y_a * g_out

    to_cols = lambda u: jnp.swapaxes(u.reshape(bsz, rows, GRID_W, u.shape[-1]), 1, 2)
    xc_c = _dwconv(zc[..., a_lx:a_lg], lru_conv_w[l], lru_conv_b[l])
    xc = _dwconv(to_cols(z[..., a_lx:a_lg]), lru_conv_w[l], lru_conv_b[l]).reshape(bsz, t, cl)
    zg = to_cols(z[..., a_lg:a_gate]).reshape(bsz, t, cl)
    blk = cl // LRU_BLOCKS
    sp_l = jax.nn.softplus(-lru_l[l])

    def lru_inputs(xcv):
        tt = xcv.shape[1]
        xb = xcv.reshape(bsz, tt, LRU_BLOCKS, blk)
        gts = jax.nn.sigmoid(jnp.einsum('btnj,dgnjk->btdgnk', xb, lru_gate_w[l], precision=hp)
                             + lru_gate_b[l]).reshape(bsz, tt, 2, 2, cl)
        log_a = -LRU_C * gts[:, :, :, 0] * sp_l
        bxv = jnp.sqrt(-jnp.expm1(2.0 * log_a)) * gts[:, :, :, 1] * xcv[:, :, None]
        return jnp.exp(log_a), bxv

    a_c, bx_c = lru_inputs(xc_c)
    a_z, bx_z = lru_inputs(xc)
    s_len = tc + t
    to_steps = lambda u: jnp.transpose(u, (2, 0, 1, 3)).reshape(s_len, 2 * bsz, cl)
    h_s = _lru_scan(to_steps(_seq_dirs(a_c, a_z)), to_steps(_seq_dirs(bx_c, bx_z)))
    h_s = h_s[tc:].reshape(t, 2, bsz, cl)
    h_sum = jnp.transpose(h_s[:, 0] + jnp.flip(h_s[:, 1], axis=0), (1, 0, 2))
    y_b = jax.nn.gelu(zg) * h_sum
    y_b = jnp.swapaxes(y_b.reshape(bsz, GRID_W, rows, cl), 1, 2).reshape(bsz, t, cl)

    gates = jax.nn.sigmoid(z[..., a_gate:])
    pa = _mm(y_a.reshape(m, cw), w_branch_a[l]).reshape(bsz, t, d)
    pb = _mm(y_b.reshape(m, cl), w_branch_b[l]).reshape(bsz, t, d)
    mixed = gates[..., :d] * pa + gates[..., d:] * pb
    y = _mm(mixed.reshape(m, d), w_out[l]).reshape(bsz, t, d)
    h = x + gt1 * _rmsnorm(y, g_post1)

    u2 = (_rmsnorm(h, g_pre2) * (1.0 + sc2) + sh2).reshape(m, d)
    n_e = router_w.shape[-1]
    per_g = n_e // N_GROUPS
    scores = jax.nn.sigmoid(jnp.dot(u2, router_w[l], precision=hp))
    biased = scores + router_b[l]
    grp_score = lax.top_k(biased.reshape(m, N_GROUPS, per_g), 2)[0].sum(-1)
    _, grp_idx = lax.top_k(grp_score, TOPK_GROUPS)
    grp_mask = jnp.any(grp_idx[:, :, None] == jnp.arange(N_GROUPS)[None, None, :], axis=1)
    masked = jnp.where(jnp.repeat(grp_mask, per_g, axis=1), biased, -jnp.inf)
    _, top_idx = lax.top_k(masked, TOP_K)
    top_s = jnp.take_along_axis(scores, top_idx, axis=1)
    gate = top_s / jnp.sum(top_s, axis=-1, keepdims=True) * ROUTED_SCALE

    n_assign = m * TOP_K
    flat_e = top_idx.reshape(-1).astype(jnp.int32)
    flat_tok = jnp.arange(n_assign, dtype=jnp.int32) // TOP_K
    order = jnp.argsort(flat_e)
    sorted_e = flat_e[order]
    counts = jnp.bincount(flat_e, length=n_e)
    padded = (counts + MOE_ROWS - 1) // MOE_ROWS * MOE_ROWS
    pad_end = jnp.cumsum(padded)
    pad_start = pad_end - padded
    start = jnp.cumsum(counts) - counts
    dest = (pad_start[sorted_e] + jnp.arange(n_assign, dtype=jnp.int32) - start[sorted_e]).astype(jnp.int32)
    n_blocks = (n_assign + n_e * (MOE_ROWS - 1) + MOE_ROWS - 1) // MOE_ROWS
    p_rows = n_blocks * MOE_ROWS
    buf_tok = jnp.full((p_rows,), m, jnp.int32).at[dest].set(flat_tok[order])
    blk_e = jnp.minimum(jnp.searchsorted(pad_end, jnp.arange(n_blocks) * MOE_ROWS, side='right'),
                        n_e - 1).astype(jnp.int32)
    x_pad = jnp.concatenate([u2.astype(jnp.bfloat16), jnp.zeros((1, d), jnp.bfloat16)], axis=0)
    ys = _grouped_experts(x_pad[buf_tok], blk_e, ex_w_gu[l], ex_w_down[l])
    pos = jnp.zeros((n_assign,), jnp.int32).at[order].set(dest)
    routed = jnp.sum(ys[pos].reshape(m, TOP_K, d) * gate[:, :, None], axis=1)
    gu = _mm(u2, sh_w_gu[l])
    ff = sh_w_down.shape[1]
    shared = _mm(jax.nn.silu(gu[:, :ff]) * gu[:, ff:], sh_w_down[l])
    moe = (routed + shared).reshape(bsz, t, d)
    return h + gt2 * _rmsnorm(moe, g_post2)
```

```python
import functools

import jax
import jax.numpy as jnp
from jax import lax
from jax.experimental import pallas as pl
from jax.experimental.pallas import tpu as pltpu

GRID_W = 64
NORM_EPS = 1e-6
RWKV_HEAD = 64
DECAY_LORA = 64
AAA_LORA = 64
GATE_LORA = 160
LNX_EPS = 64e-5
LRU_BLOCKS = 16
CONV_LEFT = 2
LRU_C = 8.0
N_GROUPS = 8
TOPK_GROUPS = 4
TOP_K = 6
ROUTED_SCALE = 2.5

LANE = 128
SUBLANE = 8
MXU_N = 256
MOE_ROWS = 256
VMEM_LIMIT = 48 * 1024 * 1024
HP = lax.Precision.HIGHEST
F32 = jnp.float32
BF16 = jnp.bfloat16


def _cparams(*sem):
    return pltpu.CompilerParams(dimension_semantics=sem, vmem_limit_bytes=VMEM_LIMIT)


def _mm_kernel(a_ref, b_ref, o_ref):
    o_ref[...] = jnp.dot(a_ref[...], b_ref[...], preferred_element_type=F32).astype(o_ref.dtype)


def _mm(a, b, tm=512, tn=512, out_dtype=F32):
    m, k = a.shape
    _, n = b.shape
    tm = min(tm, m)
    tn = min(tn, n)
    assert m % tm == 0 and n % tn == 0, (m, n, tm, tn)
    return pl.pallas_call(
        _mm_kernel,
        grid=(m // tm, n // tn),
        in_specs=[pl.BlockSpec((tm, k), lambda i, j: (i, 0)),
                  pl.BlockSpec((k, tn), lambda i, j: (0, j))],
        out_specs=pl.BlockSpec((tm, tn), lambda i, j: (i, j)),
        out_shape=jax.ShapeDtypeStruct((m, n), out_dtype),
        compiler_params=_cparams("parallel", "parallel"),
        name="dense_mm",
    )(a.astype(BF16), b.astype(BF16))


def _inproj_kernel(x_ref, scp_ref, sh_ref, g_ref, w_ref, o_ref, xm_ref):
    @pl.when(pl.program_id(1) == 0)
    def _():
        x = x_ref[...]
        ms = jnp.mean(x * x, axis=-1, keepdims=True)
        xm = x * lax.rsqrt(ms + NORM_EPS) * g_ref[...] * scp_ref[...] + sh_ref[...]
        xm_ref[...] = xm.reshape(xm_ref.shape).astype(BF16)

    o_ref[...] = jnp.dot(xm_ref[...], w_ref[...], preferred_element_type=F32)


def _inproj(x_tm, scp, sh, g, w, tt=128, tn=1024):
    t, bsz, d = x_tm.shape
    n = w.shape[1]
    tt = min(tt, t)
    assert t % tt == 0 and n % tn == 0
    return pl.pallas_call(
        _inproj_kernel,
        grid=(t // tt, n // tn),
        in_specs=[pl.BlockSpec((tt, bsz, d), lambda i, j: (i, 0, 0)),
                  pl.BlockSpec((bsz, d), lambda i, j: (0, 0)),
                  pl.BlockSpec((bsz, d), lambda i, j: (0, 0)),
                  pl.BlockSpec((1, d), lambda i, j: (0, 0)),
                  pl.BlockSpec((d, tn), lambda i, j: (0, j))],
        out_specs=pl.BlockSpec((tt * bsz, tn), lambda i, j: (i, j)),
        out_shape=jax.ShapeDtypeStruct((t * bsz, n), F32),
        scratch_shapes=[pltpu.VMEM((tt * bsz, d), BF16)],
        compiler_params=_cparams("parallel", "arbitrary"),
        name="in_proj",
    )(x_tm, scp, sh, g, w)


def _lru_kernel(z_ref, cw_ref, cb_ref, gw_ref, gb_ref, sp_ref, h0_ref, h_ref, hf_ref,
                a_ref, bx_ref):
    d = pl.program_id(0)
    w = pl.program_id(2)
    nr, _, bsz, cq = z_ref.shape
    taps = cw_ref.shape[0]

    @pl.when(w == 0)
    def _():
        hf_ref[0] = h0_ref[0]

    zx = z_ref[:, 0]
    xc = jnp.zeros_like(zx) + cb_ref[...]
    for j in range(taps):
        off = j - CONV_LEFT
        if off < 0:
            sh = jnp.concatenate([jnp.zeros((-off, bsz, cq), F32), zx[:nr + off]], axis=0)
        elif off > 0:
            sh = jnp.concatenate([zx[off:], jnp.zeros((off, bsz, cq), F32)], axis=0)
        else:
            sh = zx
        xc = xc + cw_ref[j:j + 1, :] * sh
    x2 = xc.reshape(nr * bsz, cq)
    xb = x2.astype(BF16)
    pre = []
    for g in range(2):
        cols = []
        for q in range(cq // MXU_N):
            cols.append(jnp.dot(xb[:, q * MXU_N:(q + 1) * MXU_N], gw_ref[0, g, q],
                                preferred_element_type=F32))
        pre.append(jnp.concatenate(cols, axis=1) + gb_ref[0, g:g + 1, :])
    rec = jax.nn.sigmoid(pre[0])
    inp = jax.nn.sigmoid(pre[1])
    log_a = -LRU_C * rec * sp_ref[0]
    a = jnp.exp(log_a)
    bx = jnp.sqrt(-jnp.tanh(log_a) * (a * a + 1.0)) * inp * x2
    a_ref[...] = a.reshape(nr, bsz, cq)
    bx_ref[...] = bx.reshape(nr, bsz, cq)

    def step(i, h):
        r = i + d * (nr - 1 - 2 * i)
        h = a_ref[r] * h + bx_ref[r]
        h_ref[0, r, 0] = h
        return h

    hf_ref[0] = lax.fori_loop(0, nr, step, hf_ref[0])


def _lru(z4, col_blk, conv_w, conv_b, gate_w, gate_b, sp_l, h0, cq=512):
    nr, nw, bsz, _ = z4.shape
    c = conv_w.shape[1]
    nq = c // cq
    cb0 = col_blk * (c // cq)

    def zmap(d, q, w):
        return (0, w + d * (nw - 1 - 2 * w), 0, cb0 + q)

    def hmap(d, q, w):
        return (d, 0, w + d * (nw - 1 - 2 * w), 0, q)

    return pl.pallas_call(
        _lru_kernel,
        grid=(2, nq, nw),
        in_specs=[pl.BlockSpec((nr, 1, bsz, cq), zmap),
                  pl.BlockSpec((conv_w.shape[0], cq), lambda d, q, w: (0, q)),
                  pl.BlockSpec((1, cq), lambda d, q, w: (0, q)),
                  pl.BlockSpec((1, 2, cq // MXU_N, MXU_N, MXU_N), lambda d, q, w: (d, 0, q, 0, 0)),
                  pl.BlockSpec((1, 2, cq), lambda d, q, w: (d, 0, q)),
                  pl.BlockSpec((1, 1, cq), lambda d, q, w: (d, 0, q)),
                  pl.BlockSpec((1, bsz, cq), lambda d, q, w: (d, 0, q))],
        out_specs=[pl.BlockSpec((1, nr, 1, bsz, cq), hmap),
                   pl.BlockSpec((1, bsz, cq), lambda d, q, w: (d, 0, q))],
        out_shape=[jax.ShapeDtypeStruct((2, nr, nw, bsz, c), F32),
                   jax.ShapeDtypeStruct((2, bsz, c), F32)],
        scratch_shapes=[pltpu.VMEM((nr, bsz, cq), F32), pltpu.VMEM((nr, bsz, cq), F32)],
        compiler_params=_cparams("arbitrary", "arbitrary", "arbitrary"),
        name="lru_fused",
    )(z4, conv_w, conv_b, gate_w, gate_b, sp_l, h0)


def _to_pairs(q, out_ref, lead=()):
    rows, c = q.shape
    npairs = rows // (2 * SUBLANE)
    lane = lax.broadcasted_iota(jnp.int32, (SUBLANE, LANE), 1)
    low = lane < RWKV_HEAD
    for p in range(npairs):
        ev = q[(2 * p) * SUBLANE:(2 * p + 1) * SUBLANE]
        od = q[(2 * p + 1) * SUBLANE:(2 * p + 2) * SUBLANE]
        pieces = []
        for j in range(c // LANE):
            e_j = ev[:, j * LANE:(j + 1) * LANE]
            o_j = od[:, j * LANE:(j + 1) * LANE]
            pieces.append(jnp.where(low, e_j, pltpu.roll(o_j, RWKV_HEAD, 1)))
            pieces.append(jnp.where(low, pltpu.roll(e_j, RWKV_HEAD, 1), o_j))
        x = jnp.concatenate(pieces, axis=0)
        out_ref[lead + (p,)] = x.T


def _group_sum(x, ones_ref):
    cols = [jnp.dot(x[:, q * MXU_N:(q + 1) * MXU_N], ones_ref[...], precision=HP,
                    preferred_element_type=F32) for q in range(x.shape[1] // MXU_N)]
    return jnp.concatenate(cols, axis=1)


def _prep_kernel(period, z_ref, zp_ref, zn_ref, mu_ref, w0_ref, wup_ref, a0_ref, aup_ref,
                 kk_ref, ka_ref, rk_ref, gup_ref, ones_ref,
                 r_o, v_o, kn_o, w_o, kd_o, b_o, bv_o, g_o):
    i = pl.program_id(0)
    rows, _ = z_ref.shape
    tt = rows // SUBLANE
    cw = w0_ref.shape[1]
    z = z_ref[...]
    has_prev = ((i * tt) % period != 0).astype(F32)
    has_next = (((i + 1) * tt) % period != 0).astype(F32)
    zprev = jnp.concatenate([zp_ref[...] * has_prev, z[:rows - SUBLANE]], axis=0)
    znext = jnp.concatenate([z[SUBLANE:], zn_ref[...] * has_next], axis=0)
    zs = z + mu_ref[0:1, :] * (zprev - z) + mu_ref[1:2, :] * (znext - z)
    r = zs[:, 0:cw]
    k = zs[:, cw:2 * cw]
    v = zs[:, 2 * cw:3 * cw]
    off = 3 * cw
    wd = jnp.tanh(zs[:, off:off + LANE])
    ad = zs[:, off + LANE:off + 2 * LANE]
    gd = jax.nn.sigmoid(zs[:, off + 2 * LANE:off + 2 * LANE + MXU_N])
    wl = jnp.dot(wd, wup_ref[...], precision=HP, preferred_element_type=F32)
    al = jnp.dot(ad, aup_ref[...], precision=HP, preferred_element_type=F32)
    g_o[...] = jnp.dot(gd, gup_ref[...], precision=HP, preferred_element_type=F32)
    kk = k * kk_ref[...]
    nrm = jnp.sqrt(_group_sum(kk * kk, ones_ref))
    kn = kk / jnp.maximum(nrm, 1e-12)
    _to_pairs(r, r_o)
    _to_pairs(v, v_o)
    _to_pairs(kn, kn_o)
    ksum = jnp.zeros_like(k)
    for d in range(2):
        xw = w0_ref[d:d + 1, :] + wl[:, d * cw:(d + 1) * cw]
        sp = jnp.maximum(-xw, 0.0) + jnp.log1p(jnp.exp(-jnp.abs(xw)))
        decay = jnp.exp(-jnp.exp(-sp - 0.5))
        a = jax.nn.sigmoid(a0_ref[d:d + 1, :] + al[:, d * cw:(d + 1) * cw])
        kd = k * (1.0 + (a - 1.0) * ka_ref[...])
        ksum = ksum + kd
        _to_pairs(decay, w_o, (d,))
        _to_pairs(kd, kd_o, (d,))
        _to_pairs(kn * a, b_o, (d,))
    bonus = _group_sum(r * ksum * rk_ref[...], ones_ref)
    bv_o[...] = bonus * v


def _rwkv_prep(z, period, zcols, mu, w0, wup, a0, aup, k_k, k_a, r_k, gup, ones, tt=16):
    rows_total = z.shape[0]
    t = rows_total // SUBLANE
    cw = w0.shape[1]
    tt = min(tt, period)
    assert t % tt == 0 and period % tt == 0 and tt % 2 == 0
    rows = tt * SUBLANE
    nblk = t // tt
    const = lambda shape: pl.BlockSpec(shape, lambda i: (0,) * len(shape))
    pair = pl.BlockSpec((tt // 2, LANE, LANE), lambda i: (i, 0, 0))
    pair_d = pl.BlockSpec((2, tt // 2, LANE, LANE), lambda i: (0, i, 0, 0))
    nat = pl.BlockSpec((rows, cw), lambda i: (i, 0))
    sh_pair = jax.ShapeDtypeStruct((t // 2, LANE, LANE), F32)
    sh_pair_d = jax.ShapeDtypeStruct((2, t // 2, LANE, LANE), F32)
    sh_nat = jax.ShapeDtypeStruct((rows_total, cw), F32)
    return pl.pallas_call(
        functools.partial(_prep_kernel, period),
        grid=(nblk,),
        in_specs=[pl.BlockSpec((rows, zcols), lambda i: (i, 0)),
                  pl.BlockSpec((SUBLANE, zcols), lambda i: (jnp.maximum(i * tt - 1, 0), 0)),
                  pl.BlockSpec((SUBLANE, zcols), lambda i: (jnp.minimum((i + 1) * tt, t - 1), 0)),
                  const(mu.shape), const(w0.shape), const(wup.shape), const(a0.shape),
                  const(aup.shape), const(k_k.shape), const(k_a.shape), const(r_k.shape),
                  const(gup.shape), const(ones.shape)],
        out_specs=[pair, pair, pair, pair_d, pair_d, pair_d, nat, nat],
        out_shape=[sh_pair, sh_pair, sh_pair, sh_pair_d, sh_pair_d, sh_pair_d, sh_nat, sh_nat],
        compiler_params=_cparams("parallel"),
        name="rwkv_prep",
    )(z, z, z, mu, w0, wup, a0, aup, k_k, k_a, r_k, gup, ones)


def _scan_kernel(reverse, r_ref, w_ref, k_ref, v_ref, kk_ref, b_ref, s0_ref, y_ref, s_ref):
    n = RWKV_HEAD
    npairs = r_ref.shape[0]

    @pl.when(pl.program_id(0) == 0)
    def _():
        s_ref[...] = s0_ref[...]

    def one_step(p, base):
        acc = jnp.zeros((n, LANE), F32)
        for k in range(n):
            acc = acc + s_ref[k] * kk_ref[p, pl.ds(base + k, 1), :]
        sa = -acc
        vv = v_ref[p, pl.ds(base, n), :]
        y = jnp.zeros((n, LANE), F32)
        for k in range(n):
            row = pl.ds(base + k, 1)
            sk = s_ref[k] * w_ref[0, p, row, :] + (sa * b_ref[0, p, row, :] + vv * k_ref[0, p, row, :])
            s_ref[k] = sk
            y = y + sk * r_ref[p, row, :]
        y_ref[p, pl.ds(base, n), :] = y

    def pair_step(i, carry):
        p = (npairs - 1 - i) if reverse else i
        for t2 in ((1, 0) if reverse else (0, 1)):
            one_step(p, t2 * n)
        return carry

    lax.fori_loop(0, npairs, pair_step, 0)


def _rwkv_scan(reverse, r, w, k, v, kk, b, s0, tbp=8):
    np_total = r.shape[0]
    tbp = min(tbp, np_total)
    assert np_total % tbp == 0
    nblk = np_total // tbp
    d = 1 if reverse else 0
    imap = (lambda i: (nblk - 1 - i, 0, 0)) if reverse else (lambda i: (i, 0, 0))
    imap_d = (lambda i: (d, nblk - 1 - i, 0, 0)) if reverse else (lambda i: (d, i, 0, 0))
    sp = pl.BlockSpec((tbp, LANE, LANE), imap)
    sp_d = pl.BlockSpec((1, tbp, LANE, LANE), imap_d)
    st = pl.BlockSpec((1, RWKV_HEAD, RWKV_HEAD, LANE), lambda i: (d, 0, 0, 0))
    return pl.pallas_call(
        functools.partial(_scan_kernel_wrap, reverse),
        grid=(nblk,),
        in_specs=[sp, sp_d, sp_d, sp, sp, sp_d, st],
        out_specs=[sp, pl.BlockSpec((RWKV_HEAD, RWKV_HEAD, LANE), lambda i: (0, 0, 0))],
        out_shape=[jax.ShapeDtypeStruct(r.shape, F32),
                   jax.ShapeDtypeStruct((RWKV_HEAD, RWKV_HEAD, LANE), F32)],
        compiler_params=_cparams("arbitrary"),
        name="rwkv_scan",
    )(r, w, k, v, kk, b, s0)


def _scan_kernel_wrap(reverse, r_ref, w_ref, k_ref, v_ref, kk_ref, b_ref, s0_ref, y_ref, s_ref):
    _scan_kernel(reverse, r_ref, w_ref, k_ref, v_ref, kk_ref, b_ref, s0_ref.at[0], y_ref, s_ref)


def _from_pairs(y_ref, nat_ref):
    npairs = y_ref.shape[0]
    lane = lax.broadcasted_iota(jnp.int32, (SUBLANE, LANE), 1)
    low = lane < RWKV_HEAD
    for p in range(npairs):
        xt = y_ref[p].T
        for j in range(nat_ref.shape[1] // LANE):
            h0 = xt[(2 * j) * SUBLANE:(2 * j + 1) * SUBLANE]
            h1 = xt[(2 * j + 1) * SUBLANE:(2 * j + 2) * SUBLANE]
            ev = jnp.where(low, h0, pltpu.roll(h1, RWKV_HEAD, 1))
            od = jnp.where(low, pltpu.roll(h0, RWKV_HEAD, 1), h1)
            nat_ref[(2 * p) * SUBLANE:(2 * p + 1) * SUBLANE, j * LANE:(j + 1) * LANE] = ev
            nat_ref[(2 * p + 1) * SUBLANE:(2 * p + 2) * SUBLANE, j * LANE:(j + 1) * LANE] = od


def _merge_kernel(y0_ref, y1_ref, bv_ref, go_ref, h0_ref, h1_ref, zg_ref, zgate_ref, x_ref,
                  lnx_ref, wa_ref, wb_ref, wo_ref, gt1_ref, gpost_ref, gpre_ref, scp_ref, sh_ref,
                  h_o, u_o, yn_ref, nat_ref):
    n = RWKV_HEAD
    d = wa_ref.shape[1]
    npairs = y0_ref.shape[0]
    tt, bsz, _ = x_ref.shape
    for p in range(npairs):
        y = y0_ref[p] + y1_ref[p]
        for t2 in range(2):
            yh = y[t2 * n:(t2 + 1) * n]
            mu = jnp.mean(yh, axis=0, keepdims=True)
            var = jnp.mean(jnp.square(yh - mu), axis=0, keepdims=True)
            yn_ref[p, t2 * n:(t2 + 1) * n, :] = (yh - mu) * lax.rsqrt(var + LNX_EPS)
    _from_pairs(yn_ref, nat_ref)
    y_a = (nat_ref[...] * lnx_ref[0:1, :] + lnx_ref[1:2, :] + bv_ref[...]) * go_ref[...]
    pa = jnp.dot(y_a.astype(BF16), wa_ref[...], preferred_element_type=F32)
    y_b = jax.nn.gelu(zg_ref[...]) * (h0_ref[0] + h1_ref[0])
    pb = jnp.dot(y_b.astype(BF16), wb_ref[...], preferred_element_type=F32)
    gates = jax.nn.sigmoid(zgate_ref[...])
    mixed = gates[:, :d] * pa + gates[:, d:] * pb
    yo = jnp.dot(mixed.astype(BF16), wo_ref[...], preferred_element_type=F32).reshape(tt, bsz, d)
    yo = yo * lax.rsqrt(jnp.mean(yo * yo, axis=-1, keepdims=True) + NORM_EPS) * gpost_ref[...]
    h = x_ref[...] + gt1_ref[...] * yo
    h_o[...] = h
    u = h * lax.rsqrt(jnp.mean(h * h, axis=-1, keepdims=True) + NORM_EPS) * gpre_ref[...]
    u_o[...] = (u * scp_ref[...] + sh_ref[...]).reshape(tt * bsz, d)


def _merge(y0, y1, bv, g_out, h_lru, z, x_tm, lnx, wa, wb, wo, gt1, g_post, g_pre, scp, sh, tt=32):
    t, bsz, d = x_tm.shape
    rows = tt * bsz
    c = wa.shape[0]
    zcol = lambda blk, width: pl.BlockSpec((rows, width), lambda i: (i, blk))
    const = lambda shape: pl.BlockSpec(shape, lambda i: (0,) * len(shape))
    pair = pl.BlockSpec((tt // 2, LANE, LANE), lambda i: (i, 0, 0))
    nat = pl.BlockSpec((rows, c), lambda i: (i, 0))
    lg_blk = (3 * c + c + c) // c
    gate_blk = (lg_blk + 1) * c // (2 * d)
    return pl.pallas_call(
        _merge_kernel,
        grid=(t // tt,),
        in_specs=[pair, pair, nat, nat,
                  pl.BlockSpec((1, rows, c), lambda i: (0, i, 0)),
                  pl.BlockSpec((1, rows, c), lambda i: (1, i, 0)),
                  zcol(lg_blk, c), zcol(gate_blk, 2 * d),
                  pl.BlockSpec((tt, bsz, d), lambda i: (i, 0, 0)),
                  const(lnx.shape), const(wa.shape), const(wb.shape), const(wo.shape),
                  const(gt1.shape), const(g_post.shape), const(g_pre.shape),
                  const(scp.shape), const(sh.shape)],
        out_specs=[pl.BlockSpec((tt, bsz, d), lambda i: (i, 0, 0)),
                   pl.BlockSpec((rows, d), lambda i: (i, 0))],
        out_shape=[jax.ShapeDtypeStruct((t, bsz, d), F32),
                   jax.ShapeDtypeStruct((t * bsz, d), F32)],
        scratch_shapes=[pltpu.VMEM((tt // 2, LANE, LANE), F32), pltpu.VMEM((rows, c), F32)],
        compiler_params=_cparams("parallel"),
        name="merge",
    )(y0, y1, bv, g_out, h_lru, h_lru, z, z, x_tm, lnx, wa, wb, wo, gt1, g_post, g_pre, scp, sh)


def _experts_kernel(blk_e_ref, x_ref, wgu_ref, wdn_ref, o_ref):
    del blk_e_ref
    ff = wdn_ref.shape[1]
    gu = jnp.dot(x_ref[...], wgu_ref[0], preferred_element_type=F32)
    g = gu[:, :ff]
    u = gu[:, ff:]
    act = (g * jax.nn.sigmoid(g) * u).astype(BF16)
    o_ref[...] = jnp.dot(act, wdn_ref[0], preferred_element_type=F32)


def _grouped_experts(xs, blk_e, w_gu, w_down):
    p_rows, d = xs.shape
    _, _, ff2 = w_gu.shape
    ff = ff2 // 2
    n_blocks = p_rows // MOE_ROWS
    grid_spec = pltpu.PrefetchScalarGridSpec(
        num_scalar_prefetch=1,
        grid=(n_blocks,),
        in_specs=[pl.BlockSpec((MOE_ROWS, d), lambda i, be: (i, 0)),
                  pl.BlockSpec((1, d, ff2), lambda i, be: (be[i], 0, 0)),
                  pl.BlockSpec((1, ff, d), lambda i, be: (be[i], 0, 0))],
        out_specs=pl.BlockSpec((MOE_ROWS, d), lambda i, be: (i, 0)),
    )
    return pl.pallas_call(
        _experts_kernel,
        grid_spec=grid_spec,
        out_shape=jax.ShapeDtypeStruct((p_rows, d), F32),
        compiler_params=_cparams("arbitrary"),
        name="moe_experts",
    )(blk_e, xs.astype(BF16), w_gu.astype(BF16), w_down.astype(BF16))


def _rmsnorm(x, g):
    return x * lax.rsqrt(jnp.mean(x * x, axis=-1, keepdims=True) + NORM_EPS) * g


def kernel(x, c, ctx, c_ctx, w_mod, b_mod, norm_g, w_in, shift_mu, rw_w0, rw_w_up, rw_a0, rw_a_up,
           rw_g_up, rw_k_k, rw_k_a, rw_r_k, rw_lnx, lru_conv_w, lru_conv_b, lru_gate_w, lru_gate_b,
           lru_l, w_branch_a, w_branch_b, w_out, router_w, router_b, ex_w_gu, ex_w_down,
           sh_w_gu, sh_w_down):
    l = 0
    bsz, t, d = x.shape
    tc = ctx.shape[1]
    rows_g = t // GRID_W
    m = bsz * t
    cw = rw_w0.shape[-1]
    cl = lru_l.shape[-1]
    nh = cw // RWKV_HEAD
    assert bsz == SUBLANE and cw == cl == d and 2 * nh * bsz // 2 == LANE
    rwkv_cols = 3 * cw + 2 * DECAY_LORA + 2 * AAA_LORA + GATE_LORA
    lora_cols = rwkv_cols - 3 * cw

    mod = jnp.dot(jax.nn.silu(c), w_mod[l], precision=HP) + b_mod[l]
    mod_c = jnp.dot(jax.nn.silu(c_ctx), w_mod[l], precision=HP) + b_mod[l]
    sh1, sc1, gt1, sh2, sc2, gt2 = jnp.split(mod, 6, axis=-1)
    csh1 = jnp.broadcast_to(mod_c[0:d], (bsz, d))
    csc1 = jnp.broadcast_to(mod_c[d:2 * d], (bsz, d))
    g_pre1, g_post1, g_pre2, g_post2 = [g[None, :] for g in norm_g[l]]

    wi = w_in[l]
    w_al = jnp.concatenate([wi[:, :rwkv_cols], jnp.zeros((d, cw - lora_cols), F32),
                            wi[:, rwkv_cols:]], axis=1).astype(BF16)
    zr_w = 4 * cw
    mu_al = jnp.concatenate([shift_mu[l], jnp.zeros((2, cw - lora_cols), F32)], axis=1)

    x_tm = jnp.transpose(x, (1, 0, 2))
    ctx_tm = jnp.transpose(ctx, (1, 0, 2))
    zc = _inproj(ctx_tm, 1.0 + csc1, csh1, g_pre1, w_al[:, :zr_w + cl])
    z = _inproj(x_tm, 1.0 + sc1, sh1, g_pre1, w_al)

    blk = cl // LRU_BLOCKS
    per = MXU_N // blk
    gw = lru_gate_w[l].reshape(2, 2, LRU_BLOCKS // per, per, blk, blk)
    eye = jnp.eye(per, dtype=F32)
    gw_bd = jnp.einsum('dgqpjk,pr->dgqpjrk', gw, eye).reshape(2, 2, LRU_BLOCKS // per, MXU_N, MXU_N)
    gw_bd = gw_bd.astype(BF16)
    gb = lru_gate_b[l].reshape(2, 2, cl)
    sp_l = jax.nn.softplus(-lru_l[l])[:, None, :]
    lru_args = (lru_conv_w[l], lru_conv_b[l][None, :], gw_bd, gb, sp_l)
    _, hfin = _lru(zc.reshape(tc, 1, bsz, -1), 4, *lru_args, jnp.zeros((2, bsz, cl), F32))
    h_lru, _ = _lru(z.reshape(rows_g, GRID_W, bsz, -1), 4, *lru_args, hfin)
    h_lru = h_lru.reshape(2, m, cl)

    zeros_c = jnp.zeros((DECAY_LORA, cw), F32)
    wup2 = jnp.concatenate([jnp.concatenate([rw_w_up[l, 0], zeros_c], 1),
                            jnp.concatenate([zeros_c, rw_w_up[l, 1]], 1)], axis=0)
    aup2 = jnp.concatenate([jnp.concatenate([rw_a_up[l, 0], zeros_c], 1),
                            jnp.concatenate([zeros_c, rw_a_up[l, 1]], 1)], axis=0)
    gup = jnp.concatenate([rw_g_up[l], jnp.zeros((MXU_N - GATE_LORA, cw), F32)], axis=0)
    hid = jnp.arange(MXU_N) // RWKV_HEAD
    ones_blk = (hid[:, None] == hid[None, :]).astype(F32)
    prep_args = (mu_al, rw_w0[l], wup2, rw_a0[l], aup2, rw_k_k[l][None, :], rw_k_a[l][None, :],
                 rw_r_k[l].reshape(1, cw), gup, ones_blk)
    pc = _rwkv_prep(zc, tc, zr_w, *prep_args)
    pz = _rwkv_prep(z, GRID_W, zr_w, *prep_args)
    s_zero = jnp.zeros((2, RWKV_HEAD, RWKV_HEAD, LANE), F32)
    ys = []
    for rev in (False, True):
        _, s_c = _rwkv_scan(rev, pc[0], pc[3], pc[4], pc[1], pc[2], pc[5], s_zero)
        s_init = jnp.stack([s_c, s_c], axis=0)
        y_d, _ = _rwkv_scan(rev, pz[0], pz[3], pz[4], pz[1], pz[2], pz[5], s_init)
        ys.append(y_d)

    h, u2 = _merge(ys[0], ys[1], pz[6], pz[7], h_lru, z, x_tm, rw_lnx[l],
                   w_branch_a[l].astype(BF16), w_branch_b[l].astype(BF16), w_out[l].astype(BF16),
                   gt1, g_post1, g_pre2, 1.0 + sc2, sh2)

    n_e = router_w.shape[-1]
    per_g = n_e // N_GROUPS
    scores = jax.nn.sigmoid(jnp.dot(u2, router_w[l], precision=HP))
    biased = scores + router_b[l]
    grp_score = lax.top_k(biased.reshape(m, N_GROUPS, per_g), 2)[0].sum(-1)
    _, grp_idx = lax.top_k(grp_score, TOPK_GROUPS)
    grp_mask = jnp.any(grp_idx[:, :, None] == jnp.arange(N_GROUPS)[None, None, :], axis=1)
    masked = jnp.where(jnp.repeat(grp_mask, per_g, axis=1), biased, -jnp.inf)
    _, top_idx = lax.top_k(masked, TOP_K)
    top_s = jnp.take_along_axis(scores, top_idx, axis=1)
    gate = top_s / jnp.sum(top_s, axis=-1, keepdims=True) * ROUTED_SCALE

    n_assign = m * TOP_K
    flat_e = top_idx.reshape(-1).astype(jnp.int32)
    flat_tok = jnp.arange(n_assign, dtype=jnp.int32) // TOP_K
    order = jnp.argsort(flat_e)
    sorted_e = flat_e[order]
    counts = jnp.bincount(flat_e, length=n_e)
    padded = (counts + MOE_ROWS - 1) // MOE_ROWS * MOE_ROWS
    pad_end = jnp.cumsum(padded)
    pad_start = pad_end - padded
    start = jnp.cumsum(counts) - counts
    dest = (pad_start[sorted_e] + jnp.arange(n_assign, dtype=jnp.int32) - start[sorted_e]).astype(jnp.int32)
    n_blocks = (n_assign + n_e * (MOE_ROWS - 1) + MOE_ROWS - 1) // MOE_ROWS
    p_rows = n_blocks * MOE_ROWS
    buf_tok = jnp.full((p_rows,), m, jnp.int32).at[dest].set(flat_tok[order])
    blk_e = jnp.minimum(jnp.searchsorted(pad_end, jnp.arange(n_blocks) * MOE_ROWS, side='right'),
                        n_e - 1).astype(jnp.int32)
    x_pad = jnp.concatenate([u2.astype(BF16), jnp.zeros((1, d), BF16)], axis=0)
    ys_e = _grouped_experts(x_pad[buf_tok], blk_e, ex_w_gu[l], ex_w_down[l])
    pos = jnp.zeros((n_assign,), jnp.int32).at[order].set(dest)
    routed = jnp.sum(ys_e[pos].reshape(m, TOP_K, d) * gate[:, :, None], axis=1)
    gu = _mm(u2, sh_w_gu[l])
    ff = sh_w_down.shape[1]
    shared = _mm(jax.nn.silu(gu[:, :ff]) * gu[:, ff:], sh_w_down[l])
    moe = (routed + shared).reshape(t, bsz, d)
    out_tm = h + gt2 * _rmsnorm(moe, g_post2)
    return jnp.transpose(out_tm, (1, 0, 2))
```

```python
import functools

import jax
import jax.numpy as jnp
from jax import lax
from jax.experimental import pallas as pl
from jax.experimental.pallas import tpu as pltpu

GRID_W = 64
NORM_EPS = 1e-6
RWKV_HEAD = 64
DECAY_LORA = 64
AAA_LORA = 64
GATE_LORA = 160
LNX_EPS = 64e-5
LRU_BLOCKS = 16
CONV_LEFT = 2
LRU_C = 8.0
N_GROUPS = 8
TOPK_GROUPS = 4
TOP_K = 6
ROUTED_SCALE = 2.5

LANE = 128
SUBLANE = 8
MXU_N = 256
MOE_ROWS = 256
KCH = 32
VMEM_LIMIT = 48 * 1024 * 1024
HP = lax.Precision.HIGHEST
F32 = jnp.float32
BF16 = jnp.bfloat16


def _cparams(*sem):
    return pltpu.CompilerParams(dimension_semantics=sem, vmem_limit_bytes=VMEM_LIMIT)


def _inproj_kernel(x_ref, scp_ref, sh_ref, g_ref, w_ref, o_ref, xm_ref):
    @pl.when(pl.program_id(1) == 0)
    def _():
        x = x_ref[...]
        ms = jnp.mean(x * x, axis=-1, keepdims=True)
        xm = x * lax.rsqrt(ms + NORM_EPS) * g_ref[...] * scp_ref[...] + sh_ref[...]
        xm_ref[...] = xm.reshape(xm_ref.shape).astype(BF16)

    o_ref[...] = jnp.dot(xm_ref[...], w_ref[...], preferred_element_type=F32)


def _inproj(x_tm, scp, sh, g, w, tt=128, tn=1024):
    t, bsz, d = x_tm.shape
    n = w.shape[1]
    tt = min(tt, t)
    assert t % tt == 0 and n % tn == 0
    return pl.pallas_call(
        _inproj_kernel,
        grid=(t // tt, n // tn),
        in_specs=[pl.BlockSpec((tt, bsz, d), lambda i, j: (i, 0, 0)),
                  pl.BlockSpec((bsz, d), lambda i, j: (0, 0)),
                  pl.BlockSpec((bsz, d), lambda i, j: (0, 0)),
                  pl.BlockSpec((1, d), lambda i, j: (0, 0)),
                  pl.BlockSpec((d, tn), lambda i, j: (0, j))],
        out_specs=pl.BlockSpec((tt * bsz, tn), lambda i, j: (i, j)),
        out_shape=jax.ShapeDtypeStruct((t * bsz, n), F32),
        scratch_shapes=[pltpu.VMEM((tt * bsz, d), BF16)],
        compiler_params=_cparams("parallel", "arbitrary"),
        name="in_proj",
    )(x_tm, scp, sh, g, w)


def _lru_kernel(z_ref, cw_ref, cb_ref, gw_ref, gb_ref, sp_ref, h0_ref, h_ref, hf_ref,
                a_ref, bx_ref):
    d = pl.program_id(0)
    w = pl.program_id(2)
    nr, _, bsz, cq = z_ref.shape
    taps = cw_ref.shape[0]

    @pl.when(w == 0)
    def _():
        hf_ref[0] = h0_ref[0]

    zx = z_ref[:, 0]
    xc = jnp.zeros_like(zx) + cb_ref[...]
    for j in range(taps):
        off = j - CONV_LEFT
        if off < 0:
            sh = jnp.concatenate([jnp.zeros((-off, bsz, cq), F32), zx[:nr + off]], axis=0)
        elif off > 0:
            sh = jnp.concatenate([zx[off:], jnp.zeros((off, bsz, cq), F32)], axis=0)
        else:
            sh = zx
        xc = xc + cw_ref[j:j + 1, :] * sh
    x2 = xc.reshape(nr * bsz, cq)
    xb = x2.astype(BF16)
    pre = []
    for g in range(2):
        cols = []
        for q in range(cq // MXU_N):
            cols.append(jnp.dot(xb[:, q * MXU_N:(q + 1) * MXU_N], gw_ref[0, g, q],
                                preferred_element_type=F32))
        pre.append(jnp.concatenate(cols, axis=1) + gb_ref[0, g:g + 1, :])
    rec = jax.nn.sigmoid(pre[0])
    inp = jax.nn.sigmoid(pre[1])
    log_a = -LRU_C * rec * sp_ref[0]
    a = jnp.exp(log_a)
    bx = jnp.sqrt(-jnp.tanh(log_a) * (a * a + 1.0)) * inp * x2
    a_ref[...] = a.reshape(nr, bsz, cq)
    bx_ref[...] = bx.reshape(nr, bsz, cq)

    def step(i, h):
        r = i + d * (nr - 1 - 2 * i)
        h = a_ref[r] * h + bx_ref[r]
        h_ref[0, r, 0] = h
        return h

    hf_ref[0] = lax.fori_loop(0, nr, step, hf_ref[0])


def _lru(z4, col_blk, conv_w, conv_b, gate_w, gate_b, sp_l, h0, cq=512):
    nr, nw, bsz, _ = z4.shape
    c = conv_w.shape[1]
    nq = c // cq
    cb0 = col_blk * (c // cq)

    def zmap(d, q, w):
        return (0, w + d * (nw - 1 - 2 * w), 0, cb0 + q)

    def hmap(d, q, w):
        return (d, 0, w + d * (nw - 1 - 2 * w), 0, q)

    return pl.pallas_call(
        _lru_kernel,
        grid=(2, nq, nw),
        in_specs=[pl.BlockSpec((nr, 1, bsz, cq), zmap),
                  pl.BlockSpec((conv_w.shape[0], cq), lambda d, q, w: (0, q)),
                  pl.BlockSpec((1, cq), lambda d, q, w: (0, q)),
                  pl.BlockSpec((1, 2, cq // MXU_N, MXU_N, MXU_N), lambda d, q, w: (d, 0, q, 0, 0)),
                  pl.BlockSpec((1, 2, cq), lambda d, q, w: (d, 0, q)),
                  pl.BlockSpec((1, 1, cq), lambda d, q, w: (d, 0, q)),
                  pl.BlockSpec((1, bsz, cq), lambda d, q, w: (d, 0, q))],
        out_specs=[pl.BlockSpec((1, nr, 1, bsz, cq), hmap),
                   pl.BlockSpec((1, bsz, cq), lambda d, q, w: (d, 0, q))],
        out_shape=[jax.ShapeDtypeStruct((2, nr, nw, bsz, c), F32),
                   jax.ShapeDtypeStruct((2, bsz, c), F32)],
        scratch_shapes=[pltpu.VMEM((nr, bsz, cq), F32), pltpu.VMEM((nr, bsz, cq), F32)],
        compiler_params=_cparams("arbitrary", "arbitrary", "arbitrary"),
        name="lru_fused",
    )(z4, conv_w, conv_b, gate_w, gate_b, sp_l, h0)


def _to_pairs(q, out_ref, lead=()):
    rows, c = q.shape
    npairs = rows // (2 * SUBLANE)
    lane = lax.broadcasted_iota(jnp.int32, (SUBLANE, LANE), 1)
    low = lane < RWKV_HEAD
    for p in range(npairs):
        ev = q[(2 * p) * SUBLANE:(2 * p + 1) * SUBLANE]
        od = q[(2 * p + 1) * SUBLANE:(2 * p + 2) * SUBLANE]
        pieces = []
        for j in range(c // LANE):
            e_j = ev[:, j * LANE:(j + 1) * LANE]
            o_j = od[:, j * LANE:(j + 1) * LANE]
            pieces.append(jnp.where(low, e_j, pltpu.roll(o_j, RWKV_HEAD, 1)))
            pieces.append(jnp.where(low, pltpu.roll(e_j, RWKV_HEAD, 1), o_j))
        x = jnp.concatenate(pieces, axis=0)
        out_ref[lead + (p,)] = x.T


def _group_sum(x, ones_ref):
    hi = x.astype(BF16)
    lo = (x - hi.astype(F32)).astype(BF16)
    cols = []
    for q in range(x.shape[1] // MXU_N):
        sl = slice(q * MXU_N, (q + 1) * MXU_N)
        cols.append(jnp.dot(hi[:, sl], ones_ref[...], preferred_element_type=F32)
                    + jnp.dot(lo[:, sl], ones_ref[...], preferred_element_type=F32))
    return jnp.concatenate(cols, axis=1)


def _prep_kernel(period, z_ref, zp_ref, zn_ref, mu_ref, w0_ref, wup_ref, a0_ref, aup_ref,
                 kk_ref, ka_ref, rk_ref, gup_ref, ones_ref,
                 r_o, v_o, kn_o, w_o, kd_o, b_o, bv_o, g_o):
    i = pl.program_id(0)
    rows, _ = z_ref.shape
    tt = rows // SUBLANE
    cw = w0_ref.shape[1]
    z = z_ref[...]
    has_prev = ((i * tt) % period != 0).astype(F32)
    has_next = (((i + 1) * tt) % period != 0).astype(F32)
    zprev = jnp.concatenate([zp_ref[...] * has_prev, z[:rows - SUBLANE]], axis=0)
    znext = jnp.concatenate([z[SUBLANE:], zn_ref[...] * has_next], axis=0)
    zs = z + mu_ref[0:1, :] * (zprev - z) + mu_ref[1:2, :] * (znext - z)
    r = zs[:, 0:cw]
    k = zs[:, cw:2 * cw]
    v = zs[:, 2 * cw:3 * cw]
    off = 3 * cw
    wd = jnp.tanh(zs[:, off:off + LANE])
    ad = zs[:, off + LANE:off + 2 * LANE]
    gd = jax.nn.sigmoid(zs[:, off + 2 * LANE:off + 2 * LANE + MXU_N])
    wl = jnp.dot(wd.astype(BF16), wup_ref[...], preferred_element_type=F32)
    al = jnp.dot(ad.astype(BF16), aup_ref[...], preferred_element_type=F32)
    g_o[...] = jnp.dot(gd.astype(BF16), gup_ref[...], preferred_element_type=F32)
    kk = k * kk_ref[...]
    nrm = jnp.sqrt(_group_sum(kk * kk, ones_ref))
    kn = kk / jnp.maximum(nrm, 1e-12)
    _to_pairs(r, r_o)
    _to_pairs(v, v_o)
    _to_pairs(kn, kn_o)
    ksum = jnp.zeros_like(k)
    for d in range(2):
        xw = w0_ref[d:d + 1, :] + wl[:, d * cw:(d + 1) * cw]
        sp = jnp.maximum(-xw, 0.0) + jnp.log1p(jnp.exp(-jnp.abs(xw)))
        decay = jnp.exp(-jnp.exp(-sp - 0.5))
        a = jax.nn.sigmoid(a0_ref[d:d + 1, :] + al[:, d * cw:(d + 1) * cw])
        kd = k * (1.0 + (a - 1.0) * ka_ref[...])
        ksum = ksum + kd
        _to_pairs(decay, w_o, (d,))
        _to_pairs(kd, kd_o, (d,))
        _to_pairs(kn * a, b_o, (d,))
    bonus = _group_sum(r * ksum * rk_ref[...], ones_ref)
    bv_o[...] = bonus * v


def _rwkv_prep(z, period, zcols, mu, w0, wup, a0, aup, k_k, k_a, r_k, gup, ones, tt=16):
    rows_total = z.shape[0]
    t = rows_total // SUBLANE
    cw = w0.shape[1]
    tt = min(tt, period)
    assert t % tt == 0 and period % tt == 0 and tt % 2 == 0
    rows = tt * SUBLANE
    nblk = t // tt
    const = lambda shape: pl.BlockSpec(shape, lambda i: (0,) * len(shape))
    pair = pl.BlockSpec((tt // 2, LANE, LANE), lambda i: (i, 0, 0))
    pair_d = pl.BlockSpec((2, tt // 2, LANE, LANE), lambda i: (0, i, 0, 0))
    nat = pl.BlockSpec((rows, cw), lambda i: (i, 0))
    sh_pair = jax.ShapeDtypeStruct((t // 2, LANE, LANE), F32)
    sh_pair_d = jax.ShapeDtypeStruct((2, t // 2, LANE, LANE), F32)
    sh_nat = jax.ShapeDtypeStruct((rows_total, cw), F32)
    return pl.pallas_call(
        functools.partial(_prep_kernel, period),
        grid=(nblk,),
        in_specs=[pl.BlockSpec((rows, zcols), lambda i: (i, 0)),
                  pl.BlockSpec((SUBLANE, zcols), lambda i: (jnp.maximum(i * tt - 1, 0), 0)),
                  pl.BlockSpec((SUBLANE, zcols), lambda i: (jnp.minimum((i + 1) * tt, t - 1), 0)),
                  const(mu.shape), const(w0.shape), const(wup.shape), const(a0.shape),
                  const(aup.shape), const(k_k.shape), const(k_a.shape), const(r_k.shape),
                  const(gup.shape), const(ones.shape)],
        out_specs=[pair, pair, pair, pair_d, pair_d, pair_d, nat, nat],
        out_shape=[sh_pair, sh_pair, sh_pair, sh_pair_d, sh_pair_d, sh_pair_d, sh_nat, sh_nat],
        compiler_params=_cparams("parallel"),
        name="rwkv_prep",
    )(z, z, z, mu, w0, wup, a0, aup, k_k, k_a, r_k, gup, ones)


def _scan_kernel(reverse, r_ref, w_ref, k_ref, v_ref, kk_ref, b_ref, s0_ref, y_ref, s_ref):
    n = RWKV_HEAD
    npairs = r_ref.shape[0]

    @pl.when(pl.program_id(0) == 0)
    def _():
        s_ref[...] = s0_ref[...]

    def removal(p, base):
        def body(kb, acc):
            k0 = pl.multiple_of(kb * KCH, KCH)
            for j in range(KCH):
                acc = acc + s_ref[k0 + j] * kk_ref[p, pl.ds(base + k0 + j, 1), :]
            return acc

        return lax.fori_loop(0, n // KCH, body, jnp.zeros((n, LANE), F32))

    def one_step(p, base, pn, basen, acc):
        sa = -acc
        vv = v_ref[p, pl.ds(base, n), :]

        def body(kb, carry):
            y, nacc = carry
            k0 = pl.multiple_of(kb * KCH, KCH)
            for j in range(KCH):
                row = pl.ds(base + k0 + j, 1)
                sk = (s_ref[k0 + j] * w_ref[0, p, row, :]
                      + (sa * b_ref[0, p, row, :] + vv * k_ref[0, p, row, :]))
                s_ref[k0 + j] = sk
                y = y + sk * r_ref[p, row, :]
                nacc = nacc + sk * kk_ref[pn, pl.ds(basen + k0 + j, 1), :]
            return y, nacc

        zero = jnp.zeros((n, LANE), F32)
        y, nacc = lax.fori_loop(0, n // KCH, body, (zero, zero))
        y_ref[p, pl.ds(base, n), :] = y
        return nacc

    first, second = ((n, 0) if reverse else (0, n))
    p_first = (npairs - 1) if reverse else 0

    def pair_step(i, acc):
        p = (npairs - 1 - i) if reverse else i
        pn = jnp.clip(p + (-1 if reverse else 1), 0, npairs - 1)
        acc = one_step(p, first, p, second, acc)
        return one_step(p, second, pn, first, acc)

    lax.fori_loop(0, npairs, pair_step, removal(p_first, first))


def _rwkv_scan(reverse, r, w, k, v, kk, b, s0, tbp=16):
    np_total = r.shape[0]
    tbp = min(tbp, np_total)
    assert np_total % tbp == 0
    nblk = np_total // tbp
    d = 1 if reverse else 0
    imap = (lambda i: (nblk - 1 - i, 0, 0)) if reverse else (lambda i: (i, 0, 0))
    imap_d = (lambda i: (d, nblk - 1 - i, 0, 0)) if reverse else (lambda i: (d, i, 0, 0))
    sp = pl.BlockSpec((tbp, LANE, LANE), imap)
    sp_d = pl.BlockSpec((1, tbp, LANE, LANE), imap_d)
    st = pl.BlockSpec((1, RWKV_HEAD, RWKV_HEAD, LANE), lambda i: (d, 0, 0, 0))
    return pl.pallas_call(
        functools.partial(_scan_kernel_wrap, reverse),
        grid=(nblk,),
        in_specs=[sp, sp_d, sp_d, sp, sp, sp_d, st],
        out_specs=[sp, pl.BlockSpec((RWKV_HEAD, RWKV_HEAD, LANE), lambda i: (0, 0, 0))],
        out_shape=[jax.ShapeDtypeStruct(r.shape, F32),
                   jax.ShapeDtypeStruct((RWKV_HEAD, RWKV_HEAD, LANE), F32)],
        compiler_params=_cparams("arbitrary"),
        name="rwkv_scan",
    )(r, w, k, v, kk, b, s0)


def _scan_kernel_wrap(reverse, r_ref, w_ref, k_ref, v_ref, kk_ref, b_ref, s0_ref, y_ref, s_ref):
    _scan_kernel(reverse, r_ref, w_ref, k_ref, v_ref, kk_ref, b_ref, s0_ref.at[0], y_ref, s_ref)


def _from_pairs(y_ref, nat_ref):
    npairs = y_ref.shape[0]
    lane = lax.broadcasted_iota(jnp.int32, (SUBLANE, LANE), 1)
    low = lane < RWKV_HEAD
    for p in range(npairs):
        xt = y_ref[p].T
        for j in range(nat_ref.shape[1] // LANE):
            h0 = xt[(2 * j) * SUBLANE:(2 * j + 1) * SUBLANE]
            h1 = xt[(2 * j + 1) * SUBLANE:(2 * j + 2) * SUBLANE]
            ev = jnp.where(low, h0, pltpu.roll(h1, RWKV_HEAD, 1))
            od = jnp.where(low, pltpu.roll(h0, RWKV_HEAD, 1), h1)
            nat_ref[(2 * p) * SUBLANE:(2 * p + 1) * SUBLANE, j * LANE:(j + 1) * LANE] = ev
            nat_ref[(2 * p + 1) * SUBLANE:(2 * p + 2) * SUBLANE, j * LANE:(j + 1) * LANE] = od


def _merge_kernel(y0_ref, y1_ref, bv_ref, go_ref, h0_ref, h1_ref, zg_ref, zgate_ref, x_ref,
                  lnx_ref, wa_ref, wb_ref, wo_ref, gt1_ref, gpost_ref, gpre_ref, scp_ref, sh_ref,
                  h_o, u_o, ub_o, yn_ref, nat_ref):
    n = RWKV_HEAD
    d = wa_ref.shape[1]
    npairs = y0_ref.shape[0]
    tt, bsz, _ = x_ref.shape
    for p in range(npairs):
        y = y0_ref[p] + y1_ref[p]
        for t2 in range(2):
            yh = y[t2 * n:(t2 + 1) * n]
            mu = jnp.mean(yh, axis=0, keepdims=True)
            var = jnp.mean(jnp.square(yh - mu), axis=0, keepdims=True)
            yn_ref[p, t2 * n:(t2 + 1) * n, :] = (yh - mu) * lax.rsqrt(var + LNX_EPS)
    _from_pairs(yn_ref, nat_ref)
    y_a = (nat_ref[...] * lnx_ref[0:1, :] + lnx_ref[1:2, :] + bv_ref[...]) * go_ref[...]
    pa = jnp.dot(y_a.astype(BF16), wa_ref[...], preferred_element_type=F32)
    y_b = jax.nn.gelu(zg_ref[...]) * (h0_ref[0] + h1_ref[0])
    pb = jnp.dot(y_b.astype(BF16), wb_ref[...], preferred_element_type=F32)
    gates = jax.nn.sigmoid(zgate_ref[...])
    mixed = gates[:, :d] * pa + gates[:, d:] * pb
    yo = jnp.dot(mixed.astype(BF16), wo_ref[...], preferred_element_type=F32).reshape(tt, bsz, d)
    yo = yo * lax.rsqrt(jnp.mean(yo * yo, axis=-1, keepdims=True) + NORM_EPS) * gpost_ref[...]
    h = x_ref[...] + gt1_ref[...] * yo
    h_o[...] = h
    u = h * lax.rsqrt(jnp.mean(h * h, axis=-1, keepdims=True) + NORM_EPS) * gpre_ref[...]
    u = (u * scp_ref[...] + sh_ref[...]).reshape(tt * bsz, d)
    u_o[...] = u
    ub_o[...] = u.astype(BF16)


def _merge(y0, y1, bv, g_out, h_lru, z, x_tm, lnx, wa, wb, wo, gt1, g_post, g_pre, scp, sh, tt=32):
    t, bsz, d = x_tm.shape
    rows = tt * bsz
    c = wa.shape[0]
    zcol = lambda blk, width: pl.BlockSpec((rows, width), lambda i: (i, blk))
    const = lambda shape: pl.BlockSpec(shape, lambda i: (0,) * len(shape))
    pair = pl.BlockSpec((tt // 2, LANE, LANE), lambda i: (i, 0, 0))
    nat = pl.BlockSpec((rows, c), lambda i: (i, 0))
    lg_blk = (3 * c + c + c) // c
    gate_blk = (lg_blk + 1) * c // (2 * d)
    return pl.pallas_call(
        _merge_kernel,
        grid=(t // tt,),
        in_specs=[pair, pair, nat, nat,
                  pl.BlockSpec((1, rows, c), lambda i: (0, i, 0)),
                  pl.BlockSpec((1, rows, c), lambda i: (1, i, 0)),
                  zcol(lg_blk, c), zcol(gate_blk, 2 * d),
                  pl.BlockSpec((tt, bsz, d), lambda i: (i, 0, 0)),
                  const(lnx.shape), const(wa.shape), const(wb.shape), const(wo.shape),
                  const(gt1.shape), const(g_post.shape), const(g_pre.shape),
                  const(scp.shape), const(sh.shape)],
        out_specs=[pl.BlockSpec((tt, bsz, d), lambda i: (i, 0, 0)),
                   pl.BlockSpec((rows, d), lambda i: (i, 0)),
                   pl.BlockSpec((rows, d), lambda i: (i, 0))],
        out_shape=[jax.ShapeDtypeStruct((t, bsz, d), F32),
                   jax.ShapeDtypeStruct((t * bsz, d), F32),
                   jax.ShapeDtypeStruct((t * bsz, d), BF16)],
        scratch_shapes=[pltpu.VMEM((tt // 2, LANE, LANE), F32), pltpu.VMEM((rows, c), F32)],
        compiler_params=_cparams("parallel"),
        name="merge",
    )(y0, y1, bv, g_out, h_lru, h_lru, z, z, x_tm, lnx, wa, wb, wo, gt1, g_post, g_pre, scp, sh)


def _experts_kernel(blk_e_ref, x_ref, wgu_ref, wdn_ref, o_ref, wgu_b, wdn_b):
    i = pl.program_id(0)
    ff = wdn_ref.shape[1]

    @pl.when(jnp.logical_or(i == 0, blk_e_ref[i] != blk_e_ref[jnp.maximum(i - 1, 0)]))
    def _():
        wgu_b[...] = wgu_ref[0].astype(BF16)
        wdn_b[...] = wdn_ref[0].astype(BF16)

    gu = jnp.dot(x_ref[...], wgu_b[...], preferred_element_type=F32)
    g = gu[:, :ff]
    u = gu[:, ff:]
    act = (g * jax.nn.sigmoid(g) * u).astype(BF16)
    o_ref[...] = jnp.dot(act, wdn_b[...], preferred_element_type=F32).astype(o_ref.dtype)


def _grouped_experts(xs, blk_e, w_gu, w_down):
    p_rows, d = xs.shape
    _, _, ff2 = w_gu.shape
    ff = ff2 // 2
    n_blocks = p_rows // MOE_ROWS
    grid_spec = pltpu.PrefetchScalarGridSpec(
        num_scalar_prefetch=1,
        grid=(n_blocks,),
        in_specs=[pl.BlockSpec((MOE_ROWS, d), lambda i, be: (i, 0)),
                  pl.BlockSpec((1, d, ff2), lambda i, be: (be[i], 0, 0)),
                  pl.BlockSpec((1, ff, d), lambda i, be: (be[i], 0, 0))],
        out_specs=pl.BlockSpec((MOE_ROWS, d), lambda i, be: (i, 0)),
        scratch_shapes=[pltpu.VMEM((d, ff2), BF16), pltpu.VMEM((ff, d), BF16)],
    )
    return pl.pallas_call(
        _experts_kernel,
        grid_spec=grid_spec,
        out_shape=jax.ShapeDtypeStruct((p_rows, d), BF16),
        compiler_params=_cparams("arbitrary"),
        name="moe_experts",
    )(blk_e, xs, w_gu, w_down)


def _argmax_pick(cur, iota, big, axis):
    mx = jnp.max(cur, axis=axis, keepdims=True)
    ix = jnp.min(jnp.where(cur == mx, iota, big), axis=axis, keepdims=True)
    return iota == ix


def _router_kernel(u_ref, wt_ref, bias_ref, tri_ref, gate_o, idx_o, rank_o, cnt_o):
    n_e = wt_ref.shape[0]
    tr = u_ref.shape[0]
    per_g = n_e // N_GROUPS
    neg = -jnp.inf

    @pl.when(pl.program_id(0) == 0)
    def _():
        cnt_o[...] = jnp.zeros_like(cnt_o)

    logits = lax.dot_general(wt_ref[...], u_ref[...], (((1,), (1,)), ((), ())),
                             precision=HP, preferred_element_type=F32)
    scores = jax.nn.sigmoid(logits)
    biased = scores + bias_ref[...]
    b3 = biased.reshape(N_GROUPS, per_g, tr)
    ie = lax.broadcasted_iota(jnp.int32, b3.shape, 1)
    p1 = _argmax_pick(b3, ie, per_g, 1)
    m1 = jnp.max(b3, axis=1)
    m2 = jnp.max(jnp.where(p1, neg, b3), axis=1)
    cur = m1 + m2
    ig = lax.broadcasted_iota(jnp.int32, cur.shape, 0)
    sel_g = jnp.zeros(cur.shape, F32)
    for _ in range(TOPK_GROUPS):
        pick = _argmax_pick(cur, ig, N_GROUPS, 0)
        sel_g = jnp.where(pick, 1.0, sel_g)
        cur = jnp.where(pick, neg, cur)
    cur = jnp.where(sel_g[:, None, :] > 0.5, b3, neg).reshape(n_e, tr)
    i_e = lax.broadcasted_iota(jnp.int32, cur.shape, 0)
    i_ef = i_e.astype(F32)
    picks, top_s = [], []
    sel_f = jnp.zeros(cur.shape, F32)
    for _ in range(TOP_K):
        pick = _argmax_pick(cur, i_e, n_e, 0)
        picks.append(pick)
        top_s.append(jnp.sum(jnp.where(pick, scores, 0.0), axis=0, keepdims=True))
        sel_f = jnp.where(pick, 1.0, sel_f)
        cur = jnp.where(pick, neg, cur)
    denom = top_s[0]
    for s in top_s[1:]:
        denom = denom + s
    rank = jnp.dot(sel_f.astype(BF16), tri_ref[...], preferred_element_type=F32) + cnt_o[:, 0:1]
    cnt_o[...] = cnt_o[...] + jnp.sum(sel_f, axis=1, keepdims=True)
    zrow = jnp.zeros((SUBLANE - TOP_K, tr), F32)
    gate_o[...] = jnp.concatenate([s / denom * ROUTED_SCALE for s in top_s] + [zrow], axis=0)
    idx_rows = [jnp.sum(jnp.where(p, i_ef, 0.0), axis=0, keepdims=True) for p in picks]
    idx_o[...] = jnp.concatenate(idx_rows + [zrow], axis=0).astype(jnp.int32)
    rank_rows = [jnp.sum(jnp.where(p, rank, 0.0), axis=0, keepdims=True) for p in picks]
    rank_o[...] = jnp.concatenate(rank_rows + [zrow], axis=0).astype(jnp.int32)


def _router(u, w_t, bias, tr=256):
    m, d = u.shape
    n_e = w_t.shape[0]
    tri = (jnp.arange(tr)[:, None] < jnp.arange(tr)[None, :]).astype(BF16)
    tok = pl.BlockSpec((SUBLANE, tr), lambda i: (0, i))
    return pl.pallas_call(
        _router_kernel,
        grid=(m // tr,),
        in_specs=[pl.BlockSpec((tr, d), lambda i: (i, 0)),
                  pl.BlockSpec((n_e, d), lambda i: (0, 0)),
                  pl.BlockSpec((n_e, 1), lambda i: (0, 0)),
                  pl.BlockSpec((tr, tr), lambda i: (0, 0))],
        out_specs=[tok, tok, tok, pl.BlockSpec((n_e, LANE), lambda i: (0, 0))],
        out_shape=[jax.ShapeDtypeStruct((SUBLANE, m), F32),
                   jax.ShapeDtypeStruct((SUBLANE, m), jnp.int32),
                   jax.ShapeDtypeStruct((SUBLANE, m), jnp.int32),
                   jax.ShapeDtypeStruct((n_e, LANE), F32)],
        compiler_params=_cparams("arbitrary"),
        name="router",
    )(u, w_t, bias, tri)


def _final_kernel(ye_ref, gate_ref, ub_ref, h_ref, wgu_ref, wdn_ref, gt_ref, g_ref, o_ref):
    tt, bsz, d = h_ref.shape
    ff = wdn_ref.shape[0]
    routed = jnp.zeros((tt * bsz, d), F32)
    for j in range(TOP_K):
        routed = routed + gate_ref[:, j:j + 1] * ye_ref[:, j * d:(j + 1) * d].astype(F32)
    gu = jnp.dot(ub_ref[...], wgu_ref[...], preferred_element_type=F32)
    g = gu[:, :ff]
    act = (g * jax.nn.sigmoid(g) * gu[:, ff:]).astype(BF16)
    moe = (routed + jnp.dot(act, wdn_ref[...], preferred_element_type=F32)).reshape(tt, bsz, d)
    moe = moe * lax.rsqrt(jnp.mean(moe * moe, axis=-1, keepdims=True) + NORM_EPS) * g_ref[...]
    o_ref[...] = h_ref[...] + gt_ref[...] * moe


def _final(ye, gate, ub, h, wgu, wdn, gt, g, tt=32):
    t, bsz, d = h.shape
    rows = tt * bsz
    const = lambda shape: pl.BlockSpec(shape, lambda i: (0,) * len(shape))
    return pl.pallas_call(
        _final_kernel,
        grid=(t // tt,),
        in_specs=[pl.BlockSpec((rows, TOP_K * d), lambda i: (i, 0)),
                  pl.BlockSpec((rows, SUBLANE), lambda i: (i, 0)),
                  pl.BlockSpec((rows, d), lambda i: (i, 0)),
                  pl.BlockSpec((tt, bsz, d), lambda i: (i, 0, 0)),
                  const(wgu.shape), const(wdn.shape), const(gt.shape), const(g.shape)],
        out_specs=pl.BlockSpec((tt, bsz, d), lambda i: (i, 0, 0)),
        out_shape=jax.ShapeDtypeStruct((t, bsz, d), F32),
        compiler_params=_cparams("parallel"),
        name="moe_final",
    )(ye, gate, ub, h, wgu, wdn, gt, g)


def kernel(x, c, ctx, c_ctx, w_mod, b_mod, norm_g, w_in, shift_mu, rw_w0, rw_w_up, rw_a0, rw_a_up,
           rw_g_up, rw_k_k, rw_k_a, rw_r_k, rw_lnx, lru_conv_w, lru_conv_b, lru_gate_w, lru_gate_b,
           lru_l, w_branch_a, w_branch_b, w_out, router_w, router_b, ex_w_gu, ex_w_down,
           sh_w_gu, sh_w_down):
    l = 0
    bsz, t, d = x.shape
    tc = ctx.shape[1]
    rows_g = t // GRID_W
    m = bsz * t
    cw = rw_w0.shape[-1]
    cl = lru_l.shape[-1]
    nh = cw // RWKV_HEAD
    assert bsz == SUBLANE and cw == cl == d and 2 * nh * bsz // 2 == LANE
    rwkv_cols = 3 * cw + 2 * DECAY_LORA + 2 * AAA_LORA + GATE_LORA
    lora_cols = rwkv_cols - 3 * cw

    mod = jnp.dot(jax.nn.silu(c), w_mod[l], precision=HP) + b_mod[l]
    mod_c = jnp.dot(jax.nn.silu(c_ctx), w_mod[l], precision=HP) + b_mod[l]
    sh1, sc1, gt1, sh2, sc2, gt2 = jnp.split(mod, 6, axis=-1)
    csh1 = jnp.broadcast_to(mod_c[0:d], (bsz, d))
    csc1 = jnp.broadcast_to(mod_c[d:2 * d], (bsz, d))
    g_pre1, g_post1, g_pre2, g_post2 = [g[None, :] for g in norm_g[l]]

    wi = w_in[l]
    w_al = jnp.concatenate([wi[:, :rwkv_cols], jnp.zeros((d, cw - lora_cols), F32),
                            wi[:, rwkv_cols:]], axis=1).astype(BF16)
    zr_w = 4 * cw
    mu_al = jnp.concatenate([shift_mu[l], jnp.zeros((2, cw - lora_cols), F32)], axis=1)

    x_tm = jnp.transpose(x, (1, 0, 2))
    ctx_tm = jnp.transpose(ctx, (1, 0, 2))
    zc = _inproj(ctx_tm, 1.0 + csc1, csh1, g_pre1, w_al[:, :zr_w + cl])
    z = _inproj(x_tm, 1.0 + sc1, sh1, g_pre1, w_al)

    blk = cl // LRU_BLOCKS
    per = MXU_N // blk
    gw = lru_gate_w[l].reshape(2, 2, LRU_BLOCKS // per, per, blk, blk)
    eye = jnp.eye(per, dtype=F32)
    gw_bd = jnp.einsum('dgqpjk,pr->dgqpjrk', gw, eye).reshape(2, 2, LRU_BLOCKS // per, MXU_N, MXU_N)
    gw_bd = gw_bd.astype(BF16)
    gb = lru_gate_b[l].reshape(2, 2, cl)
    sp_l = jax.nn.softplus(-lru_l[l])[:, None, :]
    lru_args = (lru_conv_w[l], lru_conv_b[l][None, :], gw_bd, gb, sp_l)
    _, hfin = _lru(zc.reshape(tc, 1, bsz, -1), 4, *lru_args, jnp.zeros((2, bsz, cl), F32))
    h_lru, _ = _lru(z.reshape(rows_g, GRID_W, bsz, -1), 4, *lru_args, hfin)
    h_lru = h_lru.reshape(2, m, cl)

    zeros_c = jnp.zeros((DECAY_LORA, cw), F32)
    wup2 = jnp.concatenate([jnp.concatenate([rw_w_up[l, 0], zeros_c], 1),
                            jnp.concatenate([zeros_c, rw_w_up[l, 1]], 1)], axis=0)
    aup2 = jnp.concatenate([jnp.concatenate([rw_a_up[l, 0], zeros_c], 1),
                            jnp.concatenate([zeros_c, rw_a_up[l, 1]], 1)], axis=0)
    gup = jnp.concatenate([rw_g_up[l], jnp.zeros((MXU_N - GATE_LORA, cw), F32)], axis=0)
    hid = jnp.arange(MXU_N) // RWKV_HEAD
    ones_blk = (hid[:, None] == hid[None, :]).astype(BF16)
    prep_args = (mu_al, rw_w0[l], wup2.astype(BF16), rw_a0[l], aup2.astype(BF16),
                 rw_k_k[l][None, :], rw_k_a[l][None, :], rw_r_k[l].reshape(1, cw),
                 gup.astype(BF16), ones_blk)
    pc = _rwkv_prep(zc, tc, zr_w, *prep_args)
    pz = _rwkv_prep(z, GRID_W, zr_w, *prep_args)
    s_zero = jnp.zeros((2, RWKV_HEAD, RWKV_HEAD, LANE), F32)
    ys = []
    for rev in (False, True):
        _, s_c = _rwkv_scan(rev, pc[0], pc[3], pc[4], pc[1], pc[2], pc[5], s_zero)
        s_init = jnp.stack([s_c, s_c], axis=0)
        y_d, _ = _rwkv_scan(rev, pz[0], pz[3], pz[4], pz[1], pz[2], pz[5], s_init)
        ys.append(y_d)

    h, u2, u2b = _merge(ys[0], ys[1], pz[6], pz[7], h_lru, z, x_tm, rw_lnx[l],
                        w_branch_a[l].astype(BF16), w_branch_b[l].astype(BF16), w_out[l].astype(BF16),
                        gt1, g_post1, g_pre2, 1.0 + sc2, sh2)

    n_e = router_w.shape[-1]
    gate6, idx6, rank6, cnt = _router(u2, router_w[l].T, router_b[l][:, None])
    counts = cnt[:, 0].astype(jnp.int32)
    padded = (counts + MOE_ROWS - 1) // MOE_ROWS * MOE_ROWS
    pad_end = jnp.cumsum(padded)
    pad_start = pad_end - padded
    start = jnp.cumsum(counts) - counts
    n_assign = m * TOP_K
    n_blocks = (n_assign + n_e * (MOE_ROWS - 1) + MOE_ROWS - 1) // MOE_ROWS
    p_rows = n_blocks * MOE_ROWS
    blk_e = jnp.minimum(jnp.searchsorted(pad_end, jnp.arange(n_blocks) * MOE_ROWS, side='right'),
                        n_e - 1).astype(jnp.int32)
    flat_e = idx6[:TOP_K].T.reshape(-1)
    order = jnp.argsort(flat_e).astype(jnp.int32)
    p_idx = jnp.arange(p_rows, dtype=jnp.int32)
    p_e = jnp.repeat(blk_e, MOE_ROWS)
    in_seg = p_idx - pad_start[p_e]
    src = jnp.clip(in_seg + start[p_e], 0, n_assign - 1)
    buf_tok = jnp.where(in_seg < counts[p_e], order[src] // TOP_K, 0)
    ys_e = _grouped_experts(u2b[buf_tok], blk_e, ex_w_gu[l], ex_w_down[l])
    pos = (pad_start[idx6[:TOP_K]] + rank6[:TOP_K]).T
    ye = ys_e[pos].reshape(m, TOP_K * d)
    out_tm = _final(ye, gate6.T, u2b, h, sh_w_gu[l].astype(BF16), sh_w_down[l].astype(BF16),
                    gt2, g_post2)
    return jnp.transpose(out_tm, (1, 0, 2))
```

```python
import functools

import jax
import jax.numpy as jnp
from jax import lax
from jax.experimental import pallas as pl
from jax.experimental.pallas import tpu as pltpu

GRID_W = 64
NORM_EPS = 1e-6
RWKV_HEAD = 64
DECAY_LORA = 64
AAA_LORA = 64
GATE_LORA = 160
LNX_EPS = 64e-5
LRU_BLOCKS = 16
CONV_LEFT = 2
LRU_C = 8.0
N_GROUPS = 8
TOPK_GROUPS = 4
TOP_K = 6
ROUTED_SCALE = 2.5
DECAY_SCALE = 0.6065306597126334

LANE = 128
SUBLANE = 8
MXU_N = 256
MOE_ROWS = 512
KCH = 32
VMEM_LIMIT = 48 * 1024 * 1024
HP = lax.Precision.HIGHEST
F32 = jnp.float32
BF16 = jnp.bfloat16


def _cparams(*sem):
    return pltpu.CompilerParams(dimension_semantics=sem, vmem_limit_bytes=VMEM_LIMIT)


def _inproj_kernel(x_ref, scp_ref, sh_ref, g_ref, w_ref, o_ref, xm_ref):
    @pl.when(pl.program_id(1) == 0)
    def _():
        x = x_ref[...]
        ms = jnp.mean(x * x, axis=-1, keepdims=True)
        xm = x * lax.rsqrt(ms + NORM_EPS) * g_ref[...] * scp_ref[...] + sh_ref[...]
        xm_ref[...] = xm.reshape(xm_ref.shape).astype(BF16)

    o_ref[...] = jnp.dot(xm_ref[...], w_ref[...], preferred_element_type=F32)


def _inproj(x_tm, scp, sh, g, w, tt=128, tn=1024):
    t, bsz, d = x_tm.shape
    n = w.shape[1]
    tt = min(tt, t)
    assert t % tt == 0 and n % tn == 0
    return pl.pallas_call(
        _inproj_kernel,
        grid=(t // tt, n // tn),
        in_specs=[pl.BlockSpec((tt, bsz, d), lambda i, j: (i, 0, 0)),
                  pl.BlockSpec((bsz, d), lambda i, j: (0, 0)),
                  pl.BlockSpec((bsz, d), lambda i, j: (0, 0)),
                  pl.BlockSpec((1, d), lambda i, j: (0, 0)),
                  pl.BlockSpec((d, tn), lambda i, j: (0, j))],
        out_specs=pl.BlockSpec((tt * bsz, tn), lambda i, j: (i, j)),
        out_shape=jax.ShapeDtypeStruct((t * bsz, n), F32),
        scratch_shapes=[pltpu.VMEM((tt * bsz, d), BF16)],
        compiler_params=_cparams("parallel", "arbitrary"),
        name="in_proj",
    )(x_tm, scp, sh, g, w)


def _lru_kernel(z_ref, cw_ref, cb_ref, gw_ref, gb_ref, sp_ref, h0_ref, h_ref, hf_ref,
                a_ref, bx_ref):
    d = pl.program_id(0)
    w = pl.program_id(2)
    nr, _, bsz, cq = z_ref.shape
    taps = cw_ref.shape[0]

    @pl.when(w == 0)
    def _():
        hf_ref[0] = h0_ref[0]

    zx = z_ref[:, 0]
    xc = jnp.zeros_like(zx) + cb_ref[...]
    for j in range(taps):
        off = j - CONV_LEFT
        if off < 0:
            sh = jnp.concatenate([jnp.zeros((-off, bsz, cq), F32), zx[:nr + off]], axis=0)
        elif off > 0:
            sh = jnp.concatenate([zx[off:], jnp.zeros((off, bsz, cq), F32)], axis=0)
        else:
            sh = zx
        xc = xc + cw_ref[j:j + 1, :] * sh
    x2 = xc.reshape(nr * bsz, cq)
    xb = x2.astype(BF16)
    pre = []
    for g in range(2):
        cols = []
        for q in range(cq // MXU_N):
            cols.append(jnp.dot(xb[:, q * MXU_N:(q + 1) * MXU_N], gw_ref[0, g, q],
                                preferred_element_type=F32))
        pre.append(jnp.concatenate(cols, axis=1) + gb_ref[0, g:g + 1, :])
    rec = jax.nn.sigmoid(pre[0])
    inp = jax.nn.sigmoid(pre[1])
    log_a = -LRU_C * rec * sp_ref[0]
    a = jnp.exp(log_a)
    bx = jnp.sqrt(-jnp.tanh(log_a) * (a * a + 1.0)) * inp * x2
    a_ref[...] = a.reshape(nr, bsz, cq)
    bx_ref[...] = bx.reshape(nr, bsz, cq)

    def step(i, h):
        r = i + d * (nr - 1 - 2 * i)
        h = a_ref[r] * h + bx_ref[r]
        h_ref[0, r, 0] = h
        return h

    hf_ref[0] = lax.fori_loop(0, nr, step, hf_ref[0])


def _lru(z4, col_blk, conv_w, conv_b, gate_w, gate_b, sp_l, h0, cq=512):
    nr, nw, bsz, _ = z4.shape
    c = conv_w.shape[1]
    nq = c // cq
    cb0 = col_blk * (c // cq)

    def zmap(d, q, w):
        return (0, w + d * (nw - 1 - 2 * w), 0, cb0 + q)

    def hmap(d, q, w):
        return (d, 0, w + d * (nw - 1 - 2 * w), 0, q)

    return pl.pallas_call(
        _lru_kernel,
        grid=(2, nq, nw),
        in_specs=[pl.BlockSpec((nr, 1, bsz, cq), zmap),
                  pl.BlockSpec((conv_w.shape[0], cq), lambda d, q, w: (0, q)),
                  pl.BlockSpec((1, cq), lambda d, q, w: (0, q)),
                  pl.BlockSpec((1, 2, cq // MXU_N, MXU_N, MXU_N), lambda d, q, w: (d, 0, q, 0, 0)),
                  pl.BlockSpec((1, 2, cq), lambda d, q, w: (d, 0, q)),
                  pl.BlockSpec((1, 1, cq), lambda d, q, w: (d, 0, q)),
                  pl.BlockSpec((1, bsz, cq), lambda d, q, w: (d, 0, q))],
        out_specs=[pl.BlockSpec((1, nr, 1, bsz, cq), hmap),
                   pl.BlockSpec((1, bsz, cq), lambda d, q, w: (d, 0, q))],
        out_shape=[jax.ShapeDtypeStruct((2, nr, nw, bsz, c), F32),
                   jax.ShapeDtypeStruct((2, bsz, c), F32)],
        scratch_shapes=[pltpu.VMEM((nr, bsz, cq), F32), pltpu.VMEM((nr, bsz, cq), F32)],
        compiler_params=_cparams("arbitrary", "arbitrary", "arbitrary"),
        name="lru_fused",
    )(z4, conv_w, conv_b, gate_w, gate_b, sp_l, h0)


def _to_pairs(q, out_ref, lead=()):
    rows, c = q.shape
    npairs = rows // (2 * SUBLANE)
    lane = lax.broadcasted_iota(jnp.int32, (SUBLANE, LANE), 1)
    low = lane < RWKV_HEAD
    for p in range(npairs):
        ev = q[(2 * p) * SUBLANE:(2 * p + 1) * SUBLANE]
        od = q[(2 * p + 1) * SUBLANE:(2 * p + 2) * SUBLANE]
        pieces = []
        for j in range(c // LANE):
            e_j = ev[:, j * LANE:(j + 1) * LANE]
            o_j = od[:, j * LANE:(j + 1) * LANE]
            pieces.append(jnp.where(low, e_j, pltpu.roll(o_j, RWKV_HEAD, 1)))
            pieces.append(jnp.where(low, pltpu.roll(e_j, RWKV_HEAD, 1), o_j))
        x = jnp.concatenate(pieces, axis=0)
        out_ref[lead + (p,)] = x.T


def _group_sum(x, ones_ref):
    hi = x.astype(BF16)
    lo = (x - hi.astype(F32)).astype(BF16)
    cols = []
    for q in range(x.shape[1] // MXU_N):
        sl = slice(q * MXU_N, (q + 1) * MXU_N)
        cols.append(jnp.dot(hi[:, sl], ones_ref[...], preferred_element_type=F32)
                    + jnp.dot(lo[:, sl], ones_ref[...], preferred_element_type=F32))
    return jnp.concatenate(cols, axis=1)


def _prep_kernel(period, z_ref, zp_ref, zn_ref, mu_ref, w0_ref, wup_ref, a0_ref, aup_ref,
                 kk_ref, ka_ref, rk_ref, gup_ref, ones_ref,
                 r_o, v_o, kn_o, w_o, kd_o, b_o, bv_o, g_o):
    i = pl.program_id(0)
    rows, _ = z_ref.shape
    tt = rows // SUBLANE
    cw = w0_ref.shape[1]
    z = z_ref[...]
    has_prev = ((i * tt) % period != 0).astype(F32)
    has_next = (((i + 1) * tt) % period != 0).astype(F32)
    zprev = jnp.concatenate([zp_ref[...] * has_prev, z[:rows - SUBLANE]], axis=0)
    znext = jnp.concatenate([z[SUBLANE:], zn_ref[...] * has_next], axis=0)
    zs = z + mu_ref[0:1, :] * (zprev - z) + mu_ref[1:2, :] * (znext - z)
    r = zs[:, 0:cw]
    k = zs[:, cw:2 * cw]
    v = zs[:, 2 * cw:3 * cw]
    off = 3 * cw
    wd = jnp.tanh(zs[:, off:off + LANE])
    ad = zs[:, off + LANE:off + 2 * LANE]
    gd = jax.nn.sigmoid(zs[:, off + 2 * LANE:off + 2 * LANE + MXU_N])
    wl = jnp.dot(wd.astype(BF16), wup_ref[...], preferred_element_type=F32)
    al = jnp.dot(ad.astype(BF16), aup_ref[...], preferred_element_type=F32)
    g_o[...] = jnp.dot(gd.astype(BF16), gup_ref[...], preferred_element_type=F32)
    kk = k * kk_ref[...]
    nrm = jnp.sqrt(_group_sum(kk * kk, ones_ref))
    kn = kk / jnp.maximum(nrm, 1e-12)
    _to_pairs(r, r_o)
    _to_pairs(v, v_o)
    _to_pairs(kn, kn_o)
    ksum = jnp.zeros_like(k)
    for d in range(2):
        xw = w0_ref[d:d + 1, :] + wl[:, d * cw:(d + 1) * cw]
        decay = jnp.exp(-DECAY_SCALE * jax.nn.sigmoid(xw))
        a = jax.nn.sigmoid(a0_ref[d:d + 1, :] + al[:, d * cw:(d + 1) * cw])
        kd = k * (1.0 + (a - 1.0) * ka_ref[...])
        ksum = ksum + kd
        _to_pairs(decay, w_o, (d,))
        _to_pairs(kd, kd_o, (d,))
        _to_pairs(kn * a, b_o, (d,))
    bonus = _group_sum(r * ksum * rk_ref[...], ones_ref)
    bv_o[...] = bonus * v


def _rwkv_prep(z, period, zcols, mu, w0, wup, a0, aup, k_k, k_a, r_k, gup, ones, tt=16):
    rows_total = z.shape[0]
    t = rows_total // SUBLANE
    cw = w0.shape[1]
    tt = min(tt, period)
    assert t % tt == 0 and period % tt == 0 and tt % 2 == 0
    rows = tt * SUBLANE
    nblk = t // tt
    const = lambda shape: pl.BlockSpec(shape, lambda i: (0,) * len(shape))
    pair = pl.BlockSpec((tt // 2, LANE, LANE), lambda i: (i, 0, 0))
    pair_d = pl.BlockSpec((2, tt // 2, LANE, LANE), lambda i: (0, i, 0, 0))
    nat = pl.BlockSpec((rows, cw), lambda i: (i, 0))
    sh_pair = jax.ShapeDtypeStruct((t // 2, LANE, LANE), F32)
    sh_pair_d = jax.ShapeDtypeStruct((2, t // 2, LANE, LANE), F32)
    sh_nat = jax.ShapeDtypeStruct((rows_total, cw), F32)
    return pl.pallas_call(
        functools.partial(_prep_kernel, period),
        grid=(nblk,),
        in_specs=[pl.BlockSpec((rows, zcols), lambda i: (i, 0)),
                  pl.BlockSpec((SUBLANE, zcols), lambda i: (jnp.maximum(i * tt - 1, 0), 0)),
                  pl.BlockSpec((SUBLANE, zcols), lambda i: (jnp.minimum((i + 1) * tt, t - 1), 0)),
                  const(mu.shape), const(w0.shape), const(wup.shape), const(a0.shape),
                  const(aup.shape), const(k_k.shape), const(k_a.shape), const(r_k.shape),
                  const(gup.shape), const(ones.shape)],
        out_specs=[pair, pair, pair, pair_d, pair_d, pair_d, nat, nat],
        out_shape=[sh_pair, sh_pair, sh_pair, sh_pair_d, sh_pair_d, sh_pair_d, sh_nat, sh_nat],
        compiler_params=_cparams("parallel"),
        name="rwkv_prep",
    )(z, z, z, mu, w0, wup, a0, aup, k_k, k_a, r_k, gup, ones)


def _scan_kernel(reverse, r_ref, w_ref, k_ref, v_ref, kk_ref, b_ref, s0_ref, y_ref, s_ref):
    n = RWKV_HEAD
    npairs = r_ref.shape[0]

    @pl.when(pl.program_id(0) == 0)
    def _():
        s_ref[...] = s0_ref[...]

    def removal(p, base):
        def body(kb, acc):
            k0 = pl.multiple_of(kb * KCH, KCH)
            for j in range(KCH):
                acc = acc + s_ref[k0 + j] * kk_ref[p, pl.ds(base + k0 + j, 1), :]
            return acc

        return lax.fori_loop(0, n // KCH, body, jnp.zeros((n, LANE), F32))

    def one_step(p, base, pn, basen, acc):
        sa = -acc
        vv = v_ref[p, pl.ds(base, n), :]

        def body(kb, carry):
            y, nacc = carry
            k0 = pl.multiple_of(kb * KCH, KCH)
            for j in range(KCH):
                row = pl.ds(base + k0 + j, 1)
                sk = (s_ref[k0 + j] * w_ref[0, p, row, :]
                      + (sa * b_ref[0, p, row, :] + vv * k_ref[0, p, row, :]))
                s_ref[k0 + j] = sk
                y = y + sk * r_ref[p, row, :]
                nacc = nacc + sk * kk_ref[pn, pl.ds(basen + k0 + j, 1), :]
            return y, nacc

        zero = jnp.zeros((n, LANE), F32)
        y, nacc = lax.fori_loop(0, n // KCH, body, (zero, zero))
        y_ref[p, pl.ds(base, n), :] = y
        return nacc

    first, second = ((n, 0) if reverse else (0, n))
    p_first = (npairs - 1) if reverse else 0

    def pair_step(i, acc):
        p = (npairs - 1 - i) if reverse else i
        pn = jnp.clip(p + (-1 if reverse else 1), 0, npairs - 1)
        acc = one_step(p, first, p, second, acc)
        return one_step(p, second, pn, first, acc)

    lax.fori_loop(0, npairs, pair_step, removal(p_first, first))


def _rwkv_scan(reverse, r, w, k, v, kk, b, s0, tbp=16):
    np_total = r.shape[0]
    tbp = min(tbp, np_total)
    assert np_total % tbp == 0
    nblk = np_total // tbp
    d = 1 if reverse else 0
    imap = (lambda i: (nblk - 1 - i, 0, 0)) if reverse else (lambda i: (i, 0, 0))
    imap_d = (lambda i: (d, nblk - 1 - i, 0, 0)) if reverse else (lambda i: (d, i, 0, 0))
    sp = pl.BlockSpec((tbp, LANE, LANE), imap)
    sp_d = pl.BlockSpec((1, tbp, LANE, LANE), imap_d)
    st = pl.BlockSpec((RWKV_HEAD, RWKV_HEAD, LANE), lambda i: (0, 0, 0))
    return pl.pallas_call(
        functools.partial(_scan_kernel, reverse),
        grid=(nblk,),
        in_specs=[sp, sp_d, sp_d, sp, sp, sp_d, st],
        out_specs=[sp, st],
        out_shape=[jax.ShapeDtypeStruct(r.shape, F32),
                   jax.ShapeDtypeStruct((RWKV_HEAD, RWKV_HEAD, LANE), F32)],
        compiler_params=_cparams("arbitrary"),
        name="rwkv_scan",
    )(r, w, k, v, kk, b, s0)


def _from_pairs(y_ref, nat_ref):
    npairs = y_ref.shape[0]
    lane = lax.broadcasted_iota(jnp.int32, (SUBLANE, LANE), 1)
    low = lane < RWKV_HEAD
    for p in range(npairs):
        xt = y_ref[p].T
        for j in range(nat_ref.shape[1] // LANE):
            h0 = xt[(2 * j) * SUBLANE:(2 * j + 1) * SUBLANE]
            h1 = xt[(2 * j + 1) * SUBLANE:(2 * j + 2) * SUBLANE]
            ev = jnp.where(low, h0, pltpu.roll(h1, RWKV_HEAD, 1))
            od = jnp.where(low, pltpu.roll(h0, RWKV_HEAD, 1), h1)
            nat_ref[(2 * p) * SUBLANE:(2 * p + 1) * SUBLANE, j * LANE:(j + 1) * LANE] = ev
            nat_ref[(2 * p + 1) * SUBLANE:(2 * p + 2) * SUBLANE, j * LANE:(j + 1) * LANE] = od


def _merge_kernel(y0_ref, y1_ref, bv_ref, go_ref, h0_ref, h1_ref, zg_ref, zgate_ref, x_ref,
                  lnx_ref, wa_ref, wb_ref, wo_ref, gt1_ref, gpost_ref, gpre_ref, scp_ref, sh_ref,
                  h_o, u_o, ub_o, yn_ref, nat_ref):
    n = RWKV_HEAD
    d = wa_ref.shape[1]
    npairs = y0_ref.shape[0]
    tt, bsz, _ = x_ref.shape
    for p in range(npairs):
        y = y0_ref[p] + y1_ref[p]
        for t2 in range(2):
            yh = y[t2 * n:(t2 + 1) * n]
            mu = jnp.mean(yh, axis=0, keepdims=True)
            var = jnp.mean(jnp.square(yh - mu), axis=0, keepdims=True)
            yn_ref[p, t2 * n:(t2 + 1) * n, :] = (yh - mu) * lax.rsqrt(var + LNX_EPS)
    _from_pairs(yn_ref, nat_ref)
    y_a = (nat_ref[...] * lnx_ref[0:1, :] + lnx_ref[1:2, :] + bv_ref[...]) * go_ref[...]
    pa = jnp.dot(y_a.astype(BF16), wa_ref[...], preferred_element_type=F32)
    y_b = jax.nn.gelu(zg_ref[...]) * (h0_ref[0] + h1_ref[0])
    pb = jnp.dot(y_b.astype(BF16), wb_ref[...], preferred_element_type=F32)
    gates = jax.nn.sigmoid(zgate_ref[...])
    mixed = gates[:, :d] * pa + gates[:, d:] * pb
    yo = jnp.dot(mixed.astype(BF16), wo_ref[...], preferred_element_type=F32).reshape(tt, bsz, d)
    yo = yo * lax.rsqrt(jnp.mean(yo * yo, axis=-1, keepdims=True) + NORM_EPS) * gpost_ref[...]
    h = x_ref[...] + gt1_ref[...] * yo
    h_o[...] = h
    u = h * lax.rsqrt(jnp.mean(h * h, axis=-1, keepdims=True) + NORM_EPS) * gpre_ref[...]
    u = (u * scp_ref[...] + sh_ref[...]).reshape(tt * bsz, d)
    u_o[...] = u
    ub_o[...] = u.astype(BF16)


def _merge(y0, y1, bv, g_out, h_lru, z, x_tm, lnx, wa, wb, wo, gt1, g_post, g_pre, scp, sh, tt=32):
    t, bsz, d = x_tm.shape
    rows = tt * bsz
    c = wa.shape[0]
    zcol = lambda blk, width: pl.BlockSpec((rows, width), lambda i: (i, blk))
    const = lambda shape: pl.BlockSpec(shape, lambda i: (0,) * len(shape))
    pair = pl.BlockSpec((tt // 2, LANE, LANE), lambda i: (i, 0, 0))
    nat = pl.BlockSpec((rows, c), lambda i: (i, 0))
    lg_blk = (3 * c + c + c) // c
    gate_blk = (lg_blk + 1) * c // (2 * d)
    return pl.pallas_call(
        _merge_kernel,
        grid=(t // tt,),
        in_specs=[pair, pair, nat, nat,
                  pl.BlockSpec((1, rows, c), lambda i: (0, i, 0)),
                  pl.BlockSpec((1, rows, c), lambda i: (1, i, 0)),
                  zcol(lg_blk, c), zcol(gate_blk, 2 * d),
                  pl.BlockSpec((tt, bsz, d), lambda i: (i, 0, 0)),
                  const(lnx.shape), const(wa.shape), const(wb.shape), const(wo.shape),
                  const(gt1.shape), const(g_post.shape), const(g_pre.shape),
                  const(scp.shape), const(sh.shape)],
        out_specs=[pl.BlockSpec((tt, bsz, d), lambda i: (i, 0, 0)),
                   pl.BlockSpec((rows, d), lambda i: (i, 0)),
                   pl.BlockSpec((rows, d), lambda i: (i, 0))],
        out_shape=[jax.ShapeDtypeStruct((t, bsz, d), F32),
                   jax.ShapeDtypeStruct((t * bsz, d), F32),
                   jax.ShapeDtypeStruct((t * bsz, d), BF16)],
        scratch_shapes=[pltpu.VMEM((tt // 2, LANE, LANE), F32), pltpu.VMEM((rows, c), F32)],
        compiler_params=_cparams("parallel"),
        name="merge",
    )(y0, y1, bv, g_out, h_lru, h_lru, z, z, x_tm, lnx, wa, wb, wo, gt1, g_post, g_pre, scp, sh)


def _experts_kernel(blk_ref, eid_ref, lo_ref, hi_ref, x_ref, wgu_ref, wdn_ref, o_ref, wgu_b, wdn_b):
    i = pl.program_id(0)
    prev = jnp.maximum(i - 1, 0)
    ff = wdn_ref.shape[1]

    @pl.when(jnp.logical_or(i == 0, eid_ref[i] != eid_ref[prev]))
    def _():
        wgu_b[...] = wgu_ref[0].astype(BF16)
        wdn_b[...] = wdn_ref[0].astype(BF16)

    gu = jnp.dot(x_ref[...], wgu_b[...], preferred_element_type=F32)
    g = gu[:, :ff]
    u = gu[:, ff:]
    act = (g * jax.nn.sigmoid(g) * u).astype(BF16)
    res = jnp.dot(act, wdn_b[...], preferred_element_type=F32)
    row = lax.broadcasted_iota(jnp.int32, (res.shape[0], 1), 0)
    mine = jnp.logical_and(row >= lo_ref[i], row < hi_ref[i])
    first = jnp.logical_or(i == 0, blk_ref[i] != blk_ref[prev])

    @pl.when(first)
    def _():
        o_ref[...] = jnp.where(mine, res, 0.0).astype(o_ref.dtype)

    @pl.when(jnp.logical_not(first))
    def _():
        o_ref[...] = jnp.where(mine, res, o_ref[...].astype(F32)).astype(o_ref.dtype)


def _grouped_experts(xs, items, w_gu, w_down):
    n_rows, d = xs.shape
    _, _, ff2 = w_gu.shape
    ff = ff2 // 2
    n_items = items[0].shape[0]
    grid_spec = pltpu.PrefetchScalarGridSpec(
        num_scalar_prefetch=4,
        grid=(n_items,),
        in_specs=[pl.BlockSpec((MOE_ROWS, d), lambda i, blk, eid, lo, hi: (blk[i], 0)),
                  pl.BlockSpec((1, d, ff2), lambda i, blk, eid, lo, hi: (eid[i], 0, 0)),
                  pl.BlockSpec((1, ff, d), lambda i, blk, eid, lo, hi: (eid[i], 0, 0))],
        out_specs=pl.BlockSpec((MOE_ROWS, d), lambda i, blk, eid, lo, hi: (blk[i], 0)),
        scratch_shapes=[pltpu.VMEM((d, ff2), BF16), pltpu.VMEM((ff, d), BF16)],
    )
    return pl.pallas_call(
        _experts_kernel,
        grid_spec=grid_spec,
        out_shape=jax.ShapeDtypeStruct((n_rows, d), BF16),
        compiler_params=_cparams("arbitrary"),
        name="moe_experts",
    )(*items, xs, w_gu, w_down)


def _expert_items(counts, n_rows):
    n_e = counts.shape[0]
    n_blocks = n_rows // MOE_ROWS
    n_items = n_blocks + n_e
    end = jnp.cumsum(counts)
    start = end - counts
    first_blk = start // MOE_ROWS
    per_e = jnp.where(counts > 0, (end - 1) // MOE_ROWS - first_blk + 1, 0)
    item_end = jnp.cumsum(per_e)
    w = jnp.arange(n_items, dtype=jnp.int32)
    eid = jnp.minimum(jnp.sum((item_end[None, :] <= w[:, None]).astype(jnp.int32), axis=1), n_e - 1)
    onehot = (eid[:, None] == jnp.arange(n_e)[None, :]).astype(jnp.int32)
    pick = lambda v: jnp.sum(onehot * v[None, :], axis=1)
    valid = w < item_end[-1]
    blk = jnp.where(valid, pick(first_blk) + w - (pick(item_end) - pick(per_e)), n_blocks - 1)
    lo = jnp.clip(pick(start) - blk * MOE_ROWS, 0, MOE_ROWS)
    hi = jnp.where(valid, jnp.clip(pick(end) - blk * MOE_ROWS, 0, MOE_ROWS), lo)
    i32 = lambda v: v.astype(jnp.int32)
    return (i32(blk), i32(eid), i32(lo), i32(hi)), i32(start)


def _argmax_pick(cur, iota, big, axis):
    mx = jnp.max(cur, axis=axis, keepdims=True)
    ix = jnp.min(jnp.where(cur == mx, iota, big), axis=axis, keepdims=True)
    return iota == ix


def _router_kernel(u_ref, wt_ref, bias_ref, tri_ref, gate_o, idx_o, rank_o, cnt_o):
    n_e = wt_ref.shape[0]
    tr = u_ref.shape[0]
    per_g = n_e // N_GROUPS
    neg = -jnp.inf

    @pl.when(pl.program_id(0) == 0)
    def _():
        cnt_o[...] = jnp.zeros_like(cnt_o)

    logits = lax.dot_general(wt_ref[...], u_ref[...], (((1,), (1,)), ((), ())),
                             precision=HP, preferred_element_type=F32)
    scores = jax.nn.sigmoid(logits)
    biased = scores + bias_ref[...]
    b3 = biased.reshape(N_GROUPS, per_g, tr)
    ie = lax.broadcasted_iota(jnp.int32, b3.shape, 1)
    p1 = _argmax_pick(b3, ie, per_g, 1)
    m1 = jnp.max(b3, axis=1)
    m2 = jnp.max(jnp.where(p1, neg, b3), axis=1)
    cur = m1 + m2
    ig = lax.broadcasted_iota(jnp.int32, cur.shape, 0)
    sel_g = jnp.zeros(cur.shape, F32)
    for _ in range(TOPK_GROUPS):
        pick = _argmax_pick(cur, ig, N_GROUPS, 0)
        sel_g = jnp.where(pick, 1.0, sel_g)
        cur = jnp.where(pick, neg, cur)
    cur = jnp.where(sel_g[:, None, :] > 0.5, b3, neg).reshape(n_e, tr)
    i_e = lax.broadcasted_iota(jnp.int32, cur.shape, 0)
    i_ef = i_e.astype(F32)
    picks, top_s = [], []
    sel_f = jnp.zeros(cur.shape, F32)
    for _ in range(TOP_K):
        pick = _argmax_pick(cur, i_e, n_e, 0)
        picks.append(pick)
        top_s.append(jnp.sum(jnp.where(pick, scores, 0.0), axis=0, keepdims=True))
        sel_f = jnp.where(pick, 1.0, sel_f)
        cur = jnp.where(pick, neg, cur)
    denom = top_s[0]
    for s in top_s[1:]:
        denom = denom + s
    rank = jnp.dot(sel_f.astype(BF16), tri_ref[...], preferred_element_type=F32) + cnt_o[:, 0:1]
    cnt_o[...] = cnt_o[...] + jnp.sum(sel_f, axis=1, keepdims=True)
    zrow = jnp.zeros((SUBLANE - TOP_K, tr), F32)
    gate_o[...] = jnp.concatenate([s / denom * ROUTED_SCALE for s in top_s] + [zrow], axis=0)
    idx_rows = [jnp.sum(jnp.where(p, i_ef, 0.0), axis=0, keepdims=True) for p in picks]
    idx_o[...] = jnp.concatenate(idx_rows + [zrow], axis=0).astype(jnp.int32)
    rank_rows = [jnp.sum(jnp.where(p, rank, 0.0), axis=0, keepdims=True) for p in picks]
    rank_o[...] = jnp.concatenate(rank_rows + [zrow], axis=0).astype(jnp.int32)


def _router(u, w_t, bias, tr=256):
    m, d = u.shape
    n_e = w_t.shape[0]
    tri = (jnp.arange(tr)[:, None] < jnp.arange(tr)[None, :]).astype(BF16)
    tok = pl.BlockSpec((SUBLANE, tr), lambda i: (0, i))
    return pl.pallas_call(
        _router_kernel,
        grid=(m // tr,),
        in_specs=[pl.BlockSpec((tr, d), lambda i: (i, 0)),
                  pl.BlockSpec((n_e, d), lambda i: (0, 0)),
                  pl.BlockSpec((n_e, 1), lambda i: (0, 0)),
                  pl.BlockSpec((tr, tr), lambda i: (0, 0))],
        out_specs=[tok, tok, tok, pl.BlockSpec((n_e, LANE), lambda i: (0, 0))],
        out_shape=[jax.ShapeDtypeStruct((SUBLANE, m), F32),
                   jax.ShapeDtypeStruct((SUBLANE, m), jnp.int32),
                   jax.ShapeDtypeStruct((SUBLANE, m), jnp.int32),
                   jax.ShapeDtypeStruct((n_e, LANE), F32)],
        compiler_params=_cparams("arbitrary"),
        name="router",
    )(u, w_t, bias, tri)


def _final_kernel(*refs):
    ye_refs = refs[:TOP_K]
    gate_ref, ub_ref, h_ref, wgu_ref, wdn_ref, gt_ref, g_ref, o_ref = refs[TOP_K:]
    tt, bsz, d = h_ref.shape
    ff = wdn_ref.shape[0]
    routed = jnp.zeros((tt * bsz, d), F32)
    for j in range(TOP_K):
        routed = routed + gate_ref[:, j:j + 1] * ye_refs[j][0].astype(F32)
    gu = jnp.dot(ub_ref[...], wgu_ref[...], preferred_element_type=F32)
    g = gu[:, :ff]
    act = (g * jax.nn.sigmoid(g) * gu[:, ff:]).astype(BF16)
    moe = (routed + jnp.dot(act, wdn_ref[...], preferred_element_type=F32)).reshape(tt, bsz, d)
    moe = moe * lax.rsqrt(jnp.mean(moe * moe, axis=-1, keepdims=True) + NORM_EPS) * g_ref[...]
    o_ref[...] = h_ref[...] + gt_ref[...] * moe


def _final(ye, gate, ub, h, wgu, wdn, gt, g, tt=32):
    t, bsz, d = h.shape
    rows = tt * bsz
    const = lambda shape: pl.BlockSpec(shape, lambda i: (0,) * len(shape))
    return pl.pallas_call(
        _final_kernel,
        grid=(t // tt,),
        in_specs=[pl.BlockSpec((1, rows, d), functools.partial(lambda j, i: (j, i, 0), j))
                  for j in range(TOP_K)] + [
                  pl.BlockSpec((rows, SUBLANE), lambda i: (i, 0)),
                  pl.BlockSpec((rows, d), lambda i: (i, 0)),
                  pl.BlockSpec((tt, bsz, d), lambda i: (i, 0, 0)),
                  const(wgu.shape), const(wdn.shape), const(gt.shape), const(g.shape)],
        out_specs=pl.BlockSpec((tt, bsz, d), lambda i: (i, 0, 0)),
        out_shape=jax.ShapeDtypeStruct((t, bsz, d), F32),
        compiler_params=_cparams("parallel"),
        name="moe_final",
    )(*([ye] * TOP_K), gate, ub, h, wgu, wdn, gt, g)


def kernel(x, c, ctx, c_ctx, w_mod, b_mod, norm_g, w_in, shift_mu, rw_w0, rw_w_up, rw_a0, rw_a_up,
           rw_g_up, rw_k_k, rw_k_a, rw_r_k, rw_lnx, lru_conv_w, lru_conv_b, lru_gate_w, lru_gate_b,
           lru_l, w_branch_a, w_branch_b, w_out, router_w, router_b, ex_w_gu, ex_w_down,
           sh_w_gu, sh_w_down):
    l = 0
    bsz, t, d = x.shape
    tc = ctx.shape[1]
    rows_g = t // GRID_W
    m = bsz * t
    cw = rw_w0.shape[-1]
    cl = lru_l.shape[-1]
    nh = cw // RWKV_HEAD
    assert bsz == SUBLANE and cw == cl == d and 2 * nh * bsz // 2 == LANE
    rwkv_cols = 3 * cw + 2 * DECAY_LORA + 2 * AAA_LORA + GATE_LORA
    lora_cols = rwkv_cols - 3 * cw

    mod = jnp.dot(jax.nn.silu(c), w_mod[l], precision=HP) + b_mod[l]
    mod_c = jnp.dot(jax.nn.silu(c_ctx), w_mod[l], precision=HP) + b_mod[l]
    sh1, sc1, gt1, sh2, sc2, gt2 = jnp.split(mod, 6, axis=-1)
    csh1 = jnp.broadcast_to(mod_c[0:d], (bsz, d))
    csc1 = jnp.broadcast_to(mod_c[d:2 * d], (bsz, d))
    g_pre1, g_post1, g_pre2, g_post2 = [g[None, :] for g in norm_g[l]]

    wi = w_in[l]
    w_al = jnp.concatenate([wi[:, :rwkv_cols], jnp.zeros((d, cw - lora_cols), F32),
                            wi[:, rwkv_cols:]], axis=1).astype(BF16)
    zr_w = 4 * cw
    mu_al = jnp.concatenate([shift_mu[l], jnp.zeros((2, cw - lora_cols), F32)], axis=1)

    x_tm = jnp.transpose(x, (1, 0, 2))
    ctx_tm = jnp.transpose(ctx, (1, 0, 2))
    zc = _inproj(ctx_tm, 1.0 + csc1, csh1, g_pre1, w_al[:, :zr_w + cl])
    z = _inproj(x_tm, 1.0 + sc1, sh1, g_pre1, w_al)

    blk = cl // LRU_BLOCKS
    per = MXU_N // blk
    gw = lru_gate_w[l].reshape(2, 2, LRU_BLOCKS // per, per, blk, blk)
    eye = jnp.eye(per, dtype=F32)
    gw_bd = jnp.einsum('dgqpjk,pr->dgqpjrk', gw, eye).reshape(2, 2, LRU_BLOCKS // per, MXU_N, MXU_N)
    gw_bd = gw_bd.astype(BF16)
    gb = lru_gate_b[l].reshape(2, 2, cl)
    sp_l = jax.nn.softplus(-lru_l[l])[:, None, :]
    lru_args = (lru_conv_w[l], lru_conv_b[l][None, :], gw_bd, gb, sp_l)
    _, hfin = _lru(zc.reshape(tc, 1, bsz, -1), 4, *lru_args, jnp.zeros((2, bsz, cl), F32))
    h_lru, _ = _lru(z.reshape(rows_g, GRID_W, bsz, -1), 4, *lru_args, hfin)
    h_lru = h_lru.reshape(2, m, cl)

    zeros_c = jnp.zeros((DECAY_LORA, cw), F32)
    wup2 = jnp.concatenate([jnp.concatenate([rw_w_up[l, 0], zeros_c], 1),
                            jnp.concatenate([zeros_c, rw_w_up[l, 1]], 1)], axis=0)
    aup2 = jnp.concatenate([jnp.concatenate([rw_a_up[l, 0], zeros_c], 1),
                            jnp.concatenate([zeros_c, rw_a_up[l, 1]], 1)], axis=0)
    gup = jnp.concatenate([rw_g_up[l], jnp.zeros((MXU_N - GATE_LORA, cw), F32)], axis=0)
    hid = jnp.arange(MXU_N) // RWKV_HEAD
    ones_blk = (hid[:, None] == hid[None, :]).astype(BF16)
    prep_args = (mu_al, rw_w0[l], wup2.astype(BF16), rw_a0[l], aup2.astype(BF16),
                 rw_k_k[l][None, :], rw_k_a[l][None, :], rw_r_k[l].reshape(1, cw),
                 gup.astype(BF16), ones_blk)
    pc = _rwkv_prep(zc, tc, zr_w, *prep_args)
    pz = _rwkv_prep(z, GRID_W, zr_w, *prep_args)
    s_zero = jnp.zeros((RWKV_HEAD, RWKV_HEAD, LANE), F32)
    ys = []
    for rev in (False, True):
        _, s_c = _rwkv_scan(rev, pc[0], pc[3], pc[4], pc[1], pc[2], pc[5], s_zero)
        y_d, _ = _rwkv_scan(rev, pz[0], pz[3], pz[4], pz[1], pz[2], pz[5], s_c)
        ys.append(y_d)

    h, u2, u2b = _merge(ys[0], ys[1], pz[6], pz[7], h_lru, z, x_tm, rw_lnx[l],
                        w_branch_a[l].astype(BF16), w_branch_b[l].astype(BF16), w_out[l].astype(BF16),
                        gt1, g_post1, g_pre2, 1.0 + sc2, sh2)

    n_e = router_w.shape[-1]
    gate6, idx6, rank6, cnt = _router(u2, router_w[l].T, router_b[l][:, None])
    n_assign = m * TOP_K
    items, start = _expert_items(cnt[:, 0].astype(jnp.int32), n_assign)
    flat_e = idx6[:TOP_K].T.reshape(-1)
    sorted_tok = jnp.argsort(flat_e).astype(jnp.int32) // TOP_K
    ys_e = _grouped_experts(u2b[sorted_tok], items, ex_w_gu[l], ex_w_down[l])
    sel = idx6[:TOP_K, :, None] == jnp.arange(n_e, dtype=jnp.int32)[None, None, :]
    pos = jnp.sum(jnp.where(sel, start[None, None, :], 0), axis=-1) + rank6[:TOP_K]
    ye = ys_e[pos.reshape(-1)].reshape(TOP_K, m, d)
    out_tm = _final(ye, gate6.T, u2b, h, sh_w_gu[l].astype(BF16), sh_w_down[l].astype(BF16),
                    gt2, g_post2)
    return jnp.transpose(out_tm, (1, 0, 2))
```

```python
import functools

import jax
import jax.numpy as jnp
from jax import lax
from jax.experimental import pallas as pl
from jax.experimental.pallas import tpu as pltpu

GRID_W = 64
NORM_EPS = 1e-6
RWKV_HEAD = 64
DECAY_LORA = 64
AAA_LORA = 64
GATE_LORA = 160
LNX_EPS = 64e-5
LRU_BLOCKS = 16
CONV_LEFT = 2
LRU_C = 8.0
N_GROUPS = 8
TOPK_GROUPS = 4
TOP_K = 6
ROUTED_SCALE = 2.5
DECAY_SCALE = 0.6065306597126334

LANE = 128
SUBLANE = 8
MXU_N = 256
MOE_ROWS = 512
KCH = 32
VMEM_LIMIT = 48 * 1024 * 1024
HP = lax.Precision.HIGHEST
F32 = jnp.float32
BF16 = jnp.bfloat16


def _cparams(*sem):
    return pltpu.CompilerParams(dimension_semantics=sem, vmem_limit_bytes=VMEM_LIMIT)


def _inproj_kernel(x_ref, scp_ref, sh_ref, g_ref, w_ref, o_ref, xm_ref):
    @pl.when(pl.program_id(1) == 0)
    def _():
        x = x_ref[...]
        ms = jnp.mean(x * x, axis=-1, keepdims=True)
        xm = x * lax.rsqrt(ms + NORM_EPS) * g_ref[...] * scp_ref[...] + sh_ref[...]
        xm_ref[...] = xm.reshape(xm_ref.shape).astype(BF16)

    o_ref[...] = jnp.dot(xm_ref[...], w_ref[...], preferred_element_type=F32)


def _inproj(x_tm, scp, sh, g, w, tt=128, tn=1024):
    t, bsz, d = x_tm.shape
    n = w.shape[1]
    tt = min(tt, t)
    assert t % tt == 0 and n % tn == 0
    return pl.pallas_call(
        _inproj_kernel,
        grid=(t // tt, n // tn),
        in_specs=[pl.BlockSpec((tt, bsz, d), lambda i, j: (i, 0, 0)),
                  pl.BlockSpec((bsz, d), lambda i, j: (0, 0)),
                  pl.BlockSpec((bsz, d), lambda i, j: (0, 0)),
                  pl.BlockSpec((1, d), lambda i, j: (0, 0)),
                  pl.BlockSpec((d, tn), lambda i, j: (0, j))],
        out_specs=pl.BlockSpec((tt * bsz, tn), lambda i, j: (i, j)),
        out_shape=jax.ShapeDtypeStruct((t * bsz, n), F32),
        scratch_shapes=[pltpu.VMEM((tt * bsz, d), BF16)],
        compiler_params=_cparams("parallel", "arbitrary"),
        name="in_proj",
    )(x_tm, scp, sh, g, w)


def _lru_kernel(z_ref, cw_ref, cb_ref, gw_ref, gb_ref, sp_ref, h0_ref, h_ref, hf_ref,
                a_ref, bx_ref):
    d = pl.program_id(0)
    w = pl.program_id(2)
    nr, _, bsz, cq = z_ref.shape
    taps = cw_ref.shape[0]

    @pl.when(w == 0)
    def _():
        hf_ref[0] = h0_ref[0]

    zx = z_ref[:, 0]
    xc = jnp.zeros_like(zx) + cb_ref[...]
    for j in range(taps):
        off = j - CONV_LEFT
        if off < 0:
            sh = jnp.concatenate([jnp.zeros((-off, bsz, cq), F32), zx[:nr + off]], axis=0)
        elif off > 0:
            sh = jnp.concatenate([zx[off:], jnp.zeros((off, bsz, cq), F32)], axis=0)
        else:
            sh = zx
        xc = xc + cw_ref[j:j + 1, :] * sh
    x2 = xc.reshape(nr * bsz, cq)
    xb = x2.astype(BF16)
    pre = []
    for g in range(2):
        cols = []
        for q in range(cq // MXU_N):
            cols.append(jnp.dot(xb[:, q * MXU_N:(q + 1) * MXU_N], gw_ref[0, g, q],
                                preferred_element_type=F32))
        pre.append(jnp.concatenate(cols, axis=1) + gb_ref[0, g:g + 1, :])
    rec = 0.5 * jnp.tanh(0.5 * pre[0]) + 0.5
    inp = 0.5 * jnp.tanh(0.5 * pre[1]) + 0.5
    log_a = -LRU_C * rec * sp_ref[0]
    a = jnp.exp(log_a)
    bx = jnp.sqrt(-jnp.tanh(log_a) * (a * a + 1.0)) * inp * x2
    a_ref[...] = a.reshape(nr, bsz, cq)
    bx_ref[...] = bx.reshape(nr, bsz, cq)

    def step(i, h):
        r = i + d * (nr - 1 - 2 * i)
        h = a_ref[r] * h + bx_ref[r]
        h_ref[0, r, 0] = h
        return h

    hf_ref[0] = lax.fori_loop(0, nr, step, hf_ref[0])


def _lru(z4, col_blk, conv_w, conv_b, gate_w, gate_b, sp_l, h0, cq=512):
    nr, nw, bsz, _ = z4.shape
    c = conv_w.shape[1]
    nq = c // cq
    cb0 = col_blk * (c // cq)

    def zmap(d, q, w):
        return (0, w + d * (nw - 1 - 2 * w), 0, cb0 + q)

    def hmap(d, q, w):
        return (d, 0, w + d * (nw - 1 - 2 * w), 0, q)

    return pl.pallas_call(
        _lru_kernel,
        grid=(2, nq, nw),
        in_specs=[pl.BlockSpec((nr, 1, bsz, cq), zmap),
                  pl.BlockSpec((conv_w.shape[0], cq), lambda d, q, w: (0, q)),
                  pl.BlockSpec((1, cq), lambda d, q, w: (0, q)),
                  pl.BlockSpec((1, 2, cq // MXU_N, MXU_N, MXU_N), lambda d, q, w: (d, 0, q, 0, 0)),
                  pl.BlockSpec((1, 2, cq), lambda d, q, w: (d, 0, q)),
                  pl.BlockSpec((1, 1, cq), lambda d, q, w: (d, 0, q)),
                  pl.BlockSpec((1, bsz, cq), lambda d, q, w: (d, 0, q))],
        out_specs=[pl.BlockSpec((1, nr, 1, bsz, cq), hmap),
                   pl.BlockSpec((1, bsz, cq), lambda d, q, w: (d, 0, q))],
        out_shape=[jax.ShapeDtypeStruct((2, nr, nw, bsz, c), F32),
                   jax.ShapeDtypeStruct((2, bsz, c), F32)],
        scratch_shapes=[pltpu.VMEM((nr, bsz, cq), F32), pltpu.VMEM((nr, bsz, cq), F32)],
        compiler_params=_cparams("arbitrary", "arbitrary", "arbitrary"),
        name="lru_fused",
    )(z4, conv_w, conv_b, gate_w, gate_b, sp_l, h0)


def _to_pairs(q, out_ref, lead=()):
    rows, c = q.shape
    npairs = rows // (2 * SUBLANE)
    lane = lax.broadcasted_iota(jnp.int32, (SUBLANE, LANE), 1)
    low = lane < RWKV_HEAD
    for p in range(npairs):
        ev = q[(2 * p) * SUBLANE:(2 * p + 1) * SUBLANE]
        od = q[(2 * p + 1) * SUBLANE:(2 * p + 2) * SUBLANE]
        pieces = []
        for j in range(c // LANE):
            e_j = ev[:, j * LANE:(j + 1) * LANE]
            o_j = od[:, j * LANE:(j + 1) * LANE]
            pieces.append(jnp.where(low, e_j, pltpu.roll(o_j, RWKV_HEAD, 1)))
            pieces.append(jnp.where(low, pltpu.roll(e_j, RWKV_HEAD, 1), o_j))
        x = jnp.concatenate(pieces, axis=0)
        out_ref[lead + (p,)] = x.T


def _group_sum(x, ones_ref):
    hi = x.astype(BF16)
    lo = (x - hi.astype(F32)).astype(BF16)
    cols = []
    for q in range(x.shape[1] // MXU_N):
        sl = slice(q * MXU_N, (q + 1) * MXU_N)
        cols.append(jnp.dot(hi[:, sl], ones_ref[...], preferred_element_type=F32)
                    + jnp.dot(lo[:, sl], ones_ref[...], preferred_element_type=F32))
    return jnp.concatenate(cols, axis=1)


def _prep_kernel(period, z_ref, zp_ref, zn_ref, mu_ref, w0_ref, wup_ref, a0_ref, aup_ref,
                 kk_ref, ka_ref, rk_ref, gup_ref, ones_ref,
                 r_o, v_o, kn_o, w_o, kd_o, b_o, bv_o, g_o):
    i = pl.program_id(0)
    rows, _ = z_ref.shape
    tt = rows // SUBLANE
    cw = w0_ref.shape[1]
    z = z_ref[...]
    has_prev = ((i * tt) % period != 0).astype(F32)
    has_next = (((i + 1) * tt) % period != 0).astype(F32)
    zprev = jnp.concatenate([zp_ref[...] * has_prev, z[:rows - SUBLANE]], axis=0)
    znext = jnp.concatenate([z[SUBLANE:], zn_ref[...] * has_next], axis=0)
    zs = z + mu_ref[0:1, :] * (zprev - z) + mu_ref[1:2, :] * (znext - z)
    r = zs[:, 0:cw]
    k = zs[:, cw:2 * cw]
    v = zs[:, 2 * cw:3 * cw]
    off = 3 * cw
    wd = jnp.tanh(zs[:, off:off + LANE])
    ad = zs[:, off + LANE:off + 2 * LANE]
    gd = jax.nn.sigmoid(zs[:, off + 2 * LANE:off + 2 * LANE + MXU_N])
    wl = jnp.dot(wd.astype(BF16), wup_ref[...], preferred_element_type=F32)
    al = jnp.dot(ad.astype(BF16), aup_ref[...], preferred_element_type=F32)
    g_o[...] = jnp.dot(gd.astype(BF16), gup_ref[...], preferred_element_type=F32)
    kk = k * kk_ref[...]
    nrm = jnp.sqrt(_group_sum(kk * kk, ones_ref))
    kn = kk / jnp.maximum(nrm, 1e-12)
    _to_pairs(r, r_o)
    _to_pairs(v, v_o)
    _to_pairs(kn, kn_o)
    ksum = jnp.zeros_like(k)
    for d in range(2):
        xw = w0_ref[d:d + 1, :] + wl[:, d * cw:(d + 1) * cw]
        decay = jnp.exp(-DECAY_SCALE * jax.nn.sigmoid(xw))
        a = jax.nn.sigmoid(a0_ref[d:d + 1, :] + al[:, d * cw:(d + 1) * cw])
        kd = k * (1.0 + (a - 1.0) * ka_ref[...])
        ksum = ksum + kd
        _to_pairs(decay, w_o, (d,))
        _to_pairs(kd, kd_o, (d,))
        _to_pairs(kn * a, b_o, (d,))
    bonus = _group_sum(r * ksum * rk_ref[...], ones_ref)
    bv_o[...] = bonus * v


def _rwkv_prep(z, period, zcols, mu, w0, wup, a0, aup, k_k, k_a, r_k, gup, ones, tt=32):
    rows_total = z.shape[0]
    t = rows_total // SUBLANE
    cw = w0.shape[1]
    tt = min(tt, period)
    assert t % tt == 0 and period % tt == 0 and tt % 2 == 0
    rows = tt * SUBLANE
    nblk = t // tt
    const = lambda shape: pl.BlockSpec(shape, lambda i: (0,) * len(shape))
    pair = pl.BlockSpec((tt // 2, LANE, LANE), lambda i: (i, 0, 0))
    pair_d = pl.BlockSpec((2, tt // 2, LANE, LANE), lambda i: (0, i, 0, 0))
    nat = pl.BlockSpec((rows, cw), lambda i: (i, 0))
    sh_pair = jax.ShapeDtypeStruct((t // 2, LANE, LANE), F32)
    sh_pair_d = jax.ShapeDtypeStruct((2, t // 2, LANE, LANE), F32)
    sh_nat = jax.ShapeDtypeStruct((rows_total, cw), F32)
    return pl.pallas_call(
        functools.partial(_prep_kernel, period),
        grid=(nblk,),
        in_specs=[pl.BlockSpec((rows, zcols), lambda i: (i, 0)),
                  pl.BlockSpec((SUBLANE, zcols), lambda i: (jnp.maximum(i * tt - 1, 0), 0)),
                  pl.BlockSpec((SUBLANE, zcols), lambda i: (jnp.minimum((i + 1) * tt, t - 1), 0)),
                  const(mu.shape), const(w0.shape), const(wup.shape), const(a0.shape),
                  const(aup.shape), const(k_k.shape), const(k_a.shape), const(r_k.shape),
                  const(gup.shape), const(ones.shape)],
        out_specs=[pair, pair, pair, pair_d, pair_d, pair_d, nat, nat],
        out_shape=[sh_pair, sh_pair, sh_pair, sh_pair_d, sh_pair_d, sh_pair_d, sh_nat, sh_nat],
        compiler_params=_cparams("parallel"),
        name="rwkv_prep",
    )(z, z, z, mu, w0, wup, a0, aup, k_k, k_a, r_k, gup, ones)


def _scan_kernel(reverse, r_ref, w_ref, k_ref, v_ref, kk_ref, b_ref, s0_ref, y_ref, s_ref):
    n = RWKV_HEAD
    npairs = r_ref.shape[0]

    @pl.when(pl.program_id(0) == 0)
    def _():
        s_ref[...] = s0_ref[...]

    def removal(p, base):
        def body(kb, acc):
            k0 = pl.multiple_of(kb * KCH, KCH)
            for j in range(KCH):
                acc = acc + s_ref[k0 + j] * kk_ref[p, pl.ds(base + k0 + j, 1), :]
            return acc

        return lax.fori_loop(0, n // KCH, body, jnp.zeros((n, LANE), F32))

    def one_step(p, base, pn, basen, acc):
        sa = -acc
        vv = v_ref[p, pl.ds(base, n), :]

        def body(kb, carry):
            y, nacc = carry
            k0 = pl.multiple_of(kb * KCH, KCH)
            for j in range(KCH):
                row = pl.ds(base + k0 + j, 1)
                sk = (s_ref[k0 + j] * w_ref[0, p, row, :]
                      + (sa * b_ref[0, p, row, :] + vv * k_ref[0, p, row, :]))
                s_ref[k0 + j] = sk
                y = y + sk * r_ref[p, row, :]
                nacc = nacc + sk * kk_ref[pn, pl.ds(basen + k0 + j, 1), :]
            return y, nacc

        zero = jnp.zeros((n, LANE), F32)
        y, nacc = lax.fori_loop(0, n // KCH, body, (zero, zero))
        y_ref[p, pl.ds(base, n), :] = y
        return nacc

    first, second = ((n, 0) if reverse else (0, n))
    p_first = (npairs - 1) if reverse else 0

    def pair_step(i, acc):
        p = (npairs - 1 - i) if reverse else i
        pn = jnp.clip(p + (-1 if reverse else 1), 0, npairs - 1)
        acc = one_step(p, first, p, second, acc)
        return one_step(p, second, pn, first, acc)

    lax.fori_loop(0, npairs, pair_step, removal(p_first, first))


def _rwkv_scan(reverse, r, w, k, v, kk, b, s0, tbp=16):
    np_total = r.shape[0]
    tbp = min(tbp, np_total)
    assert np_total % tbp == 0
    nblk = np_total // tbp
    d = 1 if reverse else 0
    imap = (lambda i: (nblk - 1 - i, 0, 0)) if reverse else (lambda i: (i, 0, 0))
    imap_d = (lambda i: (d, nblk - 1 - i, 0, 0)) if reverse else (lambda i: (d, i, 0, 0))
    sp = pl.BlockSpec((tbp, LANE, LANE), imap)
    sp_d = pl.BlockSpec((1, tbp, LANE, LANE), imap_d)
    st = pl.BlockSpec((RWKV_HEAD, RWKV_HEAD, LANE), lambda i: (0, 0, 0))
    return pl.pallas_call(
        functools.partial(_scan_kernel, reverse),
        grid=(nblk,),
        in_specs=[sp, sp_d, sp_d, sp, sp, sp_d, st],
        out_specs=[sp, st],
        out_shape=[jax.ShapeDtypeStruct(r.shape, F32),
                   jax.ShapeDtypeStruct((RWKV_HEAD, RWKV_HEAD, LANE), F32)],
        compiler_params=_cparams("arbitrary"),
        name="rwkv_scan",
    )(r, w, k, v, kk, b, s0)


def _from_pairs(y_ref, nat_ref):
    npairs = y_ref.shape[0]
    lane = lax.broadcasted_iota(jnp.int32, (SUBLANE, LANE), 1)
    low = lane < RWKV_HEAD
    for p in range(npairs):
        xt = y_ref[p].T
        for j in range(nat_ref.shape[1] // LANE):
            h0 = xt[(2 * j) * SUBLANE:(2 * j + 1) * SUBLANE]
            h1 = xt[(2 * j + 1) * SUBLANE:(2 * j + 2) * SUBLANE]
            ev = jnp.where(low, h0, pltpu.roll(h1, RWKV_HEAD, 1))
            od = jnp.where(low, pltpu.roll(h0, RWKV_HEAD, 1), h1)
            nat_ref[(2 * p) * SUBLANE:(2 * p + 1) * SUBLANE, j * LANE:(j + 1) * LANE] = ev
            nat_ref[(2 * p + 1) * SUBLANE:(2 * p + 2) * SUBLANE, j * LANE:(j + 1) * LANE] = od


def _merge_kernel(y0_ref, y1_ref, bv_ref, go_ref, h0_ref, h1_ref, zg_ref, zgate_ref, x_ref,
                  lnx_ref, wa_ref, wb_ref, wo_ref, gt1_ref, gpost_ref, gpre_ref, scp_ref, sh_ref,
                  h_o, u_o, ub_o, yn_ref, nat_ref):
    n = RWKV_HEAD
    d = wa_ref.shape[1]
    npairs = y0_ref.shape[0]
    tt, bsz, _ = x_ref.shape
    for p in range(npairs):
        y = y0_ref[p] + y1_ref[p]
        for t2 in range(2):
            yh = y[t2 * n:(t2 + 1) * n]
            mu = jnp.mean(yh, axis=0, keepdims=True)
            var = jnp.mean(jnp.square(yh - mu), axis=0, keepdims=True)
            yn_ref[p, t2 * n:(t2 + 1) * n, :] = (yh - mu) * lax.rsqrt(var + LNX_EPS)
    _from_pairs(yn_ref, nat_ref)
    y_a = (nat_ref[...] * lnx_ref[0:1, :] + lnx_ref[1:2, :] + bv_ref[...]) * go_ref[...]
    pa = jnp.dot(y_a.astype(BF16), wa_ref[...], preferred_element_type=F32)
    y_b = jax.nn.gelu(zg_ref[...]) * (h0_ref[0] + h1_ref[0])
    pb = jnp.dot(y_b.astype(BF16), wb_ref[...], preferred_element_type=F32)
    gates = jax.nn.sigmoid(zgate_ref[...])
    mixed = gates[:, :d] * pa + gates[:, d:] * pb
    yo = jnp.dot(mixed.astype(BF16), wo_ref[...], preferred_element_type=F32).reshape(tt, bsz, d)
    yo = yo * lax.rsqrt(jnp.mean(yo * yo, axis=-1, keepdims=True) + NORM_EPS) * gpost_ref[...]
    h = x_ref[...] + gt1_ref[...] * yo
    h_o[...] = h
    u = h * lax.rsqrt(jnp.mean(h * h, axis=-1, keepdims=True) + NORM_EPS) * gpre_ref[...]
    u = (u * scp_ref[...] + sh_ref[...]).reshape(tt * bsz, d)
    u_o[...] = u
    ub_o[...] = u.astype(BF16)


def _merge(y0, y1, bv, g_out, h_lru, z, x_tm, lnx, wa, wb, wo, gt1, g_post, g_pre, scp, sh, tt=32):
    t, bsz, d = x_tm.shape
    rows = tt * bsz
    c = wa.shape[0]
    zcol = lambda blk, width: pl.BlockSpec((rows, width), lambda i: (i, blk))
    const = lambda shape: pl.BlockSpec(shape, lambda i: (0,) * len(shape))
    pair = pl.BlockSpec((tt // 2, LANE, LANE), lambda i: (i, 0, 0))
    nat = pl.BlockSpec((rows, c), lambda i: (i, 0))
    lg_blk = (3 * c + c + c) // c
    gate_blk = (lg_blk + 1) * c // (2 * d)
    return pl.pallas_call(
        _merge_kernel,
        grid=(t // tt,),
        in_specs=[pair, pair, nat, nat,
                  pl.BlockSpec((1, rows, c), lambda i: (0, i, 0)),
                  pl.BlockSpec((1, rows, c), lambda i: (1, i, 0)),
                  zcol(lg_blk, c), zcol(gate_blk, 2 * d),
                  pl.BlockSpec((tt, bsz, d), lambda i: (i, 0, 0)),
                  const(lnx.shape), const(wa.shape), const(wb.shape), const(wo.shape),
                  const(gt1.shape), const(g_post.shape), const(g_pre.shape),
                  const(scp.shape), const(sh.shape)],
        out_specs=[pl.BlockSpec((tt, bsz, d), lambda i: (i, 0, 0)),
                   pl.BlockSpec((rows, d), lambda i: (i, 0)),
                   pl.BlockSpec((rows, d), lambda i: (i, 0))],
        out_shape=[jax.ShapeDtypeStruct((t, bsz, d), F32),
                   jax.ShapeDtypeStruct((t * bsz, d), F32),
                   jax.ShapeDtypeStruct((t * bsz, d), BF16)],
        scratch_shapes=[pltpu.VMEM((tt // 2, LANE, LANE), F32), pltpu.VMEM((rows, c), F32)],
        compiler_params=_cparams("parallel"),
        name="merge",
    )(y0, y1, bv, g_out, h_lru, h_lru, z, z, x_tm, lnx, wa, wb, wo, gt1, g_post, g_pre, scp, sh)


def _experts_kernel(blk_ref, eid_ref, lo_ref, hi_ref, x_ref, wgu_ref, wdn_ref, o_ref, wgu_b, wdn_b):
    i = pl.program_id(0)
    prev = jnp.maximum(i - 1, 0)
    ff = wdn_ref.shape[1]

    @pl.when(jnp.logical_or(i == 0, eid_ref[i] != eid_ref[prev]))
    def _():
        wgu_b[...] = wgu_ref[0].astype(BF16)
        wdn_b[...] = wdn_ref[0].astype(BF16)

    gu = jnp.dot(x_ref[...], wgu_b[...], preferred_element_type=F32)
    g = gu[:, :ff]
    u = gu[:, ff:]
    act = (g * jax.nn.sigmoid(g) * u).astype(BF16)
    res = jnp.dot(act, wdn_b[...], preferred_element_type=F32)
    row = lax.broadcasted_iota(jnp.int32, (res.shape[0], 1), 0)
    mine = jnp.logical_and(row >= lo_ref[i], row < hi_ref[i])
    first = jnp.logical_or(i == 0, blk_ref[i] != blk_ref[prev])

    @pl.when(first)
    def _():
        o_ref[...] = jnp.where(mine, res, 0.0).astype(o_ref.dtype)

    @pl.when(jnp.logical_not(first))
    def _():
        o_ref[...] = jnp.where(mine, res, o_ref[...].astype(F32)).astype(o_ref.dtype)


def _grouped_experts(xs, items, w_gu, w_down):
    n_rows, d = xs.shape
    _, _, ff2 = w_gu.shape
    ff = ff2 // 2
    n_items = items[0].shape[0]
    grid_spec = pltpu.PrefetchScalarGridSpec(
        num_scalar_prefetch=4,
        grid=(n_items,),
        in_specs=[pl.BlockSpec((MOE_ROWS, d), lambda i, blk, eid, lo, hi: (blk[i], 0)),
                  pl.BlockSpec((1, d, ff2), lambda i, blk, eid, lo, hi: (eid[i], 0, 0)),
                  pl.BlockSpec((1, ff, d), lambda i, blk, eid, lo, hi: (eid[i], 0, 0))],
        out_specs=pl.BlockSpec((MOE_ROWS, d), lambda i, blk, eid, lo, hi: (blk[i], 0)),
        scratch_shapes=[pltpu.VMEM((d, ff2), BF16), pltpu.VMEM((ff, d), BF16)],
    )
    return pl.pallas_call(
        _experts_kernel,
        grid_spec=grid_spec,
        out_shape=jax.ShapeDtypeStruct((n_rows, d), BF16),
        compiler_params=_cparams("arbitrary"),
        name="moe_experts",
    )(*items, xs, w_gu, w_down)


def _expert_items(counts, n_rows):
    n_e = counts.shape[0]
    n_blocks = n_rows // MOE_ROWS
    n_items = n_blocks + n_e
    end = jnp.cumsum(counts)
    start = end - counts
    first_blk = start // MOE_ROWS
    per_e = jnp.where(counts > 0, (end - 1) // MOE_ROWS - first_blk + 1, 0)
    item_end = jnp.cumsum(per_e)
    w = jnp.arange(n_items, dtype=jnp.int32)
    eid = jnp.minimum(jnp.sum((item_end[None, :] <= w[:, None]).astype(jnp.int32), axis=1), n_e - 1)
    onehot = (eid[:, None] == jnp.arange(n_e)[None, :]).astype(jnp.int32)
    pick = lambda v: jnp.sum(onehot * v[None, :], axis=1)
    valid = w < item_end[-1]
    blk = jnp.where(valid, pick(first_blk) + w - (pick(item_end) - pick(per_e)), n_blocks - 1)
    lo = jnp.clip(pick(start) - blk * MOE_ROWS, 0, MOE_ROWS)
    hi = jnp.where(valid, jnp.clip(pick(end) - blk * MOE_ROWS, 0, MOE_ROWS), lo)
    i32 = lambda v: v.astype(jnp.int32)
    return (i32(blk), i32(eid), i32(lo), i32(hi)), i32(start)


def _argmax_pick(cur, iota, big, axis):
    mx = jnp.max(cur, axis=axis, keepdims=True)
    ix = jnp.min(jnp.where(cur == mx, iota, big), axis=axis, keepdims=True)
    return iota == ix


def _router_kernel(u_ref, wt_ref, bias_ref, tri_ref, gate_o, idx_o, rank_o, cnt_o):
    n_e = wt_ref.shape[0]
    tr = u_ref.shape[0]
    per_g = n_e // N_GROUPS
    neg = -jnp.inf

    @pl.when(pl.program_id(0) == 0)
    def _():
        cnt_o[...] = jnp.zeros_like(cnt_o)

    logits = lax.dot_general(wt_ref[...], u_ref[...], (((1,), (1,)), ((), ())),
                             precision=HP, preferred_element_type=F32)
    scores = jax.nn.sigmoid(logits)
    biased = scores + bias_ref[...]
    b3 = biased.reshape(N_GROUPS, per_g, tr)
    ie = lax.broadcasted_iota(jnp.int32, b3.shape, 1)
    p1 = _argmax_pick(b3, ie, per_g, 1)
    m1 = jnp.max(b3, axis=1)
    m2 = jnp.max(jnp.where(p1, neg, b3), axis=1)
    cur = m1 + m2
    ig = lax.broadcasted_iota(jnp.int32, cur.shape, 0)
    sel_g = jnp.zeros(cur.shape, F32)
    for _ in range(TOPK_GROUPS):
        pick = _argmax_pick(cur, ig, N_GROUPS, 0)
        sel_g = jnp.where(pick, 1.0, sel_g)
        cur = jnp.where(pick, neg, cur)
    cur = jnp.where(sel_g[:, None, :] > 0.5, b3, neg).reshape(n_e, tr)
    i_e = lax.broadcasted_iota(jnp.int32, cur.shape, 0)
    i_ef = i_e.astype(F32)
    picks, top_s = [], []
    sel_f = jnp.zeros(cur.shape, F32)
    for _ in range(TOP_K):
        pick = _argmax_pick(cur, i_e, n_e, 0)
        picks.append(pick)
        top_s.append(jnp.sum(jnp.where(pick, scores, 0.0), axis=0, keepdims=True))
        sel_f = jnp.where(pick, 1.0, sel_f)
        cur = jnp.where(pick, neg, cur)
    denom = top_s[0]
    for s in top_s[1:]:
        denom = denom + s
    rank = jnp.dot(sel_f.astype(BF16), tri_ref[...], preferred_element_type=F32) + cnt_o[:, 0:1]
    cnt_o[...] = cnt_o[...] + jnp.sum(sel_f, axis=1, keepdims=True)
    zrow = jnp.zeros((SUBLANE - TOP_K, tr), F32)
    gate_o[...] = jnp.concatenate([s / denom * ROUTED_SCALE for s in top_s] + [zrow], axis=0)
    idx_rows = [jnp.sum(jnp.where(p, i_ef, 0.0), axis=0, keepdims=True) for p in picks]
    idx_o[...] = jnp.concatenate(idx_rows + [zrow], axis=0).astype(jnp.int32)
    rank_rows = [jnp.sum(jnp.where(p, rank, 0.0), axis=0, keepdims=True) for p in picks]
    rank_o[...] = jnp.concatenate(rank_rows + [zrow], axis=0).astype(jnp.int32)


def _router(u, w_t, bias, tr=256):
    m, d = u.shape
    n_e = w_t.shape[0]
    tri = (jnp.arange(tr)[:, None] < jnp.arange(tr)[None, :]).astype(BF16)
    tok = pl.BlockSpec((SUBLANE, tr), lambda i: (0, i))
    return pl.pallas_call(
        _router_kernel,
        grid=(m // tr,),
        in_specs=[pl.BlockSpec((tr, d), lambda i: (i, 0)),
                  pl.BlockSpec((n_e, d), lambda i: (0, 0)),
                  pl.BlockSpec((n_e, 1), lambda i: (0, 0)),
                  pl.BlockSpec((tr, tr), lambda i: (0, 0))],
        out_specs=[tok, tok, tok, pl.BlockSpec((n_e, LANE), lambda i: (0, 0))],
        out_shape=[jax.ShapeDtypeStruct((SUBLANE, m), F32),
                   jax.ShapeDtypeStruct((SUBLANE, m), jnp.int32),
                   jax.ShapeDtypeStruct((SUBLANE, m), jnp.int32),
                   jax.ShapeDtypeStruct((n_e, LANE), F32)],
        compiler_params=_cparams("arbitrary"),
        name="router",
    )(u, w_t, bias, tri)


def _final_kernel(*refs):
    ye_refs = refs[:TOP_K]
    gate_ref, ub_ref, h_ref, wgu_ref, wdn_ref, gt_ref, g_ref, o_ref = refs[TOP_K:]
    tt, bsz, d = h_ref.shape
    ff = wdn_ref.shape[0]
    routed = jnp.zeros((tt * bsz, d), F32)
    for j in range(TOP_K):
        routed = routed + gate_ref[:, j:j + 1] * ye_refs[j][0].astype(F32)
    gu = jnp.dot(ub_ref[...], wgu_ref[...], preferred_element_type=F32)
    g = gu[:, :ff]
    act = (g * jax.nn.sigmoid(g) * gu[:, ff:]).astype(BF16)
    moe = (routed + jnp.dot(act, wdn_ref[...], preferred_element_type=F32)).reshape(tt, bsz, d)
    moe = moe * lax.rsqrt(jnp.mean(moe * moe, axis=-1, keepdims=True) + NORM_EPS) * g_ref[...]
    o_ref[...] = h_ref[...] + gt_ref[...] * moe


def _final(ye, gate, ub, h, wgu, wdn, gt, g, tt=32):
    t, bsz, d = h.shape
    rows = tt * bsz
    const = lambda shape: pl.BlockSpec(shape, lambda i: (0,) * len(shape))
    return pl.pallas_call(
        _final_kernel,
        grid=(t // tt,),
        in_specs=[pl.BlockSpec((1, rows, d), functools.partial(lambda j, i: (j, i, 0), j))
                  for j in range(TOP_K)] + [
                  pl.BlockSpec((rows, SUBLANE), lambda i: (i, 0)),
                  pl.BlockSpec((rows, d), lambda i: (i, 0)),
                  pl.BlockSpec((tt, bsz, d), lambda i: (i, 0, 0)),
                  const(wgu.shape), const(wdn.shape), const(gt.shape), const(g.shape)],
        out_specs=pl.BlockSpec((tt, bsz, d), lambda i: (i, 0, 0)),
        out_shape=jax.ShapeDtypeStruct((t, bsz, d), F32),
        compiler_params=_cparams("parallel"),
        name="moe_final",
    )(*([ye] * TOP_K), gate, ub, h, wgu, wdn, gt, g)


def kernel(x, c, ctx, c_ctx, w_mod, b_mod, norm_g, w_in, shift_mu, rw_w0, rw_w_up, rw_a0, rw_a_up,
           rw_g_up, rw_k_k, rw_k_a, rw_r_k, rw_lnx, lru_conv_w, lru_conv_b, lru_gate_w, lru_gate_b,
           lru_l, w_branch_a, w_branch_b, w_out, router_w, router_b, ex_w_gu, ex_w_down,
           sh_w_gu, sh_w_down):
    l = 0
    bsz, t, d = x.shape
    tc = ctx.shape[1]
    rows_g = t // GRID_W
    m = bsz * t
    cw = rw_w0.shape[-1]
    cl = lru_l.shape[-1]
    nh = cw // RWKV_HEAD
    assert bsz == SUBLANE and cw == cl == d and 2 * nh * bsz // 2 == LANE
    rwkv_cols = 3 * cw + 2 * DECAY_LORA + 2 * AAA_LORA + GATE_LORA
    lora_cols = rwkv_cols - 3 * cw

    mod = jnp.dot(jax.nn.silu(c), w_mod[l], precision=HP) + b_mod[l]
    mod_c = jnp.dot(jax.nn.silu(c_ctx), w_mod[l], precision=HP) + b_mod[l]
    sh1, sc1, gt1, sh2, sc2, gt2 = jnp.split(mod, 6, axis=-1)
    csh1 = jnp.broadcast_to(mod_c[0:d], (bsz, d))
    csc1 = jnp.broadcast_to(mod_c[d:2 * d], (bsz, d))
    g_pre1, g_post1, g_pre2, g_post2 = [g[None, :] for g in norm_g[l]]

    wi = w_in[l]
    w_al = jnp.concatenate([wi[:, :rwkv_cols], jnp.zeros((d, cw - lora_cols), F32),
                            wi[:, rwkv_cols:]], axis=1).astype(BF16)
    zr_w = 4 * cw
    mu_al = jnp.concatenate([shift_mu[l], jnp.zeros((2, cw - lora_cols), F32)], axis=1)

    x_tm = jnp.transpose(x, (1, 0, 2))
    ctx_tm = jnp.transpose(ctx, (1, 0, 2))
    zc = _inproj(ctx_tm, 1.0 + csc1, csh1, g_pre1, w_al[:, :zr_w + cl])
    z = _inproj(x_tm, 1.0 + sc1, sh1, g_pre1, w_al)

    blk = cl // LRU_BLOCKS
    per = MXU_N // blk
    gw = lru_gate_w[l].reshape(2, 2, LRU_BLOCKS // per, per, blk, blk)
    eye = jnp.eye(per, dtype=F32)
    gw_bd = jnp.einsum('dgqpjk,pr->dgqpjrk', gw, eye).reshape(2, 2, LRU_BLOCKS // per, MXU_N, MXU_N)
    gw_bd = gw_bd.astype(BF16)
    gb = lru_gate_b[l].reshape(2, 2, cl)
    sp_l = jax.nn.softplus(-lru_l[l])[:, None, :]
    lru_args = (lru_conv_w[l], lru_conv_b[l][None, :], gw_bd, gb, sp_l)
    _, hfin = _lru(zc.reshape(tc, 1, bsz, -1), 4, *lru_args, jnp.zeros((2, bsz, cl), F32))
    h_lru, _ = _lru(z.reshape(rows_g, GRID_W, bsz, -1), 4, *lru_args, hfin)
    h_lru = h_lru.reshape(2, m, cl)

    zeros_c = jnp.zeros((DECAY_LORA, cw), F32)
    wup2 = jnp.concatenate([jnp.concatenate([rw_w_up[l, 0], zeros_c], 1),
                            jnp.concatenate([zeros_c, rw_w_up[l, 1]], 1)], axis=0)
    aup2 = jnp.concatenate([jnp.concatenate([rw_a_up[l, 0], zeros_c], 1),
                            jnp.concatenate([zeros_c, rw_a_up[l, 1]], 1)], axis=0)
    gup = jnp.concatenate([rw_g_up[l], jnp.zeros((MXU_N - GATE_LORA, cw), F32)], axis=0)
    hid = jnp.arange(MXU_N) // RWKV_HEAD
    ones_blk = (hid[:, None] == hid[None, :]).astype(BF16)
    prep_args = (mu_al, rw_w0[l], wup2.astype(BF16), rw_a0[l], aup2.astype(BF16),
                 rw_k_k[l][None, :], rw_k_a[l][None, :], rw_r_k[l].reshape(1, cw),
                 gup.astype(BF16), ones_blk)
    pc = _rwkv_prep(zc, tc, zr_w, *prep_args)
    pz = _rwkv_prep(z, GRID_W, zr_w, *prep_args)
    s_zero = jnp.zeros((RWKV_HEAD, RWKV_HEAD, LANE), F32)
    ys = []
    for rev in (False, True):
        _, s_c = _rwkv_scan(rev, pc[0], pc[3], pc[4], pc[1], pc[2], pc[5], s_zero)
        y_d, _ = _rwkv_scan(rev, pz[0], pz[3], pz[4], pz[1], pz[2], pz[5], s_c)
        ys.append(y_d)

    h, u2, u2b = _merge(ys[0], ys[1], pz[6], pz[7], h_lru, z, x_tm, rw_lnx[l],
                        w_branch_a[l].astype(BF16), w_branch_b[l].astype(BF16), w_out[l].astype(BF16),
                        gt1, g_post1, g_pre2, 1.0 + sc2, sh2)

    n_e = router_w.shape[-1]
    gate6, idx6, rank6, cnt = _router(u2, router_w[l].T, router_b[l][:, None])
    n_assign = m * TOP_K
    items, start = _expert_items(cnt[:, 0].astype(jnp.int32), n_assign)
    flat_e = idx6[:TOP_K].T.reshape(-1)
    sorted_tok = jnp.argsort(flat_e).astype(jnp.int32) // TOP_K
    ys_e = _grouped_experts(u2b[sorted_tok], items, ex_w_gu[l], ex_w_down[l])
    sel = idx6[:TOP_K, :, None] == jnp.arange(n_e, dtype=jnp.int32)[None, None, :]
    pos = jnp.sum(jnp.where(sel, start[None, None, :], 0), axis=-1) + rank6[:TOP_K]
    ye = ys_e[pos.reshape(-1)].reshape(TOP_K, m, d)
    out_tm = _final(ye, gate6.T, u2b, h, sh_w_gu[l].astype(BF16), sh_w_down[l].astype(BF16),
                    gt2, g_post2)
    return jnp.transpose(out_tm, (1, 0, 2))
```

```python
import functools

import jax
import jax.numpy as jnp
from jax import lax
from jax.experimental import pallas as pl
from jax.experimental.pallas import tpu as pltpu

GRID_W = 64
NORM_EPS = 1e-6
RWKV_HEAD = 64
DECAY_LORA = 64
AAA_LORA = 64
GATE_LORA = 160
LNX_EPS = 64e-5
LRU_BLOCKS = 16
CONV_LEFT = 2
LRU_C = 8.0
N_GROUPS = 8
TOPK_GROUPS = 4
TOP_K = 6
ROUTED_SCALE = 2.5
DECAY_SCALE = 0.6065306597126334

LANE = 128
SUBLANE = 8
MXU_N = 256
MOE_ROWS = 512
KCH = 32
LRU_BLOCK_BYTES = 4 * 1024 * 1024
VMEM_LIMIT = 48 * 1024 * 1024
HP = lax.Precision.HIGHEST
F32 = jnp.float32
BF16 = jnp.bfloat16


def _cparams(*sem):
    return pltpu.CompilerParams(dimension_semantics=sem, vmem_limit_bytes=VMEM_LIMIT)


def _inproj_kernel(x_ref, scp_ref, sh_ref, g_ref, w_ref, o_ref, xm_ref):
    @pl.when(pl.program_id(1) == 0)
    def _():
        x = x_ref[...]
        ms = jnp.mean(x * x, axis=-1, keepdims=True)
        xm = x * lax.rsqrt(ms + NORM_EPS) * g_ref[...] * scp_ref[...] + sh_ref[...]
        xm_ref[...] = xm.reshape(xm_ref.shape).astype(BF16)

    o_ref[...] = jnp.dot(xm_ref[...], w_ref[...], preferred_element_type=F32)


def _inproj(x_tm, scp, sh, g, w, tt=128, tn=1024):
    t, bsz, d = x_tm.shape
    n = w.shape[1]
    tt = min(tt, t)
    assert t % tt == 0 and n % tn == 0
    return pl.pallas_call(
        _inproj_kernel,
        grid=(t // tt, n // tn),
        in_specs=[pl.BlockSpec((tt, bsz, d), lambda i, j: (i, 0, 0)),
                  pl.BlockSpec((bsz, d), lambda i, j: (0, 0)),
                  pl.BlockSpec((bsz, d), lambda i, j: (0, 0)),
                  pl.BlockSpec((1, d), lambda i, j: (0, 0)),
                  pl.BlockSpec((d, tn), lambda i, j: (0, j))],
        out_specs=pl.BlockSpec((tt * bsz, tn), lambda i, j: (i, j)),
        out_shape=jax.ShapeDtypeStruct((t * bsz, n), F32),
        scratch_shapes=[pltpu.VMEM((tt * bsz, d), BF16)],
        compiler_params=_cparams("parallel", "arbitrary"),
        name="in_proj",
    )(x_tm, scp, sh, g, w)


def _lru_kernel(z_ref, cw_ref, cb_ref, gw_ref, gb_ref, sp_ref, h0_ref, h_ref, hf_ref,
                a_ref, bx_ref):
    d = pl.program_id(0)
    w = pl.program_id(2)
    nr, _, bsz, cq = z_ref.shape
    taps = cw_ref.shape[0]

    @pl.when(w == 0)
    def _():
        hf_ref[0] = h0_ref[0]

    zx = z_ref[:, 0]
    xc = jnp.zeros_like(zx) + cb_ref[...]
    for j in range(taps):
        off = j - CONV_LEFT
        if off < 0:
            sh = jnp.concatenate([jnp.zeros((-off, bsz, cq), F32), zx[:nr + off]], axis=0)
        elif off > 0:
            sh = jnp.concatenate([zx[off:], jnp.zeros((off, bsz, cq), F32)], axis=0)
        else:
            sh = zx
        xc = xc + cw_ref[j:j + 1, :] * sh
    x2 = xc.reshape(nr * bsz, cq)
    xb = x2.astype(BF16)
    pre = []
    for g in range(2):
        cols = []
        for q in range(cq // MXU_N):
            cols.append(jnp.dot(xb[:, q * MXU_N:(q + 1) * MXU_N], gw_ref[0, g, q],
                                preferred_element_type=F32))
        pre.append(jnp.concatenate(cols, axis=1) + gb_ref[0, g:g + 1, :])
    rec = 0.5 * jnp.tanh(0.5 * pre[0]) + 0.5
    inp = 0.5 * jnp.tanh(0.5 * pre[1]) + 0.5
    log_a = -LRU_C * rec * sp_ref[0]
    a = jnp.exp(log_a)
    bx = jnp.sqrt(-jnp.tanh(log_a) * (a * a + 1.0)) * inp * x2
    a_ref[...] = a.reshape(nr, bsz, cq)
    bx_ref[...] = bx.reshape(nr, bsz, cq)

    def step(i, h):
        r = i + d * (nr - 1 - 2 * i)
        h = a_ref[r] * h + bx_ref[r]
        h_ref[0, r, 0] = h
        return h

    hf_ref[0] = lax.fori_loop(0, nr, step, hf_ref[0])


def _lru(z4, col_blk, conv_w, conv_b, gate_w, gate_b, sp_l, h0):
    nr, nw, bsz, _ = z4.shape
    c = conv_w.shape[1]
    cq = min(c, max(MXU_N, LRU_BLOCK_BYTES // (nr * bsz * 4)))
    nq = c // cq
    cb0 = col_blk * (c // cq)

    def zmap(d, q, w):
        return (0, w + d * (nw - 1 - 2 * w), 0, cb0 + q)

    def hmap(d, q, w):
        return (d, 0, w + d * (nw - 1 - 2 * w), 0, q)

    return pl.pallas_call(
        _lru_kernel,
        grid=(2, nq, nw),
        in_specs=[pl.BlockSpec((nr, 1, bsz, cq), zmap),
                  pl.BlockSpec((conv_w.shape[0], cq), lambda d, q, w: (0, q)),
                  pl.BlockSpec((1, cq), lambda d, q, w: (0, q)),
                  pl.BlockSpec((1, 2, cq // MXU_N, MXU_N, MXU_N), lambda d, q, w: (d, 0, q, 0, 0)),
                  pl.BlockSpec((1, 2, cq), lambda d, q, w: (d, 0, q)),
                  pl.BlockSpec((1, 1, cq), lambda d, q, w: (d, 0, q)),
                  pl.BlockSpec((1, bsz, cq), lambda d, q, w: (d, 0, q))],
        out_specs=[pl.BlockSpec((1, nr, 1, bsz, cq), hmap),
                   pl.BlockSpec((1, bsz, cq), lambda d, q, w: (d, 0, q))],
        out_shape=[jax.ShapeDtypeStruct((2, nr, nw, bsz, c), F32),
                   jax.ShapeDtypeStruct((2, bsz, c), F32)],
        scratch_shapes=[pltpu.VMEM((nr, bsz, cq), F32), pltpu.VMEM((nr, bsz, cq), F32)],
        compiler_params=_cparams("arbitrary", "arbitrary", "arbitrary"),
        name="lru_fused",
    )(z4, conv_w, conv_b, gate_w, gate_b, sp_l, h0)


def _to_pairs(q, out_ref, lead=()):
    rows, c = q.shape
    npairs = rows // (2 * SUBLANE)
    lane = lax.broadcasted_iota(jnp.int32, (SUBLANE, LANE), 1)
    low = lane < RWKV_HEAD
    for p in range(npairs):
        ev = q[(2 * p) * SUBLANE:(2 * p + 1) * SUBLANE]
        od = q[(2 * p + 1) * SUBLANE:(2 * p + 2) * SUBLANE]
        pieces = []
        for j in range(c // LANE):
            e_j = ev[:, j * LANE:(j + 1) * LANE]
            o_j = od[:, j * LANE:(j + 1) * LANE]
            pieces.append(jnp.where(low, e_j, pltpu.roll(o_j, RWKV_HEAD, 1)))
            pieces.append(jnp.where(low, pltpu.roll(e_j, RWKV_HEAD, 1), o_j))
        x = jnp.concatenate(pieces, axis=0)
        out_ref[lead + (p,)] = x.T


def _group_sum(x, ones_ref):
    hi = x.astype(BF16)
    lo = (x - hi.astype(F32)).astype(BF16)
    cols = []
    for q in range(x.shape[1] // MXU_N):
        sl = slice(q * MXU_N, (q + 1) * MXU_N)
        cols.append(jnp.dot(hi[:, sl], ones_ref[...], preferred_element_type=F32)
                    + jnp.dot(lo[:, sl], ones_ref[...], preferred_element_type=F32))
    return jnp.concatenate(cols, axis=1)


def _prep_kernel(period, z_ref, zp_ref, zn_ref, mu_ref, w0_ref, wup_ref, a0_ref, aup_ref,
                 kk_ref, ka_ref, rk_ref, gup_ref, ones_ref,
                 r_o, v_o, kn_o, w_o, kd_o, b_o, bv_o, g_o):
    i = pl.program_id(0)
    rows, _ = z_ref.shape
    tt = rows // SUBLANE
    cw = w0_ref.shape[1]
    z = z_ref[...]
    has_prev = ((i * tt) % period != 0).astype(F32)
    has_next = (((i + 1) * tt) % period != 0).astype(F32)
    zprev = jnp.concatenate([zp_ref[...] * has_prev, z[:rows - SUBLANE]], axis=0)
    znext = jnp.concatenate([z[SUBLANE:], zn_ref[...] * has_next], axis=0)
    zs = z + mu_ref[0:1, :] * (zprev - z) + mu_ref[1:2, :] * (znext - z)
    r = zs[:, 0:cw]
    k = zs[:, cw:2 * cw]
    v = zs[:, 2 * cw:3 * cw]
    off = 3 * cw
    wd = jnp.tanh(zs[:, off:off + LANE])
    ad = zs[:, off + LANE:off + 2 * LANE]
    gd = jax.nn.sigmoid(zs[:, off + 2 * LANE:off + 2 * LANE + MXU_N])
    wl = jnp.dot(wd.astype(BF16), wup_ref[...], preferred_element_type=F32)
    al = jnp.dot(ad.astype(BF16), aup_ref[...], preferred_element_type=F32)
    g_o[...] = jnp.dot(gd.astype(BF16), gup_ref[...], preferred_element_type=F32)
    kk = k * kk_ref[...]
    nrm = jnp.sqrt(_group_sum(kk * kk, ones_ref))
    kn = kk / jnp.maximum(nrm, 1e-12)
    _to_pairs(r, r_o)
    _to_pairs(v, v_o)
    _to_pairs(kn, kn_o)
    ksum = jnp.zeros_like(k)
    for d in range(2):
        xw = w0_ref[d:d + 1, :] + wl[:, d * cw:(d + 1) * cw]
        decay = jnp.exp(-DECAY_SCALE * jax.nn.sigmoid(xw))
        a = jax.nn.sigmoid(a0_ref[d:d + 1, :] + al[:, d * cw:(d + 1) * cw])
        kd = k * (1.0 + (a - 1.0) * ka_ref[...])
        ksum = ksum + kd
        _to_pairs(decay, w_o, (d,))
        _to_pairs(kd, kd_o, (d,))
        _to_pairs(kn * a, b_o, (d,))
    bonus = _group_sum(r * ksum * rk_ref[...], ones_ref)
    bv_o[...] = bonus * v


def _rwkv_prep(z, period, zcols, mu, w0, wup, a0, aup, k_k, k_a, r_k, gup, ones, tt=32):
    rows_total = z.shape[0]
    t = rows_total // SUBLANE
    cw = w0.shape[1]
    tt = min(tt, period)
    assert t % tt == 0 and period % tt == 0 and tt % 2 == 0
    rows = tt * SUBLANE
    nblk = t // tt
    const = lambda shape: pl.BlockSpec(shape, lambda i: (0,) * len(shape))
    pair = pl.BlockSpec((tt // 2, LANE, LANE), lambda i: (i, 0, 0))
    pair_d = pl.BlockSpec((2, tt // 2, LANE, LANE), lambda i: (0, i, 0, 0))
    nat = pl.BlockSpec((rows, cw), lambda i: (i, 0))
    sh_pair = jax.ShapeDtypeStruct((t // 2, LANE, LANE), F32)
    sh_pair_d = jax.ShapeDtypeStruct((2, t // 2, LANE, LANE), F32)
    sh_nat = jax.ShapeDtypeStruct((rows_total, cw), F32)
    return pl.pallas_call(
        functools.partial(_prep_kernel, period),
        grid=(nblk,),
        in_specs=[pl.BlockSpec((rows, zcols), lambda i: (i, 0)),
                  pl.BlockSpec((SUBLANE, zcols), lambda i: (jnp.maximum(i * tt - 1, 0), 0)),
                  pl.BlockSpec((SUBLANE, zcols), lambda i: (jnp.minimum((i + 1) * tt, t - 1), 0)),
                  const(mu.shape), const(w0.shape), const(wup.shape), const(a0.shape),
                  const(aup.shape), const(k_k.shape), const(k_a.shape), const(r_k.shape),
                  const(gup.shape), const(ones.shape)],
        out_specs=[pair, pair, pair, pair_d, pair_d, pair_d, nat, nat],
        out_shape=[sh_pair, sh_pair, sh_pair, sh_pair_d, sh_pair_d, sh_pair_d, sh_nat, sh_nat],
        compiler_params=_cparams("parallel"),
        name="rwkv_prep",
    )(z, z, z, mu, w0, wup, a0, aup, k_k, k_a, r_k, gup, ones)


def _scan_kernel(reverse, r_ref, w_ref, k_ref, v_ref, kk_ref, b_ref, s0_ref, *out_refs):
    y_ref = out_refs[0] if len(out_refs) == 2 else None
    s_ref = out_refs[-1]
    n = RWKV_HEAD
    npairs = r_ref.shape[0]

    @pl.when(pl.program_id(0) == 0)
    def _():
        s_ref[...] = s0_ref[...]

    def removal(p, base):
        def body(kb, acc):
            k0 = pl.multiple_of(kb * KCH, KCH)
            for j in range(KCH):
                acc = acc + s_ref[k0 + j] * kk_ref[p, pl.ds(base + k0 + j, 1), :]
            return acc

        return lax.fori_loop(0, n // KCH, body, jnp.zeros((n, LANE), F32))

    def one_step(p, base, pn, basen, acc):
        sa = -acc
        vv = v_ref[p, pl.ds(base, n), :]

        def body(kb, carry):
            y, nacc = carry
            k0 = pl.multiple_of(kb * KCH, KCH)
            for j in range(KCH):
                row = pl.ds(base + k0 + j, 1)
                sk = (s_ref[k0 + j] * w_ref[0, p, row, :]
                      + (sa * b_ref[0, p, row, :] + vv * k_ref[0, p, row, :]))
                s_ref[k0 + j] = sk
                if y_ref is not None:
                    y = y + sk * r_ref[p, row, :]
                nacc = nacc + sk * kk_ref[pn, pl.ds(basen + k0 + j, 1), :]
            return y, nacc

        zero = jnp.zeros((n, LANE), F32)
        y, nacc = lax.fori_loop(0, n // KCH, body, (zero, zero))
        if y_ref is not None:
            y_ref[p, pl.ds(base, n), :] = y
        return nacc

    first, second = ((n, 0) if reverse else (0, n))
    p_first = (npairs - 1) if reverse else 0

    def pair_step(i, acc):
        p = (npairs - 1 - i) if reverse else i
        pn = jnp.clip(p + (-1 if reverse else 1), 0, npairs - 1)
        acc = one_step(p, first, p, second, acc)
        return one_step(p, second, pn, first, acc)

    lax.fori_loop(0, npairs, pair_step, removal(p_first, first))


def _rwkv_scan(reverse, r, w, k, v, kk, b, s0, with_y=True, tbp=16):
    np_total = r.shape[0]
    tbp = min(tbp, np_total)
    assert np_total % tbp == 0
    nblk = np_total // tbp
    d = 1 if reverse else 0
    imap = (lambda i: (nblk - 1 - i, 0, 0)) if reverse else (lambda i: (i, 0, 0))
    imap_d = (lambda i: (d, nblk - 1 - i, 0, 0)) if reverse else (lambda i: (d, i, 0, 0))
    sp = pl.BlockSpec((tbp, LANE, LANE), imap)
    sp_d = pl.BlockSpec((1, tbp, LANE, LANE), imap_d)
    st = pl.BlockSpec((RWKV_HEAD, RWKV_HEAD, LANE), lambda i: (0, 0, 0))
    return pl.pallas_call(
        functools.partial(_scan_kernel, reverse),
        grid=(nblk,),
        in_specs=[sp, sp_d, sp_d, sp, sp, sp_d, st],
        out_specs=([sp] if with_y else []) + [st],
        out_shape=([jax.ShapeDtypeStruct(r.shape, F32)] if with_y else [])
        + [jax.ShapeDtypeStruct((RWKV_HEAD, RWKV_HEAD, LANE), F32)],
        compiler_params=_cparams("arbitrary"),
        name="rwkv_scan",
    )(r, w, k, v, kk, b, s0)


def _from_pairs(y_ref, nat_ref):
    npairs = y_ref.shape[0]
    lane = lax.broadcasted_iota(jnp.int32, (SUBLANE, LANE), 1)
    low = lane < RWKV_HEAD
    for p in range(npairs):
        xt = y_ref[p].T
        for j in range(nat_ref.shape[1] // LANE):
            h0 = xt[(2 * j) * SUBLANE:(2 * j + 1) * SUBLANE]
            h1 = xt[(2 * j + 1) * SUBLANE:(2 * j + 2) * SUBLANE]
            ev = jnp.where(low, h0, pltpu.roll(h1, RWKV_HEAD, 1))
            od = jnp.where(low, pltpu.roll(h0, RWKV_HEAD, 1), h1)
            nat_ref[(2 * p) * SUBLANE:(2 * p + 1) * SUBLANE, j * LANE:(j + 1) * LANE] = ev
            nat_ref[(2 * p + 1) * SUBLANE:(2 * p + 2) * SUBLANE, j * LANE:(j + 1) * LANE] = od


def _merge_kernel(y0_ref, y1_ref, bv_ref, go_ref, h0_ref, h1_ref, zg_ref, zgate_ref, x_ref,
                  lnx_ref, wa_ref, wb_ref, wo_ref, gt1_ref, gpost_ref, gpre_ref, scp_ref, sh_ref,
                  h_o, u_o, ub_o, yn_ref, nat_ref):
    n = RWKV_HEAD
    d = wa_ref.shape[1]
    npairs = y0_ref.shape[0]
    tt, bsz, _ = x_ref.shape
    for p in range(npairs):
        y = y0_ref[p] + y1_ref[p]
        for t2 in range(2):
            yh = y[t2 * n:(t2 + 1) * n]
            mu = jnp.mean(yh, axis=0, keepdims=True)
            var = jnp.mean(jnp.square(yh - mu), axis=0, keepdims=True)
            yn_ref[p, t2 * n:(t2 + 1) * n, :] = (yh - mu) * lax.rsqrt(var + LNX_EPS)
    _from_pairs(yn_ref, nat_ref)
    y_a = (nat_ref[...] * lnx_ref[0:1, :] + lnx_ref[1:2, :] + bv_ref[...]) * go_ref[...]
    pa = jnp.dot(y_a.astype(BF16), wa_ref[...], preferred_element_type=F32)
    y_b = jax.nn.gelu(zg_ref[...]) * (h0_ref[0] + h1_ref[0])
    pb = jnp.dot(y_b.astype(BF16), wb_ref[...], preferred_element_type=F32)
    gates = jax.nn.sigmoid(zgate_ref[...])
    mixed = gates[:, :d] * pa + gates[:, d:] * pb
    yo = jnp.dot(mixed.astype(BF16), wo_ref[...], preferred_element_type=F32).reshape(tt, bsz, d)
    yo = yo * lax.rsqrt(jnp.mean(yo * yo, axis=-1, keepdims=True) + NORM_EPS) * gpost_ref[...]
    h = x_ref[...] + gt1_ref[...] * yo
    h_o[...] = h
    u = h * lax.rsqrt(jnp.mean(h * h, axis=-1, keepdims=True) + NORM_EPS) * gpre_ref[...]
    u = (u * scp_ref[...] + sh_ref[...]).reshape(tt * bsz, d)
    u_o[...] = u
    ub_o[...] = u.astype(BF16)


def _merge(y0, y1, bv, g_out, h_lru, z, x_tm, lnx, wa, wb, wo, gt1, g_post, g_pre, scp, sh, tt=32):
    t, bsz, d = x_tm.shape
    rows = tt * bsz
    c = wa.shape[0]
    zcol = lambda blk, width: pl.BlockSpec((rows, width), lambda i: (i, blk))
    const = lambda shape: pl.BlockSpec(shape, lambda i: (0,) * len(shape))
    pair = pl.BlockSpec((tt // 2, LANE, LANE), lambda i: (i, 0, 0))
    nat = pl.BlockSpec((rows, c), lambda i: (i, 0))
    lg_blk = (3 * c + c + c) // c
    gate_blk = (lg_blk + 1) * c // (2 * d)
    return pl.pallas_call(
        _merge_kernel,
        grid=(t // tt,),
        in_specs=[pair, pair, nat, nat,
                  pl.BlockSpec((1, rows, c), lambda i: (0, i, 0)),
                  pl.BlockSpec((1, rows, c), lambda i: (1, i, 0)),
                  zcol(lg_blk, c), zcol(gate_blk, 2 * d),
                  pl.BlockSpec((tt, bsz, d), lambda i: (i, 0, 0)),
                  const(lnx.shape), const(wa.shape), const(wb.shape), const(wo.shape),
                  const(gt1.shape), const(g_post.shape), const(g_pre.shape),
                  const(scp.shape), const(sh.shape)],
        out_specs=[pl.BlockSpec((tt, bsz, d), lambda i: (i, 0, 0)),
                   pl.BlockSpec((rows, d), lambda i: (i, 0)),
                   pl.BlockSpec((rows, d), lambda i: (i, 0))],
        out_shape=[jax.ShapeDtypeStruct((t, bsz, d), F32),
                   jax.ShapeDtypeStruct((t * bsz, d), F32),
                   jax.ShapeDtypeStruct((t * bsz, d), BF16)],
        scratch_shapes=[pltpu.VMEM((tt // 2, LANE, LANE), F32), pltpu.VMEM((rows, c), F32)],
        compiler_params=_cparams("parallel"),
        name="merge",
    )(y0, y1, bv, g_out, h_lru, h_lru, z, z, x_tm, lnx, wa, wb, wo, gt1, g_post, g_pre, scp, sh)


def _experts_kernel(blk_ref, eid_ref, lo_ref, hi_ref, x_ref, wgu_ref, wdn_ref, o_ref, wgu_b, wdn_b):
    i = pl.program_id(0)
    prev = jnp.maximum(i - 1, 0)
    ff = wdn_ref.shape[1]

    @pl.when(jnp.logical_or(i == 0, eid_ref[i] != eid_ref[prev]))
    def _():
        wgu_b[...] = wgu_ref[0].astype(BF16)
        wdn_b[...] = wdn_ref[0].astype(BF16)

    gu = jnp.dot(x_ref[...], wgu_b[...], preferred_element_type=F32)
    g = gu[:, :ff]
    u = gu[:, ff:]
    act = (g * jax.nn.sigmoid(g) * u).astype(BF16)
    res = jnp.dot(act, wdn_b[...], preferred_element_type=F32)
    row = lax.broadcasted_iota(jnp.int32, (res.shape[0], 1), 0)
    mine = jnp.logical_and(row >= lo_ref[i], row < hi_ref[i])
    first = jnp.logical_or(i == 0, blk_ref[i] != blk_ref[prev])

    @pl.when(first)
    def _():
        o_ref[...] = jnp.where(mine, res, 0.0).astype(o_ref.dtype)

    @pl.when(jnp.logical_not(first))
    def _():
        o_ref[...] = jnp.where(mine, res, o_ref[...].astype(F32)).astype(o_ref.dtype)


def _grouped_experts(xs, items, w_gu, w_down):
    n_rows, d = xs.shape
    _, _, ff2 = w_gu.shape
    ff = ff2 // 2
    n_items = items[0].shape[0]
    grid_spec = pltpu.PrefetchScalarGridSpec(
        num_scalar_prefetch=4,
        grid=(n_items,),
        in_specs=[pl.BlockSpec((MOE_ROWS, d), lambda i, blk, eid, lo, hi: (blk[i], 0)),
                  pl.BlockSpec((1, d, ff2), lambda i, blk, eid, lo, hi: (eid[i], 0, 0)),
                  pl.BlockSpec((1, ff, d), lambda i, blk, eid, lo, hi: (eid[i], 0, 0))],
        out_specs=pl.BlockSpec((MOE_ROWS, d), lambda i, blk, eid, lo, hi: (blk[i], 0)),
        scratch_shapes=[pltpu.VMEM((d, ff2), BF16), pltpu.VMEM((ff, d), BF16)],
    )
    return pl.pallas_call(
        _experts_kernel,
        grid_spec=grid_spec,
        out_shape=jax.ShapeDtypeStruct((n_rows, d), BF16),
        compiler_params=_cparams("arbitrary"),
        name="moe_experts",
    )(*items, xs, w_gu, w_down)


def _expert_items(counts, n_rows):
    n_e = counts.shape[0]
    n_blocks = n_rows // MOE_ROWS
    n_items = n_blocks + n_e
    end = jnp.cumsum(counts)
    start = end - counts
    first_blk = start // MOE_ROWS
    per_e = jnp.where(counts > 0, (end - 1) // MOE_ROWS - first_blk + 1, 0)
    item_end = jnp.cumsum(per_e)
    w = jnp.arange(n_items, dtype=jnp.int32)
    eid = jnp.minimum(jnp.sum((item_end[None, :] <= w[:, None]).astype(jnp.int32), axis=1), n_e - 1)
    onehot = (eid[:, None] == jnp.arange(n_e)[None, :]).astype(jnp.int32)
    pick = lambda v: jnp.sum(onehot * v[None, :], axis=1)
    valid = w < item_end[-1]
    blk = jnp.where(valid, pick(first_blk) + w - (pick(item_end) - pick(per_e)), n_blocks - 1)
    lo = jnp.clip(pick(start) - blk * MOE_ROWS, 0, MOE_ROWS)
    hi = jnp.where(valid, jnp.clip(pick(end) - blk * MOE_ROWS, 0, MOE_ROWS), lo)
    i32 = lambda v: v.astype(jnp.int32)
    return (i32(blk), i32(eid), i32(lo), i32(hi)), i32(start)


def _argmax_pick(cur, iota, big, axis):
    mx = jnp.max(cur, axis=axis, keepdims=True)
    ix = jnp.min(jnp.where(cur == mx, iota, big), axis=axis, keepdims=True)
    return iota == ix


def _router_kernel(u_ref, wt_ref, bias_ref, tri_ref, gate_o, idx_o, rank_o, cnt_o):
    n_e = wt_ref.shape[0]
    tr = u_ref.shape[0]
    per_g = n_e // N_GROUPS
    neg = -jnp.inf

    @pl.when(pl.program_id(0) == 0)
    def _():
        cnt_o[...] = jnp.zeros_like(cnt_o)

    logits = lax.dot_general(wt_ref[...], u_ref[...], (((1,), (1,)), ((), ())),
                             precision=HP, preferred_element_type=F32)
    scores = jax.nn.sigmoid(logits)
    biased = scores + bias_ref[...]
    b3 = biased.reshape(N_GROUPS, per_g, tr)
    ie = lax.broadcasted_iota(jnp.int32, b3.shape, 1)
    p1 = _argmax_pick(b3, ie, per_g, 1)
    m1 = jnp.max(b3, axis=1)
    m2 = jnp.max(jnp.where(p1, neg, b3), axis=1)
    cur = m1 + m2
    ig = lax.broadcasted_iota(jnp.int32, cur.shape, 0)
    sel_g = jnp.zeros(cur.shape, F32)
    for _ in range(TOPK_GROUPS):
        pick = _argmax_pick(cur, ig, N_GROUPS, 0)
        sel_g = jnp.where(pick, 1.0, sel_g)
        cur = jnp.where(pick, neg, cur)
    cur = jnp.where(sel_g[:, None, :] > 0.5, b3, neg).reshape(n_e, tr)
    i_e = lax.broadcasted_iota(jnp.int32, cur.shape, 0)
    i_ef = i_e.astype(F32)
    picks, top_s = [], []
    sel_f = jnp.zeros(cur.shape, F32)
    for _ in range(TOP_K):
        pick = _argmax_pick(cur, i_e, n_e, 0)
        picks.append(pick)
        top_s.append(jnp.sum(jnp.where(pick, scores, 0.0), axis=0, keepdims=True))
        sel_f = jnp.where(pick, 1.0, sel_f)
        cur = jnp.where(pick, neg, cur)
    denom = top_s[0]
    for s in top_s[1:]:
        denom = denom + s
    rank = jnp.dot(sel_f.astype(BF16), tri_ref[...], preferred_element_type=F32) + cnt_o[:, 0:1]
    cnt_o[...] = cnt_o[...] + jnp.sum(sel_f, axis=1, keepdims=True)
    zrow = jnp.zeros((SUBLANE - TOP_K, tr), F32)
    gate_o[...] = jnp.concatenate([s / denom * ROUTED_SCALE for s in top_s] + [zrow], axis=0)
    idx_rows = [jnp.sum(jnp.where(p, i_ef, 0.0), axis=0, keepdims=True) for p in picks]
    idx_o[...] = jnp.concatenate(idx_rows + [zrow], axis=0).astype(jnp.int32)
    rank_rows = [jnp.sum(jnp.where(p, rank, 0.0), axis=0, keepdims=True) for p in picks]
    rank_o[...] = jnp.concatenate(rank_rows + [zrow], axis=0).astype(jnp.int32)


def _router(u, w_t, bias, tr=512):
    m, d = u.shape
    n_e = w_t.shape[0]
    tri = (jnp.arange(tr)[:, None] < jnp.arange(tr)[None, :]).astype(BF16)
    tok = pl.BlockSpec((SUBLANE, tr), lambda i: (0, i))
    return pl.pallas_call(
        _router_kernel,
        grid=(m // tr,),
        in_specs=[pl.BlockSpec((tr, d), lambda i: (i, 0)),
                  pl.BlockSpec((n_e, d), lambda i: (0, 0)),
                  pl.BlockSpec((n_e, 1), lambda i: (0, 0)),
                  pl.BlockSpec((tr, tr), lambda i: (0, 0))],
        out_specs=[tok, tok, tok, pl.BlockSpec((n_e, LANE), lambda i: (0, 0))],
        out_shape=[jax.ShapeDtypeStruct((SUBLANE, m), F32),
                   jax.ShapeDtypeStruct((SUBLANE, m), jnp.int32),
                   jax.ShapeDtypeStruct((SUBLANE, m), jnp.int32),
                   jax.ShapeDtypeStruct((n_e, LANE), F32)],
        compiler_params=_cparams("arbitrary"),
        name="router",
    )(u, w_t, bias, tri)


def _final_kernel(*refs):
    ye_refs = refs[:TOP_K]
    gate_ref, ub_ref, h_ref, wgu_ref, wdn_ref, gt_ref, g_ref, o_ref = refs[TOP_K:]
    tt, bsz, d = h_ref.shape
    ff = wdn_ref.shape[0]
    routed = jnp.zeros((tt * bsz, d), F32)
    for j in range(TOP_K):
        routed = routed + gate_ref[:, j:j + 1] * ye_refs[j][0].astype(F32)
    gu = jnp.dot(ub_ref[...], wgu_ref[...], preferred_element_type=F32)
    g = gu[:, :ff]
    act = (g * jax.nn.sigmoid(g) * gu[:, ff:]).astype(BF16)
    moe = (routed + jnp.dot(act, wdn_ref[...], preferred_element_type=F32)).reshape(tt, bsz, d)
    moe = moe * lax.rsqrt(jnp.mean(moe * moe, axis=-1, keepdims=True) + NORM_EPS) * g_ref[...]
    o_ref[...] = h_ref[...] + gt_ref[...] * moe


def _final(ye, gate, ub, h, wgu, wdn, gt, g, tt=64):
    t, bsz, d = h.shape
    rows = tt * bsz
    const = lambda shape: pl.BlockSpec(shape, lambda i: (0,) * len(shape))
    return pl.pallas_call(
        _final_kernel,
        grid=(t // tt,),
        in_specs=[pl.BlockSpec((1, rows, d), functools.partial(lambda j, i: (j, i, 0), j))
                  for j in range(TOP_K)] + [
                  pl.BlockSpec((rows, SUBLANE), lambda i: (i, 0)),
                  pl.BlockSpec((rows, d), lambda i: (i, 0)),
                  pl.BlockSpec((tt, bsz, d), lambda i: (i, 0, 0)),
                  const(wgu.shape), const(wdn.shape), const(gt.shape), const(g.shape)],
        out_specs=pl.BlockSpec((tt, bsz, d), lambda i: (i, 0, 0)),
        out_shape=jax.ShapeDtypeStruct((t, bsz, d), F32),
        compiler_params=_cparams("parallel"),
        name="moe_final",
    )(*([ye] * TOP_K), gate, ub, h, wgu, wdn, gt, g)


def kernel(x, c, ctx, c_ctx, w_mod, b_mod, norm_g, w_in, shift_mu, rw_w0, rw_w_up, rw_a0, rw_a_up,
           rw_g_up, rw_k_k, rw_k_a, rw_r_k, rw_lnx, lru_conv_w, lru_conv_b, lru_gate_w, lru_gate_b,
           lru_l, w_branch_a, w_branch_b, w_out, router_w, router_b, ex_w_gu, ex_w_down,
           sh_w_gu, sh_w_down):
    l = 0
    bsz, t, d = x.shape
    tc = ctx.shape[1]
    rows_g = t // GRID_W
    m = bsz * t
    cw = rw_w0.shape[-1]
    cl = lru_l.shape[-1]
    nh = cw // RWKV_HEAD
    assert bsz == SUBLANE and cw == cl == d and 2 * nh * bsz // 2 == LANE
    rwkv_cols = 3 * cw + 2 * DECAY_LORA + 2 * AAA_LORA + GATE_LORA
    lora_cols = rwkv_cols - 3 * cw

    mod = jnp.dot(jax.nn.silu(c), w_mod[l], precision=HP) + b_mod[l]
    mod_c = jnp.dot(jax.nn.silu(c_ctx), w_mod[l], precision=HP) + b_mod[l]
    sh1, sc1, gt1, sh2, sc2, gt2 = jnp.split(mod, 6, axis=-1)
    csh1 = jnp.broadcast_to(mod_c[0:d], (bsz, d))
    csc1 = jnp.broadcast_to(mod_c[d:2 * d], (bsz, d))
    g_pre1, g_post1, g_pre2, g_post2 = [g[None, :] for g in norm_g[l]]

    wi = w_in[l]
    w_al = jnp.concatenate([wi[:, :rwkv_cols], jnp.zeros((d, cw - lora_cols), F32),
                            wi[:, rwkv_cols:]], axis=1).astype(BF16)
    zr_w = 4 * cw
    mu_al = jnp.concatenate([shift_mu[l], jnp.zeros((2, cw - lora_cols), F32)], axis=1)

    x_tm = jnp.transpose(x, (1, 0, 2))
    ctx_tm = jnp.transpose(ctx, (1, 0, 2))
    zc = _inproj(ctx_tm, 1.0 + csc1, csh1, g_pre1, w_al[:, :zr_w + cl])
    z = _inproj(x_tm, 1.0 + sc1, sh1, g_pre1, w_al, tn=2 * cw)

    blk = cl // LRU_BLOCKS
    per = MXU_N // blk
    gw = lru_gate_w[l].reshape(2, 2, LRU_BLOCKS // per, per, blk, blk)
    eye = jnp.eye(per, dtype=F32)
    gw_bd = jnp.einsum('dgqpjk,pr->dgqpjrk', gw, eye).reshape(2, 2, LRU_BLOCKS // per, MXU_N, MXU_N)
    gw_bd = gw_bd.astype(BF16)
    gb = lru_gate_b[l].reshape(2, 2, cl)
    sp_l = jax.nn.softplus(-lru_l[l])[:, None, :]
    lru_args = (lru_conv_w[l], lru_conv_b[l][None, :], gw_bd, gb, sp_l)
    _, hfin = _lru(zc.reshape(tc, 1, bsz, -1), 4, *lru_args, jnp.zeros((2, bsz, cl), F32))
    h_lru, _ = _lru(z.reshape(rows_g, GRID_W, bsz, -1), 4, *lru_args, hfin)
    h_lru = h_lru.reshape(2, m, cl)

    zeros_c = jnp.zeros((DECAY_LORA, cw), F32)
    wup2 = jnp.concatenate([jnp.concatenate([rw_w_up[l, 0], zeros_c], 1),
                            jnp.concatenate([zeros_c, rw_w_up[l, 1]], 1)], axis=0)
    aup2 = jnp.concatenate([jnp.concatenate([rw_a_up[l, 0], zeros_c], 1),
                            jnp.concatenate([zeros_c, rw_a_up[l, 1]], 1)], axis=0)
    gup = jnp.concatenate([rw_g_up[l], jnp.zeros((MXU_N - GATE_LORA, cw), F32)], axis=0)
    hid = jnp.arange(MXU_N) // RWKV_HEAD
    ones_blk = (hid[:, None] == hid[None, :]).astype(BF16)
    prep_args = (mu_al, rw_w0[l], wup2.astype(BF16), rw_a0[l], aup2.astype(BF16),
                 rw_k_k[l][None, :], rw_k_a[l][None, :], rw_r_k[l].reshape(1, cw),
                 gup.astype(BF16), ones_blk)
    pc = _rwkv_prep(zc, tc, zr_w, *prep_args)
    pz = _rwkv_prep(z, GRID_W, zr_w, *prep_args)
    s_zero = jnp.zeros((RWKV_HEAD, RWKV_HEAD, LANE), F32)
    ys = []
    for rev in (False, True):
        (s_c,) = _rwkv_scan(rev, pc[0], pc[3], pc[4], pc[1], pc[2], pc[5], s_zero, with_y=False)
        y_d, _ = _rwkv_scan(rev, pz[0], pz[3], pz[4], pz[1], pz[2], pz[5], s_c)
        ys.append(y_d)

    h, u2, u2b = _merge(ys[0], ys[1], pz[6], pz[7], h_lru, z, x_tm, rw_lnx[l],
                        w_branch_a[l].astype(BF16), w_branch_b[l].astype(BF16), w_out[l].astype(BF16),
                        gt1, g_post1, g_pre2, 1.0 + sc2, sh2)

    n_e = router_w.shape[-1]
    gate6, idx6, rank6, cnt = _router(u2, router_w[l].T, router_b[l][:, None])
    n_assign = m * TOP_K
    items, start = _expert_items(cnt[:, 0].astype(jnp.int32), n_assign)
    flat_e = idx6[:TOP_K].T.reshape(-1)
    sorted_tok = jnp.argsort(flat_e).astype(jnp.int32) // TOP_K
    ys_e = _grouped_experts(u2b[sorted_tok], items, ex_w_gu[l], ex_w_down[l])
    sel = idx6[:TOP_K, :, None] == jnp.arange(n_e, dtype=jnp.int32)[None, None, :]
    pos = jnp.sum(jnp.where(sel, start[None, None, :], 0), axis=-1) + rank6[:TOP_K]
    ye = ys_e[pos.reshape(-1)].reshape(TOP_K, m, d)
    out_tm = _final(ye, gate6.T, u2b, h, sh_w_gu[l].astype(BF16), sh_w_down[l].astype(BF16),
                    gt2, g_post2)
    return jnp.transpose(out_tm, (1, 0, 2))
```

```python
import functools

import jax
import jax.numpy as jnp
from jax import lax
from jax.experimental import pallas as pl
from jax.experimental.pallas import tpu as pltpu

GRID_W = 64
NORM_EPS = 1e-6
RWKV_HEAD = 64
DECAY_LORA = 64
AAA_LORA = 64
GATE_LORA = 160
LNX_EPS = 64e-5
LRU_BLOCKS = 16
CONV_LEFT = 2
LRU_C = 8.0
N_GROUPS = 8
TOPK_GROUPS = 4
TOP_K = 6
ROUTED_SCALE = 2.5
DECAY_SCALE = 0.6065306597126334

LANE = 128
SUBLANE = 8
MXU_N = 256
MOE_ROWS = 512
KCH = 32
LRU_BLOCK_BYTES = 4 * 1024 * 1024
VMEM_LIMIT = 48 * 1024 * 1024
HP = lax.Precision.HIGHEST
F32 = jnp.float32
BF16 = jnp.bfloat16


def _cparams(*sem):
    return pltpu.CompilerParams(dimension_semantics=sem, vmem_limit_bytes=VMEM_LIMIT)


def _inproj_kernel(x_ref, scp_ref, sh_ref, g_ref, w_ref, o_ref, xm_ref):
    @pl.when(pl.program_id(1) == 0)
    def _():
        x = x_ref[...]
        ms = jnp.mean(x * x, axis=-1, keepdims=True)
        xm = x * lax.rsqrt(ms + NORM_EPS) * g_ref[...] * scp_ref[...] + sh_ref[...]
        xm_ref[...] = xm.reshape(xm_ref.shape).astype(BF16)

    o_ref[...] = jnp.dot(xm_ref[...], w_ref[...], preferred_element_type=F32)


def _inproj(x_tm, scp, sh, g, w, tt=128, tn=1024):
    t, bsz, d = x_tm.shape
    n = w.shape[1]
    tt = min(tt, t)
    assert t % tt == 0 and n % tn == 0
    return pl.pallas_call(
        _inproj_kernel,
        grid=(t // tt, n // tn),
        in_specs=[pl.BlockSpec((tt, bsz, d), lambda i, j: (i, 0, 0)),
                  pl.BlockSpec((bsz, d), lambda i, j: (0, 0)),
                  pl.BlockSpec((bsz, d), lambda i, j: (0, 0)),
                  pl.BlockSpec((1, d), lambda i, j: (0, 0)),
                  pl.BlockSpec((d, tn), lambda i, j: (0, j))],
        out_specs=pl.BlockSpec((tt * bsz, tn), lambda i, j: (i, j)),
        out_shape=jax.ShapeDtypeStruct((t * bsz, n), F32),
        scratch_shapes=[pltpu.VMEM((tt * bsz, d), BF16)],
        compiler_params=_cparams("parallel", "arbitrary"),
        name="in_proj",
    )(x_tm, scp, sh, g, w)


def _lru_kernel(z_ref, cw_ref, cb_ref, gw_ref, gb_ref, sp_ref, h0_ref, h_ref, hf_ref,
                a_ref, bx_ref):
    d = pl.program_id(0)
    w = pl.program_id(2)
    nr, _, bsz, cq = z_ref.shape
    taps = cw_ref.shape[0]

    @pl.when(w == 0)
    def _():
        hf_ref[0] = h0_ref[0]

    zx = z_ref[:, 0]
    xc = jnp.zeros_like(zx) + cb_ref[...]
    for j in range(taps):
        off = j - CONV_LEFT
        if off < 0:
            sh = jnp.concatenate([jnp.zeros((-off, bsz, cq), F32), zx[:nr + off]], axis=0)
        elif off > 0:
            sh = jnp.concatenate([zx[off:], jnp.zeros((off, bsz, cq), F32)], axis=0)
        else:
            sh = zx
        xc = xc + cw_ref[j:j + 1, :] * sh
    x2 = xc.reshape(nr * bsz, cq)
    xb = x2.astype(BF16)
    pre = []
    for g in range(2):
        cols = []
        for q in range(cq // MXU_N):
            cols.append(jnp.dot(xb[:, q * MXU_N:(q + 1) * MXU_N], gw_ref[0, g, q],
                                preferred_element_type=F32))
        pre.append(jnp.concatenate(cols, axis=1) + gb_ref[0, g:g + 1, :])
    rec = 0.5 * jnp.tanh(0.5 * pre[0]) + 0.5
    inp = 0.5 * jnp.tanh(0.5 * pre[1]) + 0.5
    log_a = -LRU_C * rec * sp_ref[0]
    a = jnp.exp(log_a)
    bx = jnp.sqrt(-jnp.tanh(log_a) * (a * a + 1.0)) * inp * x2
    a_ref[...] = a.reshape(nr, bsz, cq)
    bx_ref[...] = bx.reshape(nr, bsz, cq)

    def step(i, h):
        r = i + d * (nr - 1 - 2 * i)
        h = a_ref[r] * h + bx_ref[r]
        h_ref[0, r, 0] = h
        return h

    hf_ref[0] = lax.fori_loop(0, nr, step, hf_ref[0])


def _lru(z4, col_blk, conv_w, conv_b, gate_w, gate_b, sp_l, h0):
    nr, nw, bsz, _ = z4.shape
    c = conv_w.shape[1]
    cq = min(c, max(MXU_N, LRU_BLOCK_BYTES // (nr * bsz * 4)))
    nq = c // cq
    cb0 = col_blk * (c // cq)

    def zmap(d, q, w):
        return (0, w + d * (nw - 1 - 2 * w), 0, cb0 + q)

    def hmap(d, q, w):
        return (d, 0, w + d * (nw - 1 - 2 * w), 0, q)

    return pl.pallas_call(
        _lru_kernel,
        grid=(2, nq, nw),
        in_specs=[pl.BlockSpec((nr, 1, bsz, cq), zmap),
                  pl.BlockSpec((conv_w.shape[0], cq), lambda d, q, w: (0, q)),
                  pl.BlockSpec((1, cq), lambda d, q, w: (0, q)),
                  pl.BlockSpec((1, 2, cq // MXU_N, MXU_N, MXU_N), lambda d, q, w: (d, 0, q, 0, 0)),
                  pl.BlockSpec((1, 2, cq), lambda d, q, w: (d, 0, q)),
                  pl.BlockSpec((1, 1, cq), lambda d, q, w: (d, 0, q)),
                  pl.BlockSpec((1, bsz, cq), lambda d, q, w: (d, 0, q))],
        out_specs=[pl.BlockSpec((1, nr, 1, bsz, cq), hmap),
                   pl.BlockSpec((1, bsz, cq), lambda d, q, w: (d, 0, q))],
        out_shape=[jax.ShapeDtypeStruct((2, nr, nw, bsz, c), F32),
                   jax.ShapeDtypeStruct((2, bsz, c), F32)],
        scratch_shapes=[pltpu.VMEM((nr, bsz, cq), F32), pltpu.VMEM((nr, bsz, cq), F32)],
        compiler_params=_cparams("arbitrary", "arbitrary", "arbitrary"),
        name="lru_fused",
    )(z4, conv_w, conv_b, gate_w, gate_b, sp_l, h0)


def _to_pairs(q, out_ref, lead=()):
    rows, c = q.shape
    npairs = rows // (2 * SUBLANE)
    lane = lax.broadcasted_iota(jnp.int32, (SUBLANE, LANE), 1)
    low = lane < RWKV_HEAD
    for p in range(npairs):
        ev = q[(2 * p) * SUBLANE:(2 * p + 1) * SUBLANE]
        od = q[(2 * p + 1) * SUBLANE:(2 * p + 2) * SUBLANE]
        pieces = []
        for j in range(c // LANE):
            e_j = ev[:, j * LANE:(j + 1) * LANE]
            o_j = od[:, j * LANE:(j + 1) * LANE]
            pieces.append(jnp.where(low, e_j, pltpu.roll(o_j, RWKV_HEAD, 1)))
            pieces.append(jnp.where(low, pltpu.roll(e_j, RWKV_HEAD, 1), o_j))
        x = jnp.concatenate(pieces, axis=0)
        out_ref[lead + (p,)] = x.T


def _group_sum(x, ones_ref):
    hi = x.astype(BF16)
    lo = (x - hi.astype(F32)).astype(BF16)
    cols = []
    for q in range(x.shape[1] // MXU_N):
        sl = slice(q * MXU_N, (q + 1) * MXU_N)
        cols.append(jnp.dot(hi[:, sl], ones_ref[...], preferred_element_type=F32)
                    + jnp.dot(lo[:, sl], ones_ref[...], preferred_element_type=F32))
    return jnp.concatenate(cols, axis=1)


def _prep_kernel(period, z_ref, zp_ref, zn_ref, mu_ref, w0_ref, wup_ref, a0_ref, aup_ref,
                 kk_ref, ka_ref, rk_ref, gup_ref, ones_ref,
                 r_o, v_o, kn_o, w_o, kd_o, b_o, bv_o, g_o):
    i = pl.program_id(0)
    rows, _ = z_ref.shape
    tt = rows // SUBLANE
    cw = w0_ref.shape[1]
    z = z_ref[...]
    has_prev = ((i * tt) % period != 0).astype(F32)
    has_next = (((i + 1) * tt) % period != 0).astype(F32)
    zprev = jnp.concatenate([zp_ref[...] * has_prev, z[:rows - SUBLANE]], axis=0)
    znext = jnp.concatenate([z[SUBLANE:], zn_ref[...] * has_next], axis=0)
    zs = z + mu_ref[0:1, :] * (zprev - z) + mu_ref[1:2, :] * (znext - z)
    r = zs[:, 0:cw]
    k = zs[:, cw:2 * cw]
    v = zs[:, 2 * cw:3 * cw]
    off = 3 * cw
    wd = jnp.tanh(zs[:, off:off + LANE])
    ad = zs[:, off + LANE:off + 2 * LANE]
    gd = jax.nn.sigmoid(zs[:, off + 2 * LANE:off + 2 * LANE + MXU_N])
    wl = jnp.dot(wd.astype(BF16), wup_ref[...], preferred_element_type=F32)
    al = jnp.dot(ad.astype(BF16), aup_ref[...], preferred_element_type=F32)
    g_o[...] = jnp.dot(gd.astype(BF16), gup_ref[...], preferred_element_type=F32)
    kk = k * kk_ref[...]
    nrm = jnp.sqrt(_group_sum(kk * kk, ones_ref))
    kn = kk / jnp.maximum(nrm, 1e-12)
    _to_pairs(r, r_o)
    _to_pairs(v, v_o)
    _to_pairs(kn, kn_o)
    ksum = jnp.zeros_like(k)
    for d in range(2):
        xw = w0_ref[d:d + 1, :] + wl[:, d * cw:(d + 1) * cw]
        decay = jnp.exp(-DECAY_SCALE * jax.nn.sigmoid(xw))
        a = jax.nn.sigmoid(a0_ref[d:d + 1, :] + al[:, d * cw:(d + 1) * cw])
        kd = k * (1.0 + (a - 1.0) * ka_ref[...])
        ksum = ksum + kd
        _to_pairs(decay, w_o, (d,))
        _to_pairs(kd, kd_o, (d,))
        _to_pairs(kn * a, b_o, (d,))
    bonus = _group_sum(r * ksum * rk_ref[...], ones_ref)
    bv_o[...] = bonus * v


def _rwkv_prep(z, period, zcols, mu, w0, wup, a0, aup, k_k, k_a, r_k, gup, ones, tt=32):
    rows_total = z.shape[0]
    t = rows_total // SUBLANE
    cw = w0.shape[1]
    tt = min(tt, period)
    assert t % tt == 0 and period % tt == 0 and tt % 2 == 0
    rows = tt * SUBLANE
    nblk = t // tt
    const = lambda shape: pl.BlockSpec(shape, lambda i: (0,) * len(shape))
    pair = pl.BlockSpec((tt // 2, LANE, LANE), lambda i: (i, 0, 0))
    pair_d = pl.BlockSpec((2, tt // 2, LANE, LANE), lambda i: (0, i, 0, 0))
    nat = pl.BlockSpec((rows, cw), lambda i: (i, 0))
    sh_pair = jax.ShapeDtypeStruct((t // 2, LANE, LANE), F32)
    sh_pair_d = jax.ShapeDtypeStruct((2, t // 2, LANE, LANE), F32)
    sh_nat = jax.ShapeDtypeStruct((rows_total, cw), F32)
    return pl.pallas_call(
        functools.partial(_prep_kernel, period),
        grid=(nblk,),
        in_specs=[pl.BlockSpec((rows, zcols), lambda i: (i, 0)),
                  pl.BlockSpec((SUBLANE, zcols), lambda i: (jnp.maximum(i * tt - 1, 0), 0)),
                  pl.BlockSpec((SUBLANE, zcols), lambda i: (jnp.minimum((i + 1) * tt, t - 1), 0)),
                  const(mu.shape), const(w0.shape), const(wup.shape), const(a0.shape),
                  const(aup.shape), const(k_k.shape), const(k_a.shape), const(r_k.shape),
                  const(gup.shape), const(ones.shape)],
        out_specs=[pair, pair, pair, pair_d, pair_d, pair_d, nat, nat],
        out_shape=[sh_pair, sh_pair, sh_pair, sh_pair_d, sh_pair_d, sh_pair_d, sh_nat, sh_nat],
        compiler_params=_cparams("parallel"),
        name="rwkv_prep",
    )(z, z, z, mu, w0, wup, a0, aup, k_k, k_a, r_k, gup, ones)


def _scan_kernel(reverse, r_ref, w_ref, k_ref, v_ref, kk_ref, b_ref, s0_ref, *out_refs):
    y_ref = out_refs[0] if len(out_refs) == 2 else None
    s_ref = out_refs[-1]
    n = RWKV_HEAD
    npairs = r_ref.shape[0]

    @pl.when(pl.program_id(0) == 0)
    def _():
        s_ref[...] = s0_ref[...]

    def removal(p, base):
        def body(kb, acc):
            k0 = pl.multiple_of(kb * KCH, KCH)
            for j in range(KCH):
                acc = acc + s_ref[k0 + j] * kk_ref[p, pl.ds(base + k0 + j, 1), :]
            return acc

        return lax.fori_loop(0, n // KCH, body, jnp.zeros((n, LANE), F32))

    def one_step(p, base, pn, basen, acc):
        sa = -acc
        vv = v_ref[p, pl.ds(base, n), :]

        def body(kb, carry):
            y, nacc = carry
            k0 = pl.multiple_of(kb * KCH, KCH)
            for j in range(KCH):
                row = pl.ds(base + k0 + j, 1)
                sk = (s_ref[k0 + j] * w_ref[0, p, row, :]
                      + (sa * b_ref[0, p, row, :] + vv * k_ref[0, p, row, :]))
                s_ref[k0 + j] = sk
                if y_ref is not None:
                    y = y + sk * r_ref[p, row, :]
                nacc = nacc + sk * kk_ref[pn, pl.ds(basen + k0 + j, 1), :]
            return y, nacc

        zero = jnp.zeros((n, LANE), F32)
        y, nacc = lax.fori_loop(0, n // KCH, body, (zero, zero))
        if y_ref is not None:
            y_ref[p, pl.ds(base, n), :] = y
        return nacc

    first, second = ((n, 0) if reverse else (0, n))
    p_first = (npairs - 1) if reverse else 0

    def pair_step(i, acc):
        p = (npairs - 1 - i) if reverse else i
        pn = jnp.clip(p + (-1 if reverse else 1), 0, npairs - 1)
        acc = one_step(p, first, p, second, acc)
        return one_step(p, second, pn, first, acc)

    lax.fori_loop(0, npairs, pair_step, removal(p_first, first))


def _rwkv_scan(reverse, r, w, k, v, kk, b, s0, with_y=True, tbp=32):
    np_total = r.shape[0]
    tbp = min(tbp, np_total)
    assert np_total % tbp == 0
    nblk = np_total // tbp
    d = 1 if reverse else 0
    imap = (lambda i: (nblk - 1 - i, 0, 0)) if reverse else (lambda i: (i, 0, 0))
    imap_d = (lambda i: (d, nblk - 1 - i, 0, 0)) if reverse else (lambda i: (d, i, 0, 0))
    sp = pl.BlockSpec((tbp, LANE, LANE), imap)
    sp_d = pl.BlockSpec((1, tbp, LANE, LANE), imap_d)
    st = pl.BlockSpec((RWKV_HEAD, RWKV_HEAD, LANE), lambda i: (0, 0, 0))
    return pl.pallas_call(
        functools.partial(_scan_kernel, reverse),
        grid=(nblk,),
        in_specs=[sp, sp_d, sp_d, sp, sp, sp_d, st],
        out_specs=([sp] if with_y else []) + [st],
        out_shape=([jax.ShapeDtypeStruct(r.shape, F32)] if with_y else [])
        + [jax.ShapeDtypeStruct((RWKV_HEAD, RWKV_HEAD, LANE), F32)],
        compiler_params=_cparams("arbitrary"),
        name="rwkv_scan",
    )(r, w, k, v, kk, b, s0)


def _from_pairs(y_ref, nat_ref):
    npairs = y_ref.shape[0]
    lane = lax.broadcasted_iota(jnp.int32, (SUBLANE, LANE), 1)
    low = lane < RWKV_HEAD
    for p in range(npairs):
        xt = y_ref[p].T
        for j in range(nat_ref.shape[1] // LANE):
            h0 = xt[(2 * j) * SUBLANE:(2 * j + 1) * SUBLANE]
            h1 = xt[(2 * j + 1) * SUBLANE:(2 * j + 2) * SUBLANE]
            ev = jnp.where(low, h0, pltpu.roll(h1, RWKV_HEAD, 1))
            od = jnp.where(low, pltpu.roll(h0, RWKV_HEAD, 1), h1)
            nat_ref[(2 * p) * SUBLANE:(2 * p + 1) * SUBLANE, j * LANE:(j + 1) * LANE] = ev
            nat_ref[(2 * p + 1) * SUBLANE:(2 * p + 2) * SUBLANE, j * LANE:(j + 1) * LANE] = od


def _merge_kernel(y0_ref, y1_ref, bv_ref, go_ref, h0_ref, h1_ref, zg_ref, zgate_ref, x_ref,
                  lnx_ref, wa_ref, wb_ref, wo_ref, gt1_ref, gpost_ref, gpre_ref, scp_ref, sh_ref,
                  h_o, u_o, ub_o, yn_ref, nat_ref):
    n = RWKV_HEAD
    d = wa_ref.shape[1]
    npairs = y0_ref.shape[0]
    tt, bsz, _ = x_ref.shape
    for p in range(npairs):
        y = y0_ref[p] + y1_ref[p]
        for t2 in range(2):
            yh = y[t2 * n:(t2 + 1) * n]
            mu = jnp.mean(yh, axis=0, keepdims=True)
            var = jnp.mean(jnp.square(yh - mu), axis=0, keepdims=True)
            yn_ref[p, t2 * n:(t2 + 1) * n, :] = (yh - mu) * lax.rsqrt(var + LNX_EPS)
    _from_pairs(yn_ref, nat_ref)
    y_a = (nat_ref[...] * lnx_ref[0:1, :] + lnx_ref[1:2, :] + bv_ref[...]) * go_ref[...]
    pa = jnp.dot(y_a.astype(BF16), wa_ref[...], preferred_element_type=F32)
    y_b = jax.nn.gelu(zg_ref[...]) * (h0_ref[0] + h1_ref[0])
    pb = jnp.dot(y_b.astype(BF16), wb_ref[...], preferred_element_type=F32)
    gates = jax.nn.sigmoid(zgate_ref[...])
    mixed = gates[:, :d] * pa + gates[:, d:] * pb
    yo = jnp.dot(mixed.astype(BF16), wo_ref[...], preferred_element_type=F32).reshape(tt, bsz, d)
    yo = yo * lax.rsqrt(jnp.mean(yo * yo, axis=-1, keepdims=True) + NORM_EPS) * gpost_ref[...]
    h = x_ref[...] + gt1_ref[...] * yo
    h_o[...] = h
    u = h * lax.rsqrt(jnp.mean(h * h, axis=-1, keepdims=True) + NORM_EPS) * gpre_ref[...]
    u = (u * scp_ref[...] + sh_ref[...]).reshape(tt * bsz, d)
    u_o[...] = u
    ub_o[...] = u.astype(BF16)


def _merge(y0, y1, bv, g_out, h_lru, z, x_tm, lnx, wa, wb, wo, gt1, g_post, g_pre, scp, sh, tt=32):
    t, bsz, d = x_tm.shape
    rows = tt * bsz
    c = wa.shape[0]
    zcol = lambda blk, width: pl.BlockSpec((rows, width), lambda i: (i, blk))
    const = lambda shape: pl.BlockSpec(shape, lambda i: (0,) * len(shape))
    pair = pl.BlockSpec((tt // 2, LANE, LANE), lambda i: (i, 0, 0))
    nat = pl.BlockSpec((rows, c), lambda i: (i, 0))
    lg_blk = (3 * c + c + c) // c
    gate_blk = (lg_blk + 1) * c // (2 * d)
    return pl.pallas_call(
        _merge_kernel,
        grid=(t // tt,),
        in_specs=[pair, pair, nat, nat,
                  pl.BlockSpec((1, rows, c), lambda i: (0, i, 0)),
                  pl.BlockSpec((1, rows, c), lambda i: (1, i, 0)),
                  zcol(lg_blk, c), zcol(gate_blk, 2 * d),
                  pl.BlockSpec((tt, bsz, d), lambda i: (i, 0, 0)),
                  const(lnx.shape), const(wa.shape), const(wb.shape), const(wo.shape),
                  const(gt1.shape), const(g_post.shape), const(g_pre.shape),
                  const(scp.shape), const(sh.shape)],
        out_specs=[pl.BlockSpec((tt, bsz, d), lambda i: (i, 0, 0)),
                   pl.BlockSpec((rows, d), lambda i: (i, 0)),
                   pl.BlockSpec((rows, d), lambda i: (i, 0))],
        out_shape=[jax.ShapeDtypeStruct((t, bsz, d), F32),
                   jax.ShapeDtypeStruct((t * bsz, d), F32),
                   jax.ShapeDtypeStruct((t * bsz, d), BF16)],
        scratch_shapes=[pltpu.VMEM((tt // 2, LANE, LANE), F32), pltpu.VMEM((rows, c), F32)],
        compiler_params=_cparams("parallel"),
        name="merge",
    )(y0, y1, bv, g_out, h_lru, h_lru, z, z, x_tm, lnx, wa, wb, wo, gt1, g_post, g_pre, scp, sh)


def _experts_kernel(has_prev, blk_ref, eid_ref, lo_ref, hi_ref, x_ref, wgu_ref, wdn_ref, *rest):
    o_ref, wgu_b, wdn_b = rest[1:] if has_prev else rest
    i = pl.program_id(0)
    prev = jnp.maximum(i - 1, 0)
    ff = wdn_ref.shape[1]

    @pl.when(jnp.logical_or(i == 0, eid_ref[i] != eid_ref[prev]))
    def _():
        wgu_b[...] = wgu_ref[0].astype(BF16)
        wdn_b[...] = wdn_ref[0].astype(BF16)

    @pl.when(hi_ref[i] > lo_ref[i])
    def _():
        gu = jnp.dot(x_ref[...], wgu_b[...], preferred_element_type=F32)
        g = gu[:, :ff]
        u = gu[:, ff:]
        act = (g * jax.nn.sigmoid(g) * u).astype(BF16)
        res = jnp.dot(act, wdn_b[...], preferred_element_type=F32)
        row = lax.broadcasted_iota(jnp.int32, (res.shape[0], 1), 0)
        mine = jnp.logical_and(row >= lo_ref[i], row < hi_ref[i])
        first = jnp.logical_or(i == 0, blk_ref[i] != blk_ref[prev])

        @pl.when(first)
        def _():
            o_ref[...] = jnp.where(mine, res, 0.0).astype(o_ref.dtype)

        @pl.when(jnp.logical_not(first))
        def _():
            o_ref[...] = jnp.where(mine, res, o_ref[...].astype(F32)).astype(o_ref.dtype)


def _grouped_experts(xs, items, w_gu, w_down, n_rows, blk_off, prev_out=None):
    d = xs.shape[1]
    _, _, ff2 = w_gu.shape
    ff = ff2 // 2
    n_items = items[0].shape[0]
    in_specs = [pl.BlockSpec((MOE_ROWS, d), lambda i, blk, eid, lo, hi: (blk[i], 0)),
                pl.BlockSpec((1, d, ff2), lambda i, blk, eid, lo, hi: (eid[i], 0, 0)),
                pl.BlockSpec((1, ff, d), lambda i, blk, eid, lo, hi: (eid[i], 0, 0))]
    args = [*items, xs, w_gu, w_down]
    aliases = {}
    if prev_out is not None:
        in_specs.append(pl.BlockSpec(memory_space=pl.ANY))
        aliases = {len(args): 0}
        args.append(prev_out)
    grid_spec = pltpu.PrefetchScalarGridSpec(
        num_scalar_prefetch=4,
        grid=(n_items,),
        in_specs=in_specs,
        out_specs=pl.BlockSpec((MOE_ROWS, d), lambda i, blk, eid, lo, hi: (blk[i] + blk_off, 0)),
        scratch_shapes=[pltpu.VMEM((d, ff2), BF16), pltpu.VMEM((ff, d), BF16)],
    )
    return pl.pallas_call(
        functools.partial(_experts_kernel, prev_out is not None),
        grid_spec=grid_spec,
        out_shape=jax.ShapeDtypeStruct((n_rows, d), BF16),
        input_output_aliases=aliases,
        compiler_params=_cparams("arbitrary"),
        name="moe_experts",
    )(*args)


def _expert_items(start, end, r0, r1):
    n_e = start.shape[0]
    n_blocks = (r1 - r0) // MOE_ROWS
    n_items = n_blocks + n_e
    start = jnp.clip(start, r0, r1) - r0
    end = jnp.clip(end, r0, r1) - r0
    first_blk = start // MOE_ROWS
    per_e = jnp.where(end > start, (end - 1) // MOE_ROWS - first_blk + 1, 0)
    item_end = jnp.cumsum(per_e)
    w = jnp.arange(n_items, dtype=jnp.int32)
    eid = jnp.minimum(jnp.sum((item_end[None, :] <= w[:, None]).astype(jnp.int32), axis=1), n_e - 1)
    onehot = (eid[:, None] == jnp.arange(n_e)[None, :]).astype(jnp.int32)
    pick = lambda v: jnp.sum(onehot * v[None, :], axis=1)
    valid = w < item_end[-1]
    blk = jnp.where(valid, pick(first_blk) + w - (pick(item_end) - pick(per_e)), n_blocks - 1)
    lo = jnp.clip(pick(start) - blk * MOE_ROWS, 0, MOE_ROWS)
    hi = jnp.where(valid, jnp.clip(pick(end) - blk * MOE_ROWS, 0, MOE_ROWS), lo)
    i32 = lambda v: v.astype(jnp.int32)
    return i32(blk), i32(eid), i32(lo), i32(hi)


def _argmax_pick(cur, iota, big, axis):
    mx = jnp.max(cur, axis=axis, keepdims=True)
    ix = jnp.min(jnp.where(cur == mx, iota, big), axis=axis, keepdims=True)
    return iota == ix


def _router_kernel(u_ref, wt_ref, bias_ref, tri_ref, gate_o, idx_o, rank_o, cnt_o):
    n_e = wt_ref.shape[0]
    tr = u_ref.shape[0]
    per_g = n_e // N_GROUPS
    neg = -jnp.inf

    @pl.when(pl.program_id(0) == 0)
    def _():
        cnt_o[...] = jnp.zeros_like(cnt_o)

    logits = lax.dot_general(wt_ref[...], u_ref[...], (((1,), (1,)), ((), ())),
                             precision=HP, preferred_element_type=F32)
    scores = jax.nn.sigmoid(logits)
    biased = scores + bias_ref[...]
    b3 = biased.reshape(N_GROUPS, per_g, tr)
    ie = lax.broadcasted_iota(jnp.int32, b3.shape, 1)
    p1 = _argmax_pick(b3, ie, per_g, 1)
    m1 = jnp.max(b3, axis=1)
    m2 = jnp.max(jnp.where(p1, neg, b3), axis=1)
    cur = m1 + m2
    ig = lax.broadcasted_iota(jnp.int32, cur.shape, 0)
    sel_g = jnp.zeros(cur.shape, F32)
    for _ in range(TOPK_GROUPS):
        pick = _argmax_pick(cur, ig, N_GROUPS, 0)
        sel_g = jnp.where(pick, 1.0, sel_g)
        cur = jnp.where(pick, neg, cur)
    cur = jnp.where(sel_g[:, None, :] > 0.5, b3, neg).reshape(n_e, tr)
    i_e = lax.broadcasted_iota(jnp.int32, cur.shape, 0)
    i_ef = i_e.astype(F32)
    picks, top_s = [], []
    sel_f = jnp.zeros(cur.shape, F32)
    for _ in range(TOP_K):
        pick = _argmax_pick(cur, i_e, n_e, 0)
        picks.append(pick)
        top_s.append(jnp.sum(jnp.where(pick, scores, 0.0), axis=0, keepdims=True))
        sel_f = jnp.where(pick, 1.0, sel_f)
        cur = jnp.where(pick, neg, cur)
    denom = top_s[0]
    for s in top_s[1:]:
        denom = denom + s
    rank = jnp.dot(sel_f.astype(BF16), tri_ref[...], preferred_element_type=F32) + cnt_o[:, 0:1]
    cnt_o[...] = cnt_o[...] + jnp.sum(sel_f, axis=1, keepdims=True)
    zrow = jnp.zeros((SUBLANE - TOP_K, tr), F32)
    gate_o[...] = jnp.concatenate([s / denom * ROUTED_SCALE for s in top_s] + [zrow], axis=0)
    idx_rows = [jnp.sum(jnp.where(p, i_ef, 0.0), axis=0, keepdims=True) for p in picks]
    idx_o[...] = jnp.concatenate(idx_rows + [zrow], axis=0).astype(jnp.int32)
    rank_rows = [jnp.sum(jnp.where(p, rank, 0.0), axis=0, keepdims=True) for p in picks]
    rank_o[...] = jnp.concatenate(rank_rows + [zrow], axis=0).astype(jnp.int32)


def _router(u, w_t, bias, tr=512):
    m, d = u.shape
    n_e = w_t.shape[0]
    tri = (jnp.arange(tr)[:, None] < jnp.arange(tr)[None, :]).astype(BF16)
    tok = pl.BlockSpec((SUBLANE, tr), lambda i: (0, i))
    return pl.pallas_call(
        _router_kernel,
        grid=(m // tr,),
        in_specs=[pl.BlockSpec((tr, d), lambda i: (i, 0)),
                  pl.BlockSpec((n_e, d), lambda i: (0, 0)),
                  pl.BlockSpec((n_e, 1), lambda i: (0, 0)),
                  pl.BlockSpec((tr, tr), lambda i: (0, 0))],
        out_specs=[tok, tok, tok, pl.BlockSpec((n_e, LANE), lambda i: (0, 0))],
        out_shape=[jax.ShapeDtypeStruct((SUBLANE, m), F32),
                   jax.ShapeDtypeStruct((SUBLANE, m), jnp.int32),
                   jax.ShapeDtypeStruct((SUBLANE, m), jnp.int32),
                   jax.ShapeDtypeStruct((n_e, LANE), F32)],
        compiler_params=_cparams("arbitrary"),
        name="router",
    )(u, w_t, bias, tri)


def _final_kernel(*refs):
    ye_refs = refs[:TOP_K]
    gate_ref, ub_ref, h_ref, wgu_ref, wdn_ref, gt_ref, g_ref, o_ref = refs[TOP_K:]
    tt, bsz, d = h_ref.shape
    ff = wdn_ref.shape[0]
    routed = jnp.zeros((tt * bsz, d), F32)
    for j in range(TOP_K):
        routed = routed + gate_ref[:, j:j + 1] * ye_refs[j][0].astype(F32)
    gu = jnp.dot(ub_ref[...], wgu_ref[...], preferred_element_type=F32)
    g = gu[:, :ff]
    act = (g * jax.nn.sigmoid(g) * gu[:, ff:]).astype(BF16)
    moe = (routed + jnp.dot(act, wdn_ref[...], preferred_element_type=F32)).reshape(tt, bsz, d)
    moe = moe * lax.rsqrt(jnp.mean(moe * moe, axis=-1, keepdims=True) + NORM_EPS) * g_ref[...]
    o_ref[...] = h_ref[...] + gt_ref[...] * moe


def _final(ye, gate, ub, h, wgu, wdn, gt, g, part, tt=64):
    t, bsz, d = h.shape
    rows = tt * bsz
    nblk = t // tt // 2
    off = part * nblk
    const = lambda shape: pl.BlockSpec(shape, lambda i: (0,) * len(shape))
    return pl.pallas_call(
        _final_kernel,
        grid=(nblk,),
        in_specs=[pl.BlockSpec((1, rows, d), functools.partial(lambda j, i: (j, i, 0), j))
                  for j in range(TOP_K)] + [
                  pl.BlockSpec((rows, SUBLANE), lambda i: (i + off, 0)),
                  pl.BlockSpec((rows, d), lambda i: (i + off, 0)),
                  pl.BlockSpec((tt, bsz, d), lambda i: (i + off, 0, 0)),
                  const(wgu.shape), const(wdn.shape), const(gt.shape), const(g.shape)],
        out_specs=pl.BlockSpec((tt, bsz, d), lambda i: (i, 0, 0)),
        out_shape=jax.ShapeDtypeStruct((t // 2, bsz, d), F32),
        compiler_params=_cparams("parallel"),
        name="moe_final",
    )(*([ye] * TOP_K), gate, ub, h, wgu, wdn, gt, g)


def kernel(x, c, ctx, c_ctx, w_mod, b_mod, norm_g, w_in, shift_mu, rw_w0, rw_w_up, rw_a0, rw_a_up,
           rw_g_up, rw_k_k, rw_k_a, rw_r_k, rw_lnx, lru_conv_w, lru_conv_b, lru_gate_w, lru_gate_b,
           lru_l, w_branch_a, w_branch_b, w_out, router_w, router_b, ex_w_gu, ex_w_down,
           sh_w_gu, sh_w_down):
    l = 0
    bsz, t, d = x.shape
    tc = ctx.shape[1]
    rows_g = t // GRID_W
    m = bsz * t
    cw = rw_w0.shape[-1]
    cl = lru_l.shape[-1]
    nh = cw // RWKV_HEAD
    assert bsz == SUBLANE and cw == cl == d and 2 * nh * bsz // 2 == LANE
    rwkv_cols = 3 * cw + 2 * DECAY_LORA + 2 * AAA_LORA + GATE_LORA
    lora_cols = rwkv_cols - 3 * cw

    mod = jnp.dot(jax.nn.silu(c), w_mod[l], precision=HP) + b_mod[l]
    mod_c = jnp.dot(jax.nn.silu(c_ctx), w_mod[l], precision=HP) + b_mod[l]
    sh1, sc1, gt1, sh2, sc2, gt2 = jnp.split(mod, 6, axis=-1)
    csh1 = jnp.broadcast_to(mod_c[0:d], (bsz, d))
    csc1 = jnp.broadcast_to(mod_c[d:2 * d], (bsz, d))
    g_pre1, g_post1, g_pre2, g_post2 = [g[None, :] for g in norm_g[l]]

    wi = w_in[l]
    w_al = jnp.concatenate([wi[:, :rwkv_cols], jnp.zeros((d, cw - lora_cols), F32),
                            wi[:, rwkv_cols:]], axis=1).astype(BF16)
    zr_w = 4 * cw
    mu_al = jnp.concatenate([shift_mu[l], jnp.zeros((2, cw - lora_cols), F32)], axis=1)

    x_tm = jnp.transpose(x, (1, 0, 2))
    ctx_tm = jnp.transpose(ctx, (1, 0, 2))
    zc = _inproj(ctx_tm, 1.0 + csc1, csh1, g_pre1, w_al[:, :zr_w + cl])
    z = _inproj(x_tm, 1.0 + sc1, sh1, g_pre1, w_al, tn=2 * cw)

    blk = cl // LRU_BLOCKS
    per = MXU_N // blk
    gw = lru_gate_w[l].reshape(2, 2, LRU_BLOCKS // per, per, blk, blk)
    eye = jnp.eye(per, dtype=F32)
    gw_bd = jnp.einsum('dgqpjk,pr->dgqpjrk', gw, eye).reshape(2, 2, LRU_BLOCKS // per, MXU_N, MXU_N)
    gw_bd = gw_bd.astype(BF16)
    gb = lru_gate_b[l].reshape(2, 2, cl)
    sp_l = jax.nn.softplus(-lru_l[l])[:, None, :]
    lru_args = (lru_conv_w[l], lru_conv_b[l][None, :], gw_bd, gb, sp_l)
    _, hfin = _lru(zc.reshape(tc, 1, bsz, -1), 4, *lru_args, jnp.zeros((2, bsz, cl), F32))
    h_lru, _ = _lru(z.reshape(rows_g, GRID_W, bsz, -1), 4, *lru_args, hfin)
    h_lru = h_lru.reshape(2, m, cl)

    zeros_c = jnp.zeros((DECAY_LORA, cw), F32)
    wup2 = jnp.concatenate([jnp.concatenate([rw_w_up[l, 0], zeros_c], 1),
                            jnp.concatenate([zeros_c, rw_w_up[l, 1]], 1)], axis=0)
    aup2 = jnp.concatenate([jnp.concatenate([rw_a_up[l, 0], zeros_c], 1),
                            jnp.concatenate([zeros_c, rw_a_up[l, 1]], 1)], axis=0)
    gup = jnp.concatenate([rw_g_up[l], jnp.zeros((MXU_N - GATE_LORA, cw), F32)], axis=0)
    hid = jnp.arange(MXU_N) // RWKV_HEAD
    ones_blk = (hid[:, None] == hid[None, :]).astype(BF16)
    prep_args = (mu_al, rw_w0[l], wup2.astype(BF16), rw_a0[l], aup2.astype(BF16),
                 rw_k_k[l][None, :], rw_k_a[l][None, :], rw_r_k[l].reshape(1, cw),
                 gup.astype(BF16), ones_blk)
    pc = _rwkv_prep(zc, tc, zr_w, *prep_args)
    pz = _rwkv_prep(z, GRID_W, zr_w, *prep_args)
    s_zero = jnp.zeros((RWKV_HEAD, RWKV_HEAD, LANE), F32)
    ys = []
    for rev in (False, True):
        (s_c,) = _rwkv_scan(rev, pc[0], pc[3], pc[4], pc[1], pc[2], pc[5], s_zero, with_y=False)
        y_d, _ = _rwkv_scan(rev, pz[0], pz[3], pz[4], pz[1], pz[2], pz[5], s_c)
        ys.append(y_d)

    h, u2, u2b = _merge(ys[0], ys[1], pz[6], pz[7], h_lru, z, x_tm, rw_lnx[l],
                        w_branch_a[l].astype(BF16), w_branch_b[l].astype(BF16), w_out[l].astype(BF16),
                        gt1, g_post1, g_pre2, 1.0 + sc2, sh2)

    n_e = router_w.shape[-1]
    gate6, idx6, rank6, cnt = _router(u2, router_w[l].T, router_b[l][:, None])
    n_assign = m * TOP_K
    counts = cnt[:, 0].astype(jnp.int32)
    end = jnp.cumsum(counts)
    start = end - counts
    flat_e = idx6[:TOP_K].T.reshape(-1)
    sorted_tok = jnp.argsort(flat_e).astype(jnp.int32) // TOP_K
    half = n_assign // 2
    assert half % MOE_ROWS == 0 and t % 2 == 0
    ys_e = None
    for r0, r1 in ((0, half), (half, n_assign)):
        items = _expert_items(start, end, r0, r1)
        ys_e = _grouped_experts(u2b[sorted_tok[r0:r1]], items, ex_w_gu[l], ex_w_down[l],
                                n_assign, r0 // MOE_ROWS, ys_e)
    sel = idx6[:TOP_K, :, None] == jnp.arange(n_e, dtype=jnp.int32)[None, None, :]
    pos = jnp.sum(jnp.where(sel, start[None, None, :], 0), axis=-1) + rank6[:TOP_K]
    gate_t = gate6.T
    outs = []
    for a in range(2):
        ye = ys_e[pos[:, a * (m // 2):(a + 1) * (m // 2)].reshape(-1)].reshape(TOP_K, m // 2, d)
        outs.append(_final(ye, gate_t, u2b, h, sh_w_gu[l].astype(BF16), sh_w_down[l].astype(BF16),
                           gt2, g_post2, a))
    return jnp.transpose(jnp.concatenate(outs, axis=0), (1, 0, 2))
```

```python
import functools

import jax
import jax.numpy as jnp
from jax import lax
from jax.experimental import pallas as pl
from jax.experimental.pallas import tpu as pltpu

GRID_W = 64
NORM_EPS = 1e-6
RWKV_HEAD = 64
DECAY_LORA = 64
AAA_LORA = 64
GATE_LORA = 160
LNX_EPS = 64e-5
LRU_BLOCKS = 16
CONV_LEFT = 2
LRU_C = 8.0
N_GROUPS = 8
TOPK_GROUPS = 4
TOP_K = 6
ROUTED_SCALE = 2.5
DECAY_SCALE = 0.6065306597126334

LANE = 128
SUBLANE = 8
MXU_N = 256
MOE_ROWS = 512
KCH = 32
LRU_BLOCK_BYTES = 4 * 1024 * 1024
VMEM_LIMIT = 48 * 1024 * 1024
HP = lax.Precision.HIGHEST
F32 = jnp.float32
BF16 = jnp.bfloat16


def _cparams(*sem):
    return pltpu.CompilerParams(dimension_semantics=sem, vmem_limit_bytes=VMEM_LIMIT)


def _inproj_kernel(x_ref, scp_ref, sh_ref, g_ref, w_ref, o_ref, xm_ref):
    @pl.when(pl.program_id(1) == 0)
    def _():
        x = x_ref[...]
        ms = jnp.mean(x * x, axis=-1, keepdims=True)
        xm = x * lax.rsqrt(ms + NORM_EPS) * g_ref[...] * scp_ref[...] + sh_ref[...]
        xm_ref[...] = xm.reshape(xm_ref.shape).astype(BF16)

    o_ref[...] = jnp.dot(xm_ref[...], w_ref[...], preferred_element_type=F32)


def _inproj(x_tm, scp, sh, g, w, tt=128, tn=1024):
    t, bsz, d = x_tm.shape
    n = w.shape[1]
    tt = min(tt, t)
    assert t % tt == 0 and n % tn == 0
    return pl.pallas_call(
        _inproj_kernel,
        grid=(t // tt, n // tn),
        in_specs=[pl.BlockSpec((tt, bsz, d), lambda i, j: (i, 0, 0)),
                  pl.BlockSpec((bsz, d), lambda i, j: (0, 0)),
                  pl.BlockSpec((bsz, d), lambda i, j: (0, 0)),
                  pl.BlockSpec((1, d), lambda i, j: (0, 0)),
                  pl.BlockSpec((d, tn), lambda i, j: (0, j))],
        out_specs=pl.BlockSpec((tt * bsz, tn), lambda i, j: (i, j)),
        out_shape=jax.ShapeDtypeStruct((t * bsz, n), F32),
        scratch_shapes=[pltpu.VMEM((tt * bsz, d), BF16)],
        compiler_params=_cparams("parallel", "arbitrary"),
        name="in_proj",
    )(x_tm, scp, sh, g, w)


def _lru_kernel(z_ref, cw_ref, cb_ref, gw_ref, gb_ref, sp_ref, h0_ref, h_ref, hf_ref,
                a_ref, bx_ref):
    d = pl.program_id(0)
    w = pl.program_id(2)
    nr, _, bsz, cq = z_ref.shape
    taps = cw_ref.shape[0]

    @pl.when(w == 0)
    def _():
        hf_ref[0] = h0_ref[0]

    zx = z_ref[:, 0]
    xc = jnp.zeros_like(zx) + cb_ref[...]
    for j in range(taps):
        off = j - CONV_LEFT
        if off < 0:
            sh = jnp.concatenate([jnp.zeros((-off, bsz, cq), F32), zx[:nr + off]], axis=0)
        elif off > 0:
            sh = jnp.concatenate([zx[off:], jnp.zeros((off, bsz, cq), F32)], axis=0)
        else:
            sh = zx
        xc = xc + cw_ref[j:j + 1, :] * sh
    x2 = xc.reshape(nr * bsz, cq)
    xb = x2.astype(BF16)
    pre = []
    for g in range(2):
        cols = []
        for q in range(cq // MXU_N):
            cols.append(jnp.dot(xb[:, q * MXU_N:(q + 1) * MXU_N], gw_ref[0, g, q],
                                preferred_element_type=F32))
        pre.append(jnp.concatenate(cols, axis=1) + gb_ref[0, g:g + 1, :])
    rec = 0.5 * jnp.tanh(0.5 * pre[0]) + 0.5
    inp = 0.5 * jnp.tanh(0.5 * pre[1]) + 0.5
    log_a = -LRU_C * rec * sp_ref[0]
    a = jnp.exp(log_a)
    bx = jnp.sqrt(-jnp.tanh(log_a) * (a * a + 1.0)) * inp * x2
    a_ref[...] = a.reshape(nr, bsz, cq)
    bx_ref[...] = bx.reshape(nr, bsz, cq)

    def step(i, h):
        r = i + d * (nr - 1 - 2 * i)
        h = a_ref[r] * h + bx_ref[r]
        h_ref[0, r, 0] = h
        return h

    hf_ref[0] = lax.fori_loop(0, nr, step, hf_ref[0])


def _lru(z4, col_blk, conv_w, conv_b, gate_w, gate_b, sp_l, h0):
    nr, nw, bsz, _ = z4.shape
    c = conv_w.shape[1]
    cq = min(c, max(MXU_N, LRU_BLOCK_BYTES // (nr * bsz * 4)))
    nq = c // cq
    cb0 = col_blk * (c // cq)

    def zmap(d, q, w):
        return (0, w + d * (nw - 1 - 2 * w), 0, cb0 + q)

    def hmap(d, q, w):
        return (d, 0, w + d * (nw - 1 - 2 * w), 0, q)

    return pl.pallas_call(
        _lru_kernel,
        grid=(2, nq, nw),
        in_specs=[pl.BlockSpec((nr, 1, bsz, cq), zmap),
                  pl.BlockSpec((conv_w.shape[0], cq), lambda d, q, w: (0, q)),
                  pl.BlockSpec((1, cq), lambda d, q, w: (0, q)),
                  pl.BlockSpec((1, 2, cq // MXU_N, MXU_N, MXU_N), lambda d, q, w: (d, 0, q, 0, 0)),
                  pl.BlockSpec((1, 2, cq), lambda d, q, w: (d, 0, q)),
                  pl.BlockSpec((1, 1, cq), lambda d, q, w: (d, 0, q)),
                  pl.BlockSpec((1, bsz, cq), lambda d, q, w: (d, 0, q))],
        out_specs=[pl.BlockSpec((1, nr, 1, bsz, cq), hmap),
                   pl.BlockSpec((1, bsz, cq), lambda d, q, w: (d, 0, q))],
        out_shape=[jax.ShapeDtypeStruct((2, nr, nw, bsz, c), F32),
                   jax.ShapeDtypeStruct((2, bsz, c), F32)],
        scratch_shapes=[pltpu.VMEM((nr, bsz, cq), F32), pltpu.VMEM((nr, bsz, cq), F32)],
        compiler_params=_cparams("arbitrary", "arbitrary", "arbitrary"),
        name="lru_fused",
    )(z4, conv_w, conv_b, gate_w, gate_b, sp_l, h0)


def _to_pairs(q, out_ref, lead=()):
    rows, c = q.shape
    npairs = rows // (2 * SUBLANE)
    lane = lax.broadcasted_iota(jnp.int32, (SUBLANE, LANE), 1)
    low = lane < RWKV_HEAD
    for p in range(npairs):
        ev = q[(2 * p) * SUBLANE:(2 * p + 1) * SUBLANE]
        od = q[(2 * p + 1) * SUBLANE:(2 * p + 2) * SUBLANE]
        pieces = []
        for j in range(c // LANE):
            e_j = ev[:, j * LANE:(j + 1) * LANE]
            o_j = od[:, j * LANE:(j + 1) * LANE]
            pieces.append(jnp.where(low, e_j, pltpu.roll(o_j, RWKV_HEAD, 1)))
            pieces.append(jnp.where(low, pltpu.roll(e_j, RWKV_HEAD, 1), o_j))
        x = jnp.concatenate(pieces, axis=0)
        out_ref[lead + (p,)] = x.T


def _group_sum(x, ones_ref):
    hi = x.astype(BF16)
    lo = (x - hi.astype(F32)).astype(BF16)
    cols = []
    for q in range(x.shape[1] // MXU_N):
        sl = slice(q * MXU_N, (q + 1) * MXU_N)
        cols.append(jnp.dot(hi[:, sl], ones_ref[...], preferred_element_type=F32)
                    + jnp.dot(lo[:, sl], ones_ref[...], preferred_element_type=F32))
    return jnp.concatenate(cols, axis=1)


def _prep_kernel(period, z_ref, zp_ref, zn_ref, mu_ref, w0_ref, wup_ref, a0_ref, aup_ref,
                 kk_ref, ka_ref, rk_ref, gup_ref, ones_ref,
                 r_o, v_o, kn_o, w_o, kd_o, b_o, bv_o, g_o):
    i = pl.program_id(0)
    rows, _ = z_ref.shape
    tt = rows // SUBLANE
    cw = w0_ref.shape[1]
    z = z_ref[...]
    has_prev = ((i * tt) % period != 0).astype(F32)
    has_next = (((i + 1) * tt) % period != 0).astype(F32)
    zprev = jnp.concatenate([zp_ref[...] * has_prev, z[:rows - SUBLANE]], axis=0)
    znext = jnp.concatenate([z[SUBLANE:], zn_ref[...] * has_next], axis=0)
    zs = z + mu_ref[0:1, :] * (zprev - z) + mu_ref[1:2, :] * (znext - z)
    r = zs[:, 0:cw]
    k = zs[:, cw:2 * cw]
    v = zs[:, 2 * cw:3 * cw]
    off = 3 * cw
    wd = jnp.tanh(zs[:, off:off + LANE])
    ad = zs[:, off + LANE:off + 2 * LANE]
    gd = jax.nn.sigmoid(zs[:, off + 2 * LANE:off + 2 * LANE + MXU_N])
    wl = jnp.dot(wd.astype(BF16), wup_ref[...], preferred_element_type=F32)
    al = jnp.dot(ad.astype(BF16), aup_ref[...], preferred_element_type=F32)
    g_o[...] = jnp.dot(gd.astype(BF16), gup_ref[...], preferred_element_type=F32)
    kk = k * kk_ref[...]
    nrm = jnp.sqrt(_group_sum(kk * kk, ones_ref))
    kn = kk / jnp.maximum(nrm, 1e-12)
    _to_pairs(r, r_o)
    _to_pairs(v, v_o)
    _to_pairs(kn, kn_o)
    ksum = jnp.zeros_like(k)
    for d in range(2):
        xw = w0_ref[d:d + 1, :] + wl[:, d * cw:(d + 1) * cw]
        decay = jnp.exp(-DECAY_SCALE * jax.nn.sigmoid(xw))
        a = jax.nn.sigmoid(a0_ref[d:d + 1, :] + al[:, d * cw:(d + 1) * cw])
        kd = k * (1.0 + (a - 1.0) * ka_ref[...])
        ksum = ksum + kd
        _to_pairs(decay, w_o, (d,))
        _to_pairs(kd, kd_o, (d,))
        _to_pairs(kn * a, b_o, (d,))
    bonus = _group_sum(r * ksum * rk_ref[...], ones_ref)
    bv_o[...] = bonus * v


def _rwkv_prep(z, period, zcols, mu, w0, wup, a0, aup, k_k, k_a, r_k, gup, ones, tt=32):
    rows_total = z.shape[0]
    t = rows_total // SUBLANE
    cw = w0.shape[1]
    tt = min(tt, period)
    assert t % tt == 0 and period % tt == 0 and tt % 2 == 0
    rows = tt * SUBLANE
    nblk = t // tt
    const = lambda shape: pl.BlockSpec(shape, lambda i: (0,) * len(shape))
    pair = pl.BlockSpec((tt // 2, LANE, LANE), lambda i: (i, 0, 0))
    pair_d = pl.BlockSpec((2, tt // 2, LANE, LANE), lambda i: (0, i, 0, 0))
    nat = pl.BlockSpec((rows, cw), lambda i: (i, 0))
    sh_pair = jax.ShapeDtypeStruct((t // 2, LANE, LANE), F32)
    sh_pair_d = jax.ShapeDtypeStruct((2, t // 2, LANE, LANE), F32)
    sh_nat = jax.ShapeDtypeStruct((rows_total, cw), F32)
    return pl.pallas_call(
        functools.partial(_prep_kernel, period),
        grid=(nblk,),
        in_specs=[pl.BlockSpec((rows, zcols), lambda i: (i, 0)),
                  pl.BlockSpec((SUBLANE, zcols), lambda i: (jnp.maximum(i * tt - 1, 0), 0)),
                  pl.BlockSpec((SUBLANE, zcols), lambda i: (jnp.minimum((i + 1) * tt, t - 1), 0)),
                  const(mu.shape), const(w0.shape), const(wup.shape), const(a0.shape),
                  const(aup.shape), const(k_k.shape), const(k_a.shape), const(r_k.shape),
                  const(gup.shape), const(ones.shape)],
        out_specs=[pair, pair, pair, pair_d, pair_d, pair_d, nat, nat],
        out_shape=[sh_pair, sh_pair, sh_pair, sh_pair_d, sh_pair_d, sh_pair_d, sh_nat, sh_nat],
        compiler_params=_cparams("parallel"),
        name="rwkv_prep",
    )(z, z, z, mu, w0, wup, a0, aup, k_k, k_a, r_k, gup, ones)


def _scan_kernel(reverse, r_ref, w_ref, k_ref, v_ref, kk_ref, b_ref, s0_ref, *out_refs):
    y_ref = out_refs[0] if len(out_refs) == 2 else None
    s_ref = out_refs[-1]
    n = RWKV_HEAD
    npairs = r_ref.shape[0]

    @pl.when(pl.program_id(0) == 0)
    def _():
        s_ref[...] = s0_ref[...]

    def removal(p, base):
        def body(kb, acc):
            k0 = pl.multiple_of(kb * KCH, KCH)
            for j in range(KCH):
                acc = acc + s_ref[k0 + j] * kk_ref[p, pl.ds(base + k0 + j, 1), :]
            return acc

        return lax.fori_loop(0, n // KCH, body, jnp.zeros((n, LANE), F32))

    def one_step(p, base, pn, basen, acc):
        sa = -acc
        vv = v_ref[p, pl.ds(base, n), :]

        def body(kb, carry):
            y, nacc = carry
            k0 = pl.multiple_of(kb * KCH, KCH)
            for j in range(KCH):
                row = pl.ds(base + k0 + j, 1)
                sk = (s_ref[k0 + j] * w_ref[0, p, row, :]
                      + (sa * b_ref[0, p, row, :] + vv * k_ref[0, p, row, :]))
                s_ref[k0 + j] = sk
                if y_ref is not None:
                    y = y + sk * r_ref[p, row, :]
                nacc = nacc + sk * kk_ref[pn, pl.ds(basen + k0 + j, 1), :]
            return y, nacc

        zero = jnp.zeros((n, LANE), F32)
        y, nacc = lax.fori_loop(0, n // KCH, body, (zero, zero))
        if y_ref is not None:
            y_ref[p, pl.ds(base, n), :] = y
        return nacc

    first, second = ((n, 0) if reverse else (0, n))
    p_first = (npairs - 1) if reverse else 0

    def pair_step(i, acc):
        p = (npairs - 1 - i) if reverse else i
        pn = jnp.clip(p + (-1 if reverse else 1), 0, npairs - 1)
        acc = one_step(p, first, p, second, acc)
        return one_step(p, second, pn, first, acc)

    lax.fori_loop(0, npairs, pair_step, removal(p_first, first))


def _rwkv_scan(reverse, r, w, k, v, kk, b, s0, with_y=True, tbp=32):
    np_total = r.shape[0]
    tbp = min(tbp, np_total)
    assert np_total % tbp == 0
    nblk = np_total // tbp
    d = 1 if reverse else 0
    imap = (lambda i: (nblk - 1 - i, 0, 0)) if reverse else (lambda i: (i, 0, 0))
    imap_d = (lambda i: (d, nblk - 1 - i, 0, 0)) if reverse else (lambda i: (d, i, 0, 0))
    sp = pl.BlockSpec((tbp, LANE, LANE), imap)
    sp_d = pl.BlockSpec((1, tbp, LANE, LANE), imap_d)
    st = pl.BlockSpec((RWKV_HEAD, RWKV_HEAD, LANE), lambda i: (0, 0, 0))
    return pl.pallas_call(
        functools.partial(_scan_kernel, reverse),
        grid=(nblk,),
        in_specs=[sp, sp_d, sp_d, sp, sp, sp_d, st],
        out_specs=([sp] if with_y else []) + [st],
        out_shape=([jax.ShapeDtypeStruct(r.shape, F32)] if with_y else [])
        + [jax.ShapeDtypeStruct((RWKV_HEAD, RWKV_HEAD, LANE), F32)],
        compiler_params=_cparams("arbitrary"),
        name="rwkv_scan",
    )(r, w, k, v, kk, b, s0)


def _from_pairs(y_ref, nat_ref):
    npairs = y_ref.shape[0]
    lane = lax.broadcasted_iota(jnp.int32, (SUBLANE, LANE), 1)
    low = lane < RWKV_HEAD
    for p in range(npairs):
        xt = y_ref[p].T
        for j in range(nat_ref.shape[1] // LANE):
            h0 = xt[(2 * j) * SUBLANE:(2 * j + 1) * SUBLANE]
            h1 = xt[(2 * j + 1) * SUBLANE:(2 * j + 2) * SUBLANE]
            ev = jnp.where(low, h0, pltpu.roll(h1, RWKV_HEAD, 1))
            od = jnp.where(low, pltpu.roll(h0, RWKV_HEAD, 1), h1)
            nat_ref[(2 * p) * SUBLANE:(2 * p + 1) * SUBLANE, j * LANE:(j + 1) * LANE] = ev
            nat_ref[(2 * p + 1) * SUBLANE:(2 * p + 2) * SUBLANE, j * LANE:(j + 1) * LANE] = od


def _merge_kernel(y0_ref, y1_ref, bv_ref, go_ref, h0_ref, h1_ref, zg_ref, zgate_ref, x_ref,
                  lnx_ref, wa_ref, wb_ref, wo_ref, gt1_ref, gpost_ref, gpre_ref, scp_ref, sh_ref,
                  h_o, u_o, ub_o, yn_ref, nat_ref):
    n = RWKV_HEAD
    d = wa_ref.shape[1]
    npairs = y0_ref.shape[0]
    tt, bsz, _ = x_ref.shape
    for p in range(npairs):
        y = y0_ref[p] + y1_ref[p]
        for t2 in range(2):
            yh = y[t2 * n:(t2 + 1) * n]
            mu = jnp.mean(yh, axis=0, keepdims=True)
            var = jnp.mean(jnp.square(yh - mu), axis=0, keepdims=True)
            yn_ref[p, t2 * n:(t2 + 1) * n, :] = (yh - mu) * lax.rsqrt(var + LNX_EPS)
    _from_pairs(yn_ref, nat_ref)
    y_a = (nat_ref[...] * lnx_ref[0:1, :] + lnx_ref[1:2, :] + bv_ref[...]) * go_ref[...]
    pa = jnp.dot(y_a.astype(BF16), wa_ref[...], preferred_element_type=F32)
    y_b = jax.nn.gelu(zg_ref[...]) * (h0_ref[0] + h1_ref[0])
    pb = jnp.dot(y_b.astype(BF16), wb_ref[...], preferred_element_type=F32)
    gates = jax.nn.sigmoid(zgate_ref[...])
    mixed = gates[:, :d] * pa + gates[:, d:] * pb
    yo = jnp.dot(mixed.astype(BF16), wo_ref[...], preferred_element_type=F32).reshape(tt, bsz, d)
    yo = yo * lax.rsqrt(jnp.mean(yo * yo, axis=-1, keepdims=True) + NORM_EPS) * gpost_ref[...]
    h = x_ref[...] + gt1_ref[...] * yo
    h_o[...] = h
    u = h * lax.rsqrt(jnp.mean(h * h, axis=-1, keepdims=True) + NORM_EPS) * gpre_ref[...]
    u = (u * scp_ref[...] + sh_ref[...]).reshape(tt * bsz, d)
    u_o[...] = u
    ub_o[...] = u.astype(BF16)


def _merge(y0, y1, bv, g_out, h_lru, z, x_tm, lnx, wa, wb, wo, gt1, g_post, g_pre, scp, sh, tt=32):
    t, bsz, d = x_tm.shape
    rows = tt * bsz
    c = wa.shape[0]
    zcol = lambda blk, width: pl.BlockSpec((rows, width), lambda i: (i, blk))
    const = lambda shape: pl.BlockSpec(shape, lambda i: (0,) * len(shape))
    pair = pl.BlockSpec((tt // 2, LANE, LANE), lambda i: (i, 0, 0))
    nat = pl.BlockSpec((rows, c), lambda i: (i, 0))
    lg_blk = (3 * c + c + c) // c
    gate_blk = (lg_blk + 1) * c // (2 * d)
    return pl.pallas_call(
        _merge_kernel,
        grid=(t // tt,),
        in_specs=[pair, pair, nat, nat,
                  pl.BlockSpec((1, rows, c), lambda i: (0, i, 0)),
                  pl.BlockSpec((1, rows, c), lambda i: (1, i, 0)),
                  zcol(lg_blk, c), zcol(gate_blk, 2 * d),
                  pl.BlockSpec((tt, bsz, d), lambda i: (i, 0, 0)),
                  const(lnx.shape), const(wa.shape), const(wb.shape), const(wo.shape),
                  const(gt1.shape), const(g_post.shape), const(g_pre.shape),
                  const(scp.shape), const(sh.shape)],
        out_specs=[pl.BlockSpec((tt, bsz, d), lambda i: (i, 0, 0)),
                   pl.BlockSpec((rows, d), lambda i: (i, 0)),
                   pl.BlockSpec((rows, d), lambda i: (i, 0))],
        out_shape=[jax.ShapeDtypeStruct((t, bsz, d), F32),
                   jax.ShapeDtypeStruct((t * bsz, d), F32),
                   jax.ShapeDtypeStruct((t * bsz, d), BF16)],
        scratch_shapes=[pltpu.VMEM((tt // 2, LANE, LANE), F32), pltpu.VMEM((rows, c), F32)],
        compiler_params=_cparams("parallel"),
        name="merge",
    )(y0, y1, bv, g_out, h_lru, h_lru, z, z, x_tm, lnx, wa, wb, wo, gt1, g_post, g_pre, scp, sh)


def _experts_kernel(has_prev, blk_ref, eid_ref, lo_ref, hi_ref, x_ref, wgu_ref, wdn_ref, *rest):
    o_ref, wgu_b, wdn_b = rest[1:] if has_prev else rest
    i = pl.program_id(0)
    prev = jnp.maximum(i - 1, 0)
    ff = wdn_ref.shape[1]

    @pl.when(jnp.logical_or(i == 0, eid_ref[i] != eid_ref[prev]))
    def _():
        wgu_b[...] = wgu_ref[0].astype(BF16)
        wdn_b[...] = wdn_ref[0].astype(BF16)

    @pl.when(hi_ref[i] > lo_ref[i])
    def _():
        gu = jnp.dot(x_ref[...], wgu_b[...], preferred_element_type=F32)
        g = gu[:, :ff]
        u = gu[:, ff:]
        act = (g * jax.nn.sigmoid(g) * u).astype(BF16)
        res = jnp.dot(act, wdn_b[...], preferred_element_type=F32)
        row = lax.broadcasted_iota(jnp.int32, (res.shape[0], 1), 0)
        mine = jnp.logical_and(row >= lo_ref[i], row < hi_ref[i])
        first = jnp.logical_or(i == 0, blk_ref[i] != blk_ref[prev])

        @pl.when(first)
        def _():
            o_ref[...] = jnp.where(mine, res, 0.0).astype(o_ref.dtype)

        @pl.when(jnp.logical_not(first))
        def _():
            o_ref[...] = jnp.where(mine, res, o_ref[...].astype(F32)).astype(o_ref.dtype)


def _grouped_experts(xs, items, w_gu, w_down, n_rows, blk_off, prev_out=None):
    d = xs.shape[1]
    _, _, ff2 = w_gu.shape
    ff = ff2 // 2
    n_items = items[0].shape[0]
    in_specs = [pl.BlockSpec((MOE_ROWS, d), lambda i, blk, eid, lo, hi: (blk[i], 0)),
                pl.BlockSpec((1, d, ff2), lambda i, blk, eid, lo, hi: (eid[i], 0, 0)),
                pl.BlockSpec((1, ff, d), lambda i, blk, eid, lo, hi: (eid[i], 0, 0))]
    args = [*items, xs, w_gu, w_down]
    aliases = {}
    if prev_out is not None:
        in_specs.append(pl.BlockSpec(memory_space=pl.ANY))
        aliases = {len(args): 0}
        args.append(prev_out)
    grid_spec = pltpu.PrefetchScalarGridSpec(
        num_scalar_prefetch=4,
        grid=(n_items,),
        in_specs=in_specs,
        out_specs=pl.BlockSpec((MOE_ROWS, d), lambda i, blk, eid, lo, hi: (blk[i] + blk_off, 0)),
        scratch_shapes=[pltpu.VMEM((d, ff2), BF16), pltpu.VMEM((ff, d), BF16)],
    )
    return pl.pallas_call(
        functools.partial(_experts_kernel, prev_out is not None),
        grid_spec=grid_spec,
        out_shape=jax.ShapeDtypeStruct((n_rows, d), BF16),
        input_output_aliases=aliases,
        compiler_params=_cparams("arbitrary"),
        name="moe_experts",
    )(*args)


def _expert_items(start, end, r0, r1):
    n_e = start.shape[0]
    n_blocks = (r1 - r0) // MOE_ROWS
    n_items = n_blocks + n_e
    start = jnp.clip(start, r0, r1) - r0
    end = jnp.clip(end, r0, r1) - r0
    first_blk = start // MOE_ROWS
    per_e = jnp.where(end > start, (end - 1) // MOE_ROWS - first_blk + 1, 0)
    item_end = jnp.cumsum(per_e)
    w = jnp.arange(n_items, dtype=jnp.int32)
    eid = jnp.minimum(jnp.sum((item_end[None, :] <= w[:, None]).astype(jnp.int32), axis=1), n_e - 1)
    onehot = (eid[:, None] == jnp.arange(n_e)[None, :]).astype(jnp.int32)
    pick = lambda v: jnp.sum(onehot * v[None, :], axis=1)
    valid = w < item_end[-1]
    blk = jnp.where(valid, pick(first_blk) + w - (pick(item_end) - pick(per_e)), n_blocks - 1)
    lo = jnp.clip(pick(start) - blk * MOE_ROWS, 0, MOE_ROWS)
    hi = jnp.where(valid, jnp.clip(pick(end) - blk * MOE_ROWS, 0, MOE_ROWS), lo)
    i32 = lambda v: v.astype(jnp.int32)
    return i32(blk), i32(eid), i32(lo), i32(hi)


def _argmax_pick(cur, iota, big, axis):
    mx = jnp.max(cur, axis=axis, keepdims=True)
    ix = jnp.min(jnp.where(cur == mx, iota, big), axis=axis, keepdims=True)
    return iota == ix


def _router_kernel(u_ref, wt_ref, bias_ref, tri_ref, gate_o, idx_o, rank_o, cnt_o):
    n_e = wt_ref.shape[0]
    tr = u_ref.shape[0]
    per_g = n_e // N_GROUPS
    neg = -jnp.inf

    @pl.when(pl.program_id(0) == 0)
    def _():
        cnt_o[...] = jnp.zeros_like(cnt_o)

    logits = lax.dot_general(wt_ref[...], u_ref[...], (((1,), (1,)), ((), ())),
                             precision=HP, preferred_element_type=F32)
    scores = jax.nn.sigmoid(logits)
    biased = scores + bias_ref[...]
    b3 = biased.reshape(N_GROUPS, per_g, tr)
    ie = lax.broadcasted_iota(jnp.int32, b3.shape, 1)
    p1 = _argmax_pick(b3, ie, per_g, 1)
    m1 = jnp.max(b3, axis=1)
    m2 = jnp.max(jnp.where(p1, neg, b3), axis=1)
    cur = m1 + m2
    ig = lax.broadcasted_iota(jnp.int32, cur.shape, 0)
    sel_g = jnp.zeros(cur.shape, F32)
    for _ in range(TOPK_GROUPS):
        pick = _argmax_pick(cur, ig, N_GROUPS, 0)
        sel_g = jnp.where(pick, 1.0, sel_g)
        cur = jnp.where(pick, neg, cur)
    cur = jnp.where(sel_g[:, None, :] > 0.5, b3, neg).reshape(n_e, tr)
    i_e = lax.broadcasted_iota(jnp.int32, cur.shape, 0)
    i_ef = i_e.astype(F32)
    picks, top_s = [], []
    sel_f = jnp.zeros(cur.shape, F32)
    for _ in range(TOP_K):
        pick = _argmax_pick(cur, i_e, n_e, 0)
        picks.append(pick)
        top_s.append(jnp.sum(jnp.where(pick, scores, 0.0), axis=0, keepdims=True))
        sel_f = jnp.where(pick, 1.0, sel_f)
        cur = jnp.where(pick, neg, cur)
    denom = top_s[0]
    for s in top_s[1:]:
        denom = denom + s
    rank = jnp.dot(sel_f.astype(BF16), tri_ref[...], preferred_element_type=F32) + cnt_o[:, 0:1]
    cnt_o[...] = cnt_o[...] + jnp.sum(sel_f, axis=1, keepdims=True)
    zrow = jnp.zeros((SUBLANE - TOP_K, tr), F32)
    gate_o[...] = jnp.concatenate([s / denom * ROUTED_SCALE for s in top_s] + [zrow], axis=0)
    idx_rows = [jnp.sum(jnp.where(p, i_ef, 0.0), axis=0, keepdims=True) for p in picks]
    idx_o[...] = jnp.concatenate(idx_rows + [zrow], axis=0).astype(jnp.int32)
    rank_rows = [jnp.sum(jnp.where(p, rank, 0.0), axis=0, keepdims=True) for p in picks]
    rank_o[...] = jnp.concatenate(rank_rows + [zrow], axis=0).astype(jnp.int32)


def _router(u, w_t, bias, tr=512):
    m, d = u.shape
    n_e = w_t.shape[0]
    tri = (jnp.arange(tr)[:, None] < jnp.arange(tr)[None, :]).astype(BF16)
    tok = pl.BlockSpec((SUBLANE, tr), lambda i: (0, i))
    return pl.pallas_call(
        _router_kernel,
        grid=(m // tr,),
        in_specs=[pl.BlockSpec((tr, d), lambda i: (i, 0)),
                  pl.BlockSpec((n_e, d), lambda i: (0, 0)),
                  pl.BlockSpec((n_e, 1), lambda i: (0, 0)),
                  pl.BlockSpec((tr, tr), lambda i: (0, 0))],
        out_specs=[tok, tok, tok, pl.BlockSpec((n_e, LANE), lambda i: (0, 0))],
        out_shape=[jax.ShapeDtypeStruct((SUBLANE, m), F32),
                   jax.ShapeDtypeStruct((SUBLANE, m), jnp.int32),
                   jax.ShapeDtypeStruct((SUBLANE, m), jnp.int32),
                   jax.ShapeDtypeStruct((n_e, LANE), F32)],
        compiler_params=_cparams("arbitrary"),
        name="router",
    )(u, w_t, bias, tri)


def _final_kernel(*refs):
    ye_refs = refs[:TOP_K]
    gate_ref, ub_ref, h_ref, wgu_ref, wdn_ref, gt_ref, g_ref, o_ref, rl_ref = refs[TOP_K:]
    tt, bsz, d = h_ref.shape
    ff = wdn_ref.shape[0]
    routed = jnp.zeros((tt * bsz, d), F32)
    for j in range(TOP_K):
        routed = routed + gate_ref[:, j:j + 1] * ye_refs[j][0].astype(F32)
    gu = jnp.dot(ub_ref[...], wgu_ref[...], preferred_element_type=F32)
    g = gu[:, :ff]
    act = (g * jax.nn.sigmoid(g) * gu[:, ff:]).astype(BF16)
    moe = (routed + jnp.dot(act, wdn_ref[...], preferred_element_type=F32)).reshape(tt, bsz, d)
    moe = moe * lax.rsqrt(jnp.mean(moe * moe, axis=-1, keepdims=True) + NORM_EPS) * g_ref[...]
    out = (h_ref[...] + gt_ref[...] * moe).reshape(tt * bsz, d)
    for j in range(d // LANE):
        rl_ref[j] = out[:, j * LANE:(j + 1) * LANE]
    for b in range(bsz):
        for j in range(d // LANE):
            o_ref[b, :, j * LANE:(j + 1) * LANE] = rl_ref[j, pl.ds(b, tt, stride=bsz), :]


def _final(ye, gate, ub, h, wgu, wdn, gt, g, tt=64):
    t, bsz, d = h.shape
    rows = tt * bsz
    const = lambda shape: pl.BlockSpec(shape, lambda i: (0,) * len(shape))
    return pl.pallas_call(
        _final_kernel,
        grid=(t // tt,),
        in_specs=[pl.BlockSpec((1, rows, d), functools.partial(lambda j, i: (j, i, 0), j))
                  for j in range(TOP_K)] + [
                  pl.BlockSpec((rows, SUBLANE), lambda i: (i, 0)),
                  pl.BlockSpec((rows, d), lambda i: (i, 0)),
                  pl.BlockSpec((tt, bsz, d), lambda i: (i, 0, 0)),
                  const(wgu.shape), const(wdn.shape), const(gt.shape), const(g.shape)],
        out_specs=pl.BlockSpec((bsz, tt, d), lambda i: (0, i, 0)),
        out_shape=jax.ShapeDtypeStruct((bsz, t, d), F32),
        scratch_shapes=[pltpu.VMEM((d // LANE, rows, LANE), F32)],
        compiler_params=_cparams("parallel"),
        name="moe_final",
    )(*([ye] * TOP_K), gate, ub, h, wgu, wdn, gt, g)


def kernel(x, c, ctx, c_ctx, w_mod, b_mod, norm_g, w_in, shift_mu, rw_w0, rw_w_up, rw_a0, rw_a_up,
           rw_g_up, rw_k_k, rw_k_a, rw_r_k, rw_lnx, lru_conv_w, lru_conv_b, lru_gate_w, lru_gate_b,
           lru_l, w_branch_a, w_branch_b, w_out, router_w, router_b, ex_w_gu, ex_w_down,
           sh_w_gu, sh_w_down):
    l = 0
    bsz, t, d = x.shape
    tc = ctx.shape[1]
    rows_g = t // GRID_W
    m = bsz * t
    cw = rw_w0.shape[-1]
    cl = lru_l.shape[-1]
    nh = cw // RWKV_HEAD
    assert bsz == SUBLANE and cw == cl == d and 2 * nh * bsz // 2 == LANE
    rwkv_cols = 3 * cw + 2 * DECAY_LORA + 2 * AAA_LORA + GATE_LORA
    lora_cols = rwkv_cols - 3 * cw

    mod = jnp.dot(jax.nn.silu(c), w_mod[l], precision=HP) + b_mod[l]
    mod_c = jnp.dot(jax.nn.silu(c_ctx), w_mod[l], precision=HP) + b_mod[l]
    sh1, sc1, gt1, sh2, sc2, gt2 = jnp.split(mod, 6, axis=-1)
    csh1 = jnp.broadcast_to(mod_c[0:d], (bsz, d))
    csc1 = jnp.broadcast_to(mod_c[d:2 * d], (bsz, d))
    g_pre1, g_post1, g_pre2, g_post2 = [g[None, :] for g in norm_g[l]]

    wi = w_in[l]
    w_al = jnp.concatenate([wi[:, :rwkv_cols], jnp.zeros((d, cw - lora_cols), F32),
                            wi[:, rwkv_cols:]], axis=1).astype(BF16)
    zr_w = 4 * cw
    mu_al = jnp.concatenate([shift_mu[l], jnp.zeros((2, cw - lora_cols), F32)], axis=1)

    x_tm = jnp.transpose(x, (1, 0, 2))
    ctx_tm = jnp.transpose(ctx, (1, 0, 2))
    zc = _inproj(ctx_tm, 1.0 + csc1, csh1, g_pre1, w_al[:, :zr_w + cl])
    z = _inproj(x_tm, 1.0 + sc1, sh1, g_pre1, w_al, tn=2 * cw)

    blk = cl // LRU_BLOCKS
    per = MXU_N // blk
    gw = lru_gate_w[l].reshape(2, 2, LRU_BLOCKS // per, per, blk, blk)
    eye = jnp.eye(per, dtype=F32)
    gw_bd = jnp.einsum('dgqpjk,pr->dgqpjrk', gw, eye).reshape(2, 2, LRU_BLOCKS // per, MXU_N, MXU_N)
    gw_bd = gw_bd.astype(BF16)
    gb = lru_gate_b[l].reshape(2, 2, cl)
    sp_l = jax.nn.softplus(-lru_l[l])[:, None, :]
    lru_args = (lru_conv_w[l], lru_conv_b[l][None, :], gw_bd, gb, sp_l)
    _, hfin = _lru(zc.reshape(tc, 1, bsz, -1), 4, *lru_args, jnp.zeros((2, bsz, cl), F32))
    h_lru, _ = _lru(z.reshape(rows_g, GRID_W, bsz, -1), 4, *lru_args, hfin)
    h_lru = h_lru.reshape(2, m, cl)

    zeros_c = jnp.zeros((DECAY_LORA, cw), F32)
    wup2 = jnp.concatenate([jnp.concatenate([rw_w_up[l, 0], zeros_c], 1),
                            jnp.concatenate([zeros_c, rw_w_up[l, 1]], 1)], axis=0)
    aup2 = jnp.concatenate([jnp.concatenate([rw_a_up[l, 0], zeros_c], 1),
                            jnp.concatenate([zeros_c, rw_a_up[l, 1]], 1)], axis=0)
    gup = jnp.concatenate([rw_g_up[l], jnp.zeros((MXU_N - GATE_LORA, cw), F32)], axis=0)
    hid = jnp.arange(MXU_N) // RWKV_HEAD
    ones_blk = (hid[:, None] == hid[None, :]).astype(BF16)
    prep_args = (mu_al, rw_w0[l], wup2.astype(BF16), rw_a0[l], aup2.astype(BF16),
                 rw_k_k[l][None, :], rw_k_a[l][None, :], rw_r_k[l].reshape(1, cw),
                 gup.astype(BF16), ones_blk)
    pc = _rwkv_prep(zc, tc, zr_w, *prep_args)
    pz = _rwkv_prep(z, GRID_W, zr_w, *prep_args)
    s_zero = jnp.zeros((RWKV_HEAD, RWKV_HEAD, LANE), F32)
    ys = []
    for rev in (False, True):
        (s_c,) = _rwkv_scan(rev, pc[0], pc[3], pc[4], pc[1], pc[2], pc[5], s_zero, with_y=False)
        y_d, _ = _rwkv_scan(rev, pz[0], pz[3], pz[4], pz[1], pz[2], pz[5], s_c)
        ys.append(y_d)

    h, u2, u2b = _merge(ys[0], ys[1], pz[6], pz[7], h_lru, z, x_tm, rw_lnx[l],
                        w_branch_a[l].astype(BF16), w_branch_b[l].astype(BF16), w_out[l].astype(BF16),
                        gt1, g_post1, g_pre2, 1.0 + sc2, sh2)

    n_e = router_w.shape[-1]
    gate6, idx6, rank6, cnt = _router(u2, router_w[l].T, router_b[l][:, None])
    n_assign = m * TOP_K
    counts = cnt[:, 0].astype(jnp.int32)
    end = jnp.cumsum(counts)
    start = end - counts
    flat_e = idx6[:TOP_K].T.reshape(-1)
    sorted_tok = jnp.argsort(flat_e).astype(jnp.int32) // TOP_K
    half = n_assign // 2
    assert half % MOE_ROWS == 0 and t % 2 == 0
    ys_e = None
    for r0, r1 in ((0, half), (half, n_assign)):
        items = _expert_items(start, end, r0, r1)
        ys_e = _grouped_experts(u2b[sorted_tok[r0:r1]], items, ex_w_gu[l], ex_w_down[l],
                                n_assign, r0 // MOE_ROWS, ys_e)
    sel = idx6[:TOP_K, :, None] == jnp.arange(n_e, dtype=jnp.int32)[None, None, :]
    pos = jnp.sum(jnp.where(sel, start[None, None, :], 0), axis=-1) + rank6[:TOP_K]
    ye = ys_e[pos.reshape(-1)].reshape(TOP_K, m, d)
    return _final(ye, gate6.T, u2b, h, sh_w_gu[l].astype(BF16), sh_w_down[l].astype(BF16),
                  gt2, g_post2)
```

```python
import functools

import jax
import jax.numpy as jnp
from jax import lax
from jax.experimental import pallas as pl
from jax.experimental.pallas import tpu as pltpu

GRID_W = 64
NORM_EPS = 1e-6
RWKV_HEAD = 64
DECAY_LORA = 64
AAA_LORA = 64
GATE_LORA = 160
LNX_EPS = 64e-5
LRU_BLOCKS = 16
CONV_LEFT = 2
LRU_C = 8.0
N_GROUPS = 8
TOPK_GROUPS = 4
TOP_K = 6
ROUTED_SCALE = 2.5
DECAY_SCALE = 0.6065306597126334

LANE = 128
SUBLANE = 8
MXU_N = 256
MOE_ROWS = 512
KCH = 32
LRU_BLOCK_BYTES = 4 * 1024 * 1024
VMEM_LIMIT = 48 * 1024 * 1024
HP = lax.Precision.HIGHEST
F32 = jnp.float32
BF16 = jnp.bfloat16


def _cparams(*sem):
    return pltpu.CompilerParams(dimension_semantics=sem, vmem_limit_bytes=VMEM_LIMIT)


def _inproj_kernel(x_ref, scp_ref, sh_ref, g_ref, w_ref, o_ref, xm_ref):
    @pl.when(pl.program_id(1) == 0)
    def _():
        x = x_ref[...]
        ms = jnp.mean(x * x, axis=-1, keepdims=True)
        xm = x * lax.rsqrt(ms + NORM_EPS) * g_ref[...] * scp_ref[...] + sh_ref[...]
        xm_ref[...] = xm.reshape(xm_ref.shape).astype(BF16)

    o_ref[...] = jnp.dot(xm_ref[...], w_ref[...], preferred_element_type=F32)


def _inproj(x_tm, scp, sh, g, w, tt=128, tn=1024):
    t, bsz, d = x_tm.shape
    n = w.shape[1]
    tt = min(tt, t)
    assert t % tt == 0 and n % tn == 0
    return pl.pallas_call(
        _inproj_kernel,
        grid=(t // tt, n // tn),
        in_specs=[pl.BlockSpec((tt, bsz, d), lambda i, j: (i, 0, 0)),
                  pl.BlockSpec((bsz, d), lambda i, j: (0, 0)),
                  pl.BlockSpec((bsz, d), lambda i, j: (0, 0)),
                  pl.BlockSpec((1, d), lambda i, j: (0, 0)),
                  pl.BlockSpec((d, tn), lambda i, j: (0, j))],
        out_specs=pl.BlockSpec((tt * bsz, tn), lambda i, j: (i, j)),
        out_shape=jax.ShapeDtypeStruct((t * bsz, n), F32),
        scratch_shapes=[pltpu.VMEM((tt * bsz, d), BF16)],
        compiler_params=_cparams("parallel", "arbitrary"),
        name="in_proj",
    )(x_tm, scp, sh, g, w)


def _lru_kernel(z_ref, cw_ref, cb_ref, gw_ref, gb_ref, sp_ref, h0_ref, h_ref, hf_ref,
                a_ref, bx_ref):
    d = pl.program_id(0)
    w = pl.program_id(2)
    nr, _, bsz, cq = z_ref.shape
    taps = cw_ref.shape[0]

    @pl.when(w == 0)
    def _():
        hf_ref[0] = h0_ref[0]

    zx = z_ref[:, 0]
    xc = jnp.zeros_like(zx) + cb_ref[...]
    for j in range(taps):
        off = j - CONV_LEFT
        if off < 0:
            sh = jnp.concatenate([jnp.zeros((-off, bsz, cq), F32), zx[:nr + off]], axis=0)
        elif off > 0:
            sh = jnp.concatenate([zx[off:], jnp.zeros((off, bsz, cq), F32)], axis=0)
        else:
            sh = zx
        xc = xc + cw_ref[j:j + 1, :] * sh
    x2 = xc.reshape(nr * bsz, cq)
    xb = x2.astype(BF16)
    pre = []
    for g in range(2):
        cols = []
        for q in range(cq // MXU_N):
            cols.append(jnp.dot(xb[:, q * MXU_N:(q + 1) * MXU_N], gw_ref[0, g, q],
                                preferred_element_type=F32))
        pre.append(jnp.concatenate(cols, axis=1) + gb_ref[0, g:g + 1, :])
    rec = 0.5 * jnp.tanh(0.5 * pre[0]) + 0.5
    inp = 0.5 * jnp.tanh(0.5 * pre[1]) + 0.5
    log_a = -LRU_C * rec * sp_ref[0]
    a = jnp.exp(log_a)
    bx = jnp.sqrt(-jnp.tanh(log_a) * (a * a + 1.0)) * inp * x2
    a_ref[...] = a.reshape(nr, bsz, cq)
    bx_ref[...] = bx.reshape(nr, bsz, cq)

    def step(i, h):
        r = i + d * (nr - 1 - 2 * i)
        h = a_ref[r] * h + bx_ref[r]
        h_ref[0, r, 0] = h
        return h

    hf_ref[0] = lax.fori_loop(0, nr, step, hf_ref[0])


def _lru(z4, col_blk, conv_w, conv_b, gate_w, gate_b, sp_l, h0):
    nr, nw, bsz, _ = z4.shape
    c = conv_w.shape[1]
    cq = min(c, max(MXU_N, LRU_BLOCK_BYTES // (nr * bsz * 4)))
    nq = c // cq
    cb0 = col_blk * (c // cq)

    def zmap(d, q, w):
        return (0, w + d * (nw - 1 - 2 * w), 0, cb0 + q)

    def hmap(d, q, w):
        return (d, 0, w + d * (nw - 1 - 2 * w), 0, q)

    return pl.pallas_call(
        _lru_kernel,
        grid=(2, nq, nw),
        in_specs=[pl.BlockSpec((nr, 1, bsz, cq), zmap),
                  pl.BlockSpec((conv_w.shape[0], cq), lambda d, q, w: (0, q)),
                  pl.BlockSpec((1, cq), lambda d, q, w: (0, q)),
                  pl.BlockSpec((1, 2, cq // MXU_N, MXU_N, MXU_N), lambda d, q, w: (d, 0, q, 0, 0)),
                  pl.BlockSpec((1, 2, cq), lambda d, q, w: (d, 0, q)),
                  pl.BlockSpec((1, 1, cq), lambda d, q, w: (d, 0, q)),
                  pl.BlockSpec((1, bsz, cq), lambda d, q, w: (d, 0, q))],
        out_specs=[pl.BlockSpec((1, nr, 1, bsz, cq), hmap),
                   pl.BlockSpec((1, bsz, cq), lambda d, q, w: (d, 0, q))],
        out_shape=[jax.ShapeDtypeStruct((2, nr, nw, bsz, c), F32),
                   jax.ShapeDtypeStruct((2, bsz, c), F32)],
        scratch_shapes=[pltpu.VMEM((nr, bsz, cq), F32), pltpu.VMEM((nr, bsz, cq), F32)],
        compiler_params=_cparams("arbitrary", "arbitrary", "arbitrary"),
        name="lru_fused",
    )(z4, conv_w, conv_b, gate_w, gate_b, sp_l, h0)


def _to_pairs(q, out_ref, lead=()):
    rows, c = q.shape
    npairs = rows // (2 * SUBLANE)
    lane = lax.broadcasted_iota(jnp.int32, (SUBLANE, LANE), 1)
    low = lane < RWKV_HEAD
    for p in range(npairs):
        ev = q[(2 * p) * SUBLANE:(2 * p + 1) * SUBLANE]
        od = q[(2 * p + 1) * SUBLANE:(2 * p + 2) * SUBLANE]
        pieces = []
        for j in range(c // LANE):
            e_j = ev[:, j * LANE:(j + 1) * LANE]
            o_j = od[:, j * LANE:(j + 1) * LANE]
            pieces.append(jnp.where(low, e_j, pltpu.roll(o_j, RWKV_HEAD, 1)))
            pieces.append(jnp.where(low, pltpu.roll(e_j, RWKV_HEAD, 1), o_j))
        x = jnp.concatenate(pieces, axis=0)
        out_ref[lead + (p,)] = x.T


def _group_sum(x, ones_ref):
    hi = x.astype(BF16)
    lo = (x - hi.astype(F32)).astype(BF16)
    cols = []
    for q in range(x.shape[1] // MXU_N):
        sl = slice(q * MXU_N, (q + 1) * MXU_N)
        cols.append(jnp.dot(hi[:, sl], ones_ref[...], preferred_element_type=F32)
                    + jnp.dot(lo[:, sl], ones_ref[...], preferred_element_type=F32))
    return jnp.concatenate(cols, axis=1)


def _prep_kernel(period, z_ref, zp_ref, zn_ref, mu_ref, w0_ref, wup_ref, a0_ref, aup_ref,
                 kk_ref, ka_ref, rk_ref, gup_ref, ones_ref,
                 r_o, v_o, kn_o, w_o, kd_o, b_o, bv_o, g_o):
    i = pl.program_id(0)
    rows, _ = z_ref.shape
    tt = rows // SUBLANE
    cw = w0_ref.shape[1]
    z = z_ref[...]
    has_prev = ((i * tt) % period != 0).astype(F32)
    has_next = (((i + 1) * tt) % period != 0).astype(F32)
    zprev = jnp.concatenate([zp_ref[...] * has_prev, z[:rows - SUBLANE]], axis=0)
    znext = jnp.concatenate([z[SUBLANE:], zn_ref[...] * has_next], axis=0)
    zs = z + mu_ref[0:1, :] * (zprev - z) + mu_ref[1:2, :] * (znext - z)
    r = zs[:, 0:cw]
    k = zs[:, cw:2 * cw]
    v = zs[:, 2 * cw:3 * cw]
    off = 3 * cw
    wd = jnp.tanh(zs[:, off:off + LANE])
    ad = zs[:, off + LANE:off + 2 * LANE]
    gd = jax.nn.sigmoid(zs[:, off + 2 * LANE:off + 2 * LANE + MXU_N])
    wl = jnp.dot(wd.astype(BF16), wup_ref[...], preferred_element_type=F32)
    al = jnp.dot(ad.astype(BF16), aup_ref[...], preferred_element_type=F32)
    g_o[...] = jnp.dot(gd.astype(BF16), gup_ref[...], preferred_element_type=F32)
    kk = k * kk_ref[...]
    nrm = jnp.sqrt(_group_sum(kk * kk, ones_ref))
    kn = kk / jnp.maximum(nrm, 1e-12)
    _to_pairs(r, r_o)
    _to_pairs(v, v_o)
    _to_pairs(kn, kn_o)
    ksum = jnp.zeros_like(k)
    for d in range(2):
        xw = w0_ref[d:d + 1, :] + wl[:, d * cw:(d + 1) * cw]
        decay = jnp.exp(-DECAY_SCALE * jax.nn.sigmoid(xw))
        a = jax.nn.sigmoid(a0_ref[d:d + 1, :] + al[:, d * cw:(d + 1) * cw])
        kd = k * (1.0 + (a - 1.0) * ka_ref[...])
        ksum = ksum + kd
        _to_pairs(decay, w_o, (d,))
        _to_pairs(kd, kd_o, (d,))
        _to_pairs(kn * a, b_o, (d,))
    bonus = _group_sum(r * ksum * rk_ref[...], ones_ref)
    bv_o[...] = bonus * v


def _rwkv_prep(z, period, zcols, mu, w0, wup, a0, aup, k_k, k_a, r_k, gup, ones, tt=32):
    rows_total = z.shape[0]
    t = rows_total // SUBLANE
    cw = w0.shape[1]
    tt = min(tt, period)
    assert t % tt == 0 and period % tt == 0 and tt % 2 == 0
    rows = tt * SUBLANE
    nblk = t // tt
    const = lambda shape: pl.BlockSpec(shape, lambda i: (0,) * len(shape))
    pair = pl.BlockSpec((tt // 2, LANE, LANE), lambda i: (i, 0, 0))
    pair_d = pl.BlockSpec((2, tt // 2, LANE, LANE), lambda i: (0, i, 0, 0))
    nat = pl.BlockSpec((rows, cw), lambda i: (i, 0))
    sh_pair = jax.ShapeDtypeStruct((t // 2, LANE, LANE), F32)
    sh_pair_d = jax.ShapeDtypeStruct((2, t // 2, LANE, LANE), F32)
    sh_nat = jax.ShapeDtypeStruct((rows_total, cw), F32)
    return pl.pallas_call(
        functools.partial(_prep_kernel, period),
        grid=(nblk,),
        in_specs=[pl.BlockSpec((rows, zcols), lambda i: (i, 0)),
                  pl.BlockSpec((SUBLANE, zcols), lambda i: (jnp.maximum(i * tt - 1, 0), 0)),
                  pl.BlockSpec((SUBLANE, zcols), lambda i: (jnp.minimum((i + 1) * tt, t - 1), 0)),
                  const(mu.shape), const(w0.shape), const(wup.shape), const(a0.shape),
                  const(aup.shape), const(k_k.shape), const(k_a.shape), const(r_k.shape),
                  const(gup.shape), const(ones.shape)],
        out_specs=[pair, pair, pair, pair_d, pair_d, pair_d, nat, nat],
        out_shape=[sh_pair, sh_pair, sh_pair, sh_pair_d, sh_pair_d, sh_pair_d, sh_nat, sh_nat],
        compiler_params=_cparams("parallel"),
        name="rwkv_prep",
    )(z, z, z, mu, w0, wup, a0, aup, k_k, k_a, r_k, gup, ones)


def _scan_kernel(reverse, r_ref, w_ref, k_ref, v_ref, kk_ref, b_ref, s0_ref, *out_refs):
    y_ref = out_refs[0] if len(out_refs) == 2 else None
    s_ref = out_refs[-1]
    n = RWKV_HEAD
    npairs = r_ref.shape[0]

    @pl.when(pl.program_id(0) == 0)
    def _():
        s_ref[...] = s0_ref[...]

    def removal(p, base):
        def body(kb, acc):
            k0 = pl.multiple_of(kb * KCH, KCH)
            for j in range(KCH):
                acc = acc + s_ref[k0 + j] * kk_ref[p, pl.ds(base + k0 + j, 1), :]
            return acc

        return lax.fori_loop(0, n // KCH, body, jnp.zeros((n, LANE), F32))

    def one_step(p, base, pn, basen, acc):
        sa = -acc
        vv = v_ref[p, pl.ds(base, n), :]

        def body(kb, carry):
            y, nacc = carry
            k0 = pl.multiple_of(kb * KCH, KCH)
            for j in range(KCH):
                row = pl.ds(base + k0 + j, 1)
                sk = (s_ref[k0 + j] * w_ref[0, p, row, :]
                      + (sa * b_ref[0, p, row, :] + vv * k_ref[0, p, row, :]))
                s_ref[k0 + j] = sk
                if y_ref is not None:
                    y = y + sk * r_ref[p, row, :]
                nacc = nacc + sk * kk_ref[pn, pl.ds(basen + k0 + j, 1), :]
            return y, nacc

        zero = jnp.zeros((n, LANE), F32)
        y, nacc = lax.fori_loop(0, n // KCH, body, (zero, zero))
        if y_ref is not None:
            y_ref[p, pl.ds(base, n), :] = y
        return nacc

    first, second = ((n, 0) if reverse else (0, n))
    p_first = (npairs - 1) if reverse else 0

    def pair_step(i, acc):
        p = (npairs - 1 - i) if reverse else i
        pn = jnp.clip(p + (-1 if reverse else 1), 0, npairs - 1)
        acc = one_step(p, first, p, second, acc)
        return one_step(p, second, pn, first, acc)

    lax.fori_loop(0, npairs, pair_step, removal(p_first, first))


def _rwkv_scan(reverse, r, w, k, v, kk, b, s0, with_y=True, tbp=32):
    np_total = r.shape[0]
    tbp = min(tbp, np_total)
    assert np_total % tbp == 0
    nblk = np_total // tbp
    d = 1 if reverse else 0
    imap = (lambda i: (nblk - 1 - i, 0, 0)) if reverse else (lambda i: (i, 0, 0))
    imap_d = (lambda i: (d, nblk - 1 - i, 0, 0)) if reverse else (lambda i: (d, i, 0, 0))
    sp = pl.BlockSpec((tbp, LANE, LANE), imap)
    sp_d = pl.BlockSpec((1, tbp, LANE, LANE), imap_d)
    st = pl.BlockSpec((RWKV_HEAD, RWKV_HEAD, LANE), lambda i: (0, 0, 0))
    return pl.pallas_call(
        functools.partial(_scan_kernel, reverse),
        grid=(nblk,),
        in_specs=[sp, sp_d, sp_d, sp, sp, sp_d, st],
        out_specs=([sp] if with_y else []) + [st],
        out_shape=([jax.ShapeDtypeStruct(r.shape, F32)] if with_y else [])
        + [jax.ShapeDtypeStruct((RWKV_HEAD, RWKV_HEAD, LANE), F32)],
        compiler_params=_cparams("arbitrary"),
        name="rwkv_scan",
    )(r, w, k, v, kk, b, s0)


def _from_pairs(y_ref, nat_ref):
    npairs = y_ref.shape[0]
    lane = lax.broadcasted_iota(jnp.int32, (SUBLANE, LANE), 1)
    low = lane < RWKV_HEAD
    for p in range(npairs):
        xt = y_ref[p].T
        for j in range(nat_ref.shape[1] // LANE):
            h0 = xt[(2 * j) * SUBLANE:(2 * j + 1) * SUBLANE]
            h1 = xt[(2 * j + 1) * SUBLANE:(2 * j + 2) * SUBLANE]
            ev = jnp.where(low, h0, pltpu.roll(h1, RWKV_HEAD, 1))
            od = jnp.where(low, pltpu.roll(h0, RWKV_HEAD, 1), h1)
            nat_ref[(2 * p) * SUBLANE:(2 * p + 1) * SUBLANE, j * LANE:(j + 1) * LANE] = ev
            nat_ref[(2 * p + 1) * SUBLANE:(2 * p + 2) * SUBLANE, j * LANE:(j + 1) * LANE] = od


def _merge_kernel(y0_ref, y1_ref, bv_ref, go_ref, h0_ref, h1_ref, zg_ref, zgate_ref, x_ref,
                  lnx_ref, wa_ref, wb_ref, wo_ref, gt1_ref, gpost_ref, gpre_ref, scp_ref, sh_ref,
                  h_o, u_o, ub_o, yn_ref, nat_ref):
    n = RWKV_HEAD
    d = wa_ref.shape[1]
    npairs = y0_ref.shape[0]
    tt, bsz, _ = x_ref.shape
    for p in range(npairs):
        y = y0_ref[p] + y1_ref[p]
        for t2 in range(2):
            yh = y[t2 * n:(t2 + 1) * n]
            mu = jnp.mean(yh, axis=0, keepdims=True)
            var = jnp.mean(jnp.square(yh - mu), axis=0, keepdims=True)
            yn_ref[p, t2 * n:(t2 + 1) * n, :] = (yh - mu) * lax.rsqrt(var + LNX_EPS)
    _from_pairs(yn_ref, nat_ref)
    y_a = (nat_ref[...] * lnx_ref[0:1, :] + lnx_ref[1:2, :] + bv_ref[...]) * go_ref[...]
    pa = jnp.dot(y_a.astype(BF16), wa_ref[...], preferred_element_type=F32)
    y_b = jax.nn.gelu(zg_ref[...]) * (h0_ref[0] + h1_ref[0])
    pb = jnp.dot(y_b.astype(BF16), wb_ref[...], preferred_element_type=F32)
    gates = jax.nn.sigmoid(zgate_ref[...])
    mixed = gates[:, :d] * pa + gates[:, d:] * pb
    yo = jnp.dot(mixed.astype(BF16), wo_ref[...], preferred_element_type=F32).reshape(tt, bsz, d)
    yo = yo * lax.rsqrt(jnp.mean(yo * yo, axis=-1, keepdims=True) + NORM_EPS) * gpost_ref[...]
    h = x_ref[...] + gt1_ref[...] * yo
    h_o[...] = h
    u = h * lax.rsqrt(jnp.mean(h * h, axis=-1, keepdims=True) + NORM_EPS) * gpre_ref[...]
    u = (u * scp_ref[...] + sh_ref[...]).reshape(tt * bsz, d)
    u_o[...] = u
    ub_o[...] = u.astype(BF16)


def _merge(y0, y1, bv, g_out, h_lru, z, x_tm, lnx, wa, wb, wo, gt1, g_post, g_pre, scp, sh, tt=32):
    t, bsz, d = x_tm.shape
    rows = tt * bsz
    c = wa.shape[0]
    zcol = lambda blk, width: pl.BlockSpec((rows, width), lambda i: (i, blk))
    const = lambda shape: pl.BlockSpec(shape, lambda i: (0,) * len(shape))
    pair = pl.BlockSpec((tt // 2, LANE, LANE), lambda i: (i, 0, 0))
    nat = pl.BlockSpec((rows, c), lambda i: (i, 0))
    lg_blk = (3 * c + c + c) // c
    gate_blk = (lg_blk + 1) * c // (2 * d)
    return pl.pallas_call(
        _merge_kernel,
        grid=(t // tt,),
        in_specs=[pair, pair, nat, nat,
                  pl.BlockSpec((1, rows, c), lambda i: (0, i, 0)),
                  pl.BlockSpec((1, rows, c), lambda i: (1, i, 0)),
                  zcol(lg_blk, c), zcol(gate_blk, 2 * d),
                  pl.BlockSpec((tt, bsz, d), lambda i: (i, 0, 0)),
                  const(lnx.shape), const(wa.shape), const(wb.shape), const(wo.shape),
                  const(gt1.shape), const(g_post.shape), const(g_pre.shape),
                  const(scp.shape), const(sh.shape)],
        out_specs=[pl.BlockSpec((tt, bsz, d), lambda i: (i, 0, 0)),
                   pl.BlockSpec((rows, d), lambda i: (i, 0)),
                   pl.BlockSpec((rows, d), lambda i: (i, 0))],
        out_shape=[jax.ShapeDtypeStruct((t, bsz, d), F32),
                   jax.ShapeDtypeStruct((t * bsz, d), F32),
                   jax.ShapeDtypeStruct((t * bsz, d), BF16)],
        scratch_shapes=[pltpu.VMEM((tt // 2, LANE, LANE), F32), pltpu.VMEM((rows, c), F32)],
        compiler_params=_cparams("parallel"),
        name="merge",
    )(y0, y1, bv, g_out, h_lru, h_lru, z, z, x_tm, lnx, wa, wb, wo, gt1, g_post, g_pre, scp, sh)


def _experts_kernel(has_prev, blk_ref, eid_ref, lo_ref, hi_ref, x_ref, wgu_ref, wdn_ref, *rest):
    o_ref, wgu_b, wdn_b = rest[1:] if has_prev else rest
    i = pl.program_id(0)
    prev = jnp.maximum(i - 1, 0)
    ff = wdn_ref.shape[1]

    @pl.when(jnp.logical_or(i == 0, eid_ref[i] != eid_ref[prev]))
    def _():
        wgu_b[...] = wgu_ref[0].astype(BF16)
        wdn_b[...] = wdn_ref[0].astype(BF16)

    @pl.when(hi_ref[i] > lo_ref[i])
    def _():
        gu = jnp.dot(x_ref[...], wgu_b[...], preferred_element_type=F32)
        g = gu[:, :ff]
        u = gu[:, ff:]
        act = (g * jax.nn.sigmoid(g) * u).astype(BF16)
        res = jnp.dot(act, wdn_b[...], preferred_element_type=F32)
        row = lax.broadcasted_iota(jnp.int32, (res.shape[0], 1), 0)
        mine = jnp.logical_and(row >= lo_ref[i], row < hi_ref[i])
        first = jnp.logical_or(i == 0, blk_ref[i] != blk_ref[prev])

        @pl.when(first)
        def _():
            o_ref[...] = jnp.where(mine, res, 0.0).astype(o_ref.dtype)

        @pl.when(jnp.logical_not(first))
        def _():
            o_ref[...] = jnp.where(mine, res, o_ref[...].astype(F32)).astype(o_ref.dtype)


def _grouped_experts(xs, items, w_gu, w_down, n_rows, blk_off, prev_out=None):
    d = xs.shape[1]
    _, _, ff2 = w_gu.shape
    ff = ff2 // 2
    n_items = items[0].shape[0]
    in_specs = [pl.BlockSpec((MOE_ROWS, d), lambda i, blk, eid, lo, hi: (blk[i], 0)),
                pl.BlockSpec((1, d, ff2), lambda i, blk, eid, lo, hi: (eid[i], 0, 0)),
                pl.BlockSpec((1, ff, d), lambda i, blk, eid, lo, hi: (eid[i], 0, 0))]
    args = [*items, xs, w_gu, w_down]
    aliases = {}
    if prev_out is not None:
        in_specs.append(pl.BlockSpec(memory_space=pl.ANY))
        aliases = {len(args): 0}
        args.append(prev_out)
    grid_spec = pltpu.PrefetchScalarGridSpec(
        num_scalar_prefetch=4,
        grid=(n_items,),
        in_specs=in_specs,
        out_specs=pl.BlockSpec((MOE_ROWS, d), lambda i, blk, eid, lo, hi: (blk[i] + blk_off, 0)),
        scratch_shapes=[pltpu.VMEM((d, ff2), BF16), pltpu.VMEM((ff, d), BF16)],
    )
    return pl.pallas_call(
        functools.partial(_experts_kernel, prev_out is not None),
        grid_spec=grid_spec,
        out_shape=jax.ShapeDtypeStruct((n_rows, d), BF16),
        input_output_aliases=aliases,
        compiler_params=_cparams("arbitrary"),
        name="moe_experts",
    )(*args)


def _expert_items(start, end, r0, r1):
    n_e = start.shape[0]
    n_blocks = (r1 - r0) // MOE_ROWS
    n_items = n_blocks + n_e
    start = jnp.clip(start, r0, r1) - r0
    end = jnp.clip(end, r0, r1) - r0
    first_blk = start // MOE_ROWS
    per_e = jnp.where(end > start, (end - 1) // MOE_ROWS - first_blk + 1, 0)
    item_end = jnp.cumsum(per_e)
    w = jnp.arange(n_items, dtype=jnp.int32)
    eid = jnp.minimum(jnp.sum((item_end[None, :] <= w[:, None]).astype(jnp.int32), axis=1), n_e - 1)
    onehot = (eid[:, None] == jnp.arange(n_e)[None, :]).astype(jnp.int32)
    pick = lambda v: jnp.sum(onehot * v[None, :], axis=1)
    valid = w < item_end[-1]
    blk = jnp.where(valid, pick(first_blk) + w - (pick(item_end) - pick(per_e)), n_blocks - 1)
    lo = jnp.clip(pick(start) - blk * MOE_ROWS, 0, MOE_ROWS)
    hi = jnp.where(valid, jnp.clip(pick(end) - blk * MOE_ROWS, 0, MOE_ROWS), lo)
    i32 = lambda v: v.astype(jnp.int32)
    return i32(blk), i32(eid), i32(lo), i32(hi)


def _argmax_pick(cur, iota, big, axis):
    mx = jnp.max(cur, axis=axis, keepdims=True)
    ix = jnp.min(jnp.where(cur == mx, iota, big), axis=axis, keepdims=True)
    return iota == ix


def _router_kernel(u_ref, wt_ref, bias_ref, tri_ref, gate_o, idx_o, rank_o, cnt_o):
    n_e = wt_ref.shape[0]
    tr = u_ref.shape[0]
    per_g = n_e // N_GROUPS
    neg = -jnp.inf

    @pl.when(pl.program_id(0) == 0)
    def _():
        cnt_o[...] = jnp.zeros_like(cnt_o)

    logits = lax.dot_general(wt_ref[...], u_ref[...], (((1,), (1,)), ((), ())),
                             precision=HP, preferred_element_type=F32)
    scores = jax.nn.sigmoid(logits)
    biased = scores + bias_ref[...]
    b3 = biased.reshape(N_GROUPS, per_g, tr)
    ie = lax.broadcasted_iota(jnp.int32, b3.shape, 1)
    p1 = _argmax_pick(b3, ie, per_g, 1)
    m1 = jnp.max(b3, axis=1)
    m2 = jnp.max(jnp.where(p1, neg, b3), axis=1)
    cur = m1 + m2
    ig = lax.broadcasted_iota(jnp.int32, cur.shape, 0)
    sel_g = jnp.zeros(cur.shape, F32)
    for _ in range(TOPK_GROUPS):
        pick = _argmax_pick(cur, ig, N_GROUPS, 0)
        sel_g = jnp.where(pick, 1.0, sel_g)
        cur = jnp.where(pick, neg, cur)
    cur = jnp.where(sel_g[:, None, :] > 0.5, b3, neg).reshape(n_e, tr)
    i_e = lax.broadcasted_iota(jnp.int32, cur.shape, 0)
    i_ef = i_e.astype(F32)
    picks, top_s = [], []
    sel_f = jnp.zeros(cur.shape, F32)
    for _ in range(TOP_K):
        pick = _argmax_pick(cur, i_e, n_e, 0)
        picks.append(pick)
        top_s.append(jnp.sum(jnp.where(pick, scores, 0.0), axis=0, keepdims=True))
        sel_f = jnp.where(pick, 1.0, sel_f)
        cur = jnp.where(pick, neg, cur)
    denom = top_s[0]
    for s in top_s[1:]:
        denom = denom + s
    rank = jnp.dot(sel_f.astype(BF16), tri_ref[...], preferred_element_type=F32) + cnt_o[:, 0:1]
    cnt_o[...] = cnt_o[...] + jnp.sum(sel_f, axis=1, keepdims=True)
    zrow = jnp.zeros((SUBLANE - TOP_K, tr), F32)
    gate_o[...] = jnp.concatenate([s / denom * ROUTED_SCALE for s in top_s] + [zrow], axis=0)
    idx_rows = [jnp.sum(jnp.where(p, i_ef, 0.0), axis=0, keepdims=True) for p in picks]
    idx_o[...] = jnp.concatenate(idx_rows + [zrow], axis=0).astype(jnp.int32)
    rank_rows = [jnp.sum(jnp.where(p, rank, 0.0), axis=0, keepdims=True) for p in picks]
    rank_o[...] = jnp.concatenate(rank_rows + [zrow], axis=0).astype(jnp.int32)


def _router(u, w_t, bias, tr=512):
    m, d = u.shape
    n_e = w_t.shape[0]
    tri = (jnp.arange(tr)[:, None] < jnp.arange(tr)[None, :]).astype(BF16)
    tok = pl.BlockSpec((SUBLANE, tr), lambda i: (0, i))
    return pl.pallas_call(
        _router_kernel,
        grid=(m // tr,),
        in_specs=[pl.BlockSpec((tr, d), lambda i: (i, 0)),
                  pl.BlockSpec((n_e, d), lambda i: (0, 0)),
                  pl.BlockSpec((n_e, 1), lambda i: (0, 0)),
                  pl.BlockSpec((tr, tr), lambda i: (0, 0))],
        out_specs=[tok, tok, tok, pl.BlockSpec((n_e, LANE), lambda i: (0, 0))],
        out_shape=[jax.ShapeDtypeStruct((SUBLANE, m), F32),
                   jax.ShapeDtypeStruct((SUBLANE, m), jnp.int32),
                   jax.ShapeDtypeStruct((SUBLANE, m), jnp.int32),
                   jax.ShapeDtypeStruct((n_e, LANE), F32)],
        compiler_params=_cparams("arbitrary"),
        name="router",
    )(u, w_t, bias, tri)


def _final_kernel(*refs):
    ye_refs = refs[:TOP_K]
    gate_ref, ub_ref, h_ref, wgu_ref, wdn_ref, gt_ref, g_ref, o_ref, rl_ref = refs[TOP_K:]
    tt, bsz, d = h_ref.shape
    ff = wdn_ref.shape[0]
    routed = jnp.zeros((tt * bsz, d), F32)
    for j in range(TOP_K):
        routed = routed + gate_ref[:, j:j + 1] * ye_refs[j][0].astype(F32)
    gu = jnp.dot(ub_ref[...], wgu_ref[...], preferred_element_type=F32)
    g = gu[:, :ff]
    act = (g * jax.nn.sigmoid(g) * gu[:, ff:]).astype(BF16)
    moe = (routed + jnp.dot(act, wdn_ref[...], preferred_element_type=F32)).reshape(tt, bsz, d)
    moe = moe * lax.rsqrt(jnp.mean(moe * moe, axis=-1, keepdims=True) + NORM_EPS) * g_ref[...]
    out = (h_ref[...] + gt_ref[...] * moe).reshape(tt * bsz, d)
    for j in range(d // LANE):
        rl_ref[j] = out[:, j * LANE:(j + 1) * LANE]
    for b in range(bsz):
        for j in range(d // LANE):
            o_ref[b, :, j * LANE:(j + 1) * LANE] = rl_ref[j, pl.ds(b, tt, stride=bsz), :]


def _final(ye, gate, ub, h, wgu, wdn, gt, g, tt=64):
    t, bsz, d = h.shape
    rows = tt * bsz
    const = lambda shape: pl.BlockSpec(shape, lambda i: (0,) * len(shape))
    return pl.pallas_call(
        _final_kernel,
        grid=(t // tt,),
        in_specs=[pl.BlockSpec((1, rows, d), functools.partial(lambda j, i: (j, i, 0), j))
                  for j in range(TOP_K)] + [
                  pl.BlockSpec((rows, SUBLANE), lambda i: (i, 0)),
                  pl.BlockSpec((rows, d), lambda i: (i, 0)),
                  pl.BlockSpec((tt, bsz, d), lambda i: (i, 0, 0)),
                  const(wgu.shape), const(wdn.shape), const(gt.shape), const(g.shape)],
        out_specs=pl.BlockSpec((bsz, tt, d), lambda i: (0, i, 0)),
        out_shape=jax.ShapeDtypeStruct((bsz, t, d), F32),
        scratch_shapes=[pltpu.VMEM((d // LANE, rows, LANE), F32)],
        compiler_params=_cparams("parallel"),
        name="moe_final",
    )(*([ye] * TOP_K), gate, ub, h, wgu, wdn, gt, g)


def kernel(x, c, ctx, c_ctx, w_mod, b_mod, norm_g, w_in, shift_mu, rw_w0, rw_w_up, rw_a0, rw_a_up,
           rw_g_up, rw_k_k, rw_k_a, rw_r_k, rw_lnx, lru_conv_w, lru_conv_b, lru_gate_w, lru_gate_b,
           lru_l, w_branch_a, w_branch_b, w_out, router_w, router_b, ex_w_gu, ex_w_down,
           sh_w_gu, sh_w_down):
    l = 0
    bsz, t, d = x.shape
    tc = ctx.shape[1]
    rows_g = t // GRID_W
    m = bsz * t
    cw = rw_w0.shape[-1]
    cl = lru_l.shape[-1]
    nh = cw // RWKV_HEAD
    assert bsz == SUBLANE and cw == cl == d and 2 * nh * bsz // 2 == LANE
    rwkv_cols = 3 * cw + 2 * DECAY_LORA + 2 * AAA_LORA + GATE_LORA
    lora_cols = rwkv_cols - 3 * cw

    mod = jnp.dot(jax.nn.silu(c), w_mod[l], precision=HP) + b_mod[l]
    mod_c = jnp.dot(jax.nn.silu(c_ctx), w_mod[l], precision=HP) + b_mod[l]
    sh1, sc1, gt1, sh2, sc2, gt2 = jnp.split(mod, 6, axis=-1)
    csh1 = jnp.broadcast_to(mod_c[0:d], (bsz, d))
    csc1 = jnp.broadcast_to(mod_c[d:2 * d], (bsz, d))
    g_pre1, g_post1, g_pre2, g_post2 = [g[None, :] for g in norm_g[l]]

    wi = w_in[l]
    w_al = jnp.concatenate([wi[:, :rwkv_cols], jnp.zeros((d, cw - lora_cols), F32),
                            wi[:, rwkv_cols:]], axis=1).astype(BF16)
    zr_w = 4 * cw
    mu_al = jnp.concatenate([shift_mu[l], jnp.zeros((2, cw - lora_cols), F32)], axis=1)

    x_tm = jnp.transpose(x, (1, 0, 2))
    ctx_tm = jnp.transpose(ctx, (1, 0, 2))
    zc = _inproj(ctx_tm, 1.0 + csc1, csh1, g_pre1, w_al[:, :zr_w + cl])
    z = _inproj(x_tm, 1.0 + sc1, sh1, g_pre1, w_al, tn=2 * cw)

    blk = cl // LRU_BLOCKS
    per = MXU_N // blk
    gw = lru_gate_w[l].reshape(2, 2, LRU_BLOCKS // per, per, blk, blk)
    eye = jnp.eye(per, dtype=F32)
    gw_bd = jnp.einsum('dgqpjk,pr->dgqpjrk', gw, eye).reshape(2, 2, LRU_BLOCKS // per, MXU_N, MXU_N)
    gw_bd = gw_bd.astype(BF16)
    gb = lru_gate_b[l].reshape(2, 2, cl)
    sp_l = jax.nn.softplus(-lru_l[l])[:, None, :]
    lru_args = (lru_conv_w[l], lru_conv_b[l][None, :], gw_bd, gb, sp_l)
    _, hfin = _lru(zc.reshape(tc, 1, bsz, -1), 4, *lru_args, jnp.zeros((2, bsz, cl), F32))
    h_lru, _ = _lru(z.reshape(rows_g, GRID_W, bsz, -1), 4, *lru_args, hfin)
    h_lru = h_lru.reshape(2, m, cl)

    zeros_c = jnp.zeros((DECAY_LORA, cw), F32)
    wup2 = jnp.concatenate([jnp.concatenate([rw_w_up[l, 0], zeros_c], 1),
                            jnp.concatenate([zeros_c, rw_w_up[l, 1]], 1)], axis=0)
    aup2 = jnp.concatenate([jnp.concatenate([rw_a_up[l, 0], zeros_c], 1),
                            jnp.concatenate([zeros_c, rw_a_up[l, 1]], 1)], axis=0)
    gup = jnp.concatenate([rw_g_up[l], jnp.zeros((MXU_N - GATE_LORA, cw), F32)], axis=0)
    hid = jnp.arange(MXU_N) // RWKV_HEAD
    ones_blk = (hid[:, None] == hid[None, :]).astype(BF16)
    prep_args = (mu_al, rw_w0[l], wup2.astype(BF16), rw_a0[l], aup2.astype(BF16),
                 rw_k_k[l][None, :], rw_k_a[l][None, :], rw_r_k[l].reshape(1, cw),
                 gup.astype(BF16), ones_blk)
    pc = _rwkv_prep(zc, tc, zr_w, *prep_args)
    pz = _rwkv_prep(z, GRID_W, zr_w, *prep_args)
    s_zero = jnp.zeros((RWKV_HEAD, RWKV_HEAD, LANE), F32)
    ys = []
    for rev in (False, True):
        (s_c,) = _rwkv_scan(rev, pc[0], pc[3], pc[4], pc[1], pc[2], pc[5], s_zero, with_y=False)
        y_d, _ = _rwkv_scan(rev, pz[0], pz[3], pz[4], pz[1], pz[2], pz[5], s_c)
        ys.append(y_d)

    h, u2, u2b = _merge(ys[0], ys[1], pz[6], pz[7], h_lru, z, x_tm, rw_lnx[l],
                        w_branch_a[l].astype(BF16), w_branch_b[l].astype(BF16), w_out[l].astype(BF16),
                        gt1, g_post1, g_pre2, 1.0 + sc2, sh2)

    n_e = router_w.shape[-1]
    gate6, idx6, rank6, cnt = _router(u2, router_w[l].T, router_b[l][:, None])
    n_assign = m * TOP_K
    counts = cnt[:, 0].astype(jnp.int32)
    end = jnp.cumsum(counts)
    start = end - counts
    flat_e = idx6[:TOP_K].T.reshape(-1)
    bits = (n_assign - 1).bit_length()
    assert n_e << bits < 2 ** 31
    key = jnp.sort(flat_e * (1 << bits) + jnp.arange(n_assign, dtype=jnp.int32))
    sorted_tok = (key & ((1 << bits) - 1)) // TOP_K
    half = n_assign // 2
    assert half % MOE_ROWS == 0 and t % 2 == 0
    ys_e = None
    for r0, r1 in ((0, half), (half, n_assign)):
        items = _expert_items(start, end, r0, r1)
        ys_e = _grouped_experts(u2b[sorted_tok[r0:r1]], items, ex_w_gu[l], ex_w_down[l],
                                n_assign, r0 // MOE_ROWS, ys_e)
    sel = idx6[:TOP_K, :, None] == jnp.arange(n_e, dtype=jnp.int32)[None, None, :]
    pos = jnp.sum(jnp.where(sel, start[None, None, :], 0), axis=-1) + rank6[:TOP_K]
    ye = ys_e[pos.reshape(-1)].reshape(TOP_K, m, d)
    return _final(ye, gate6.T, u2b, h, sh_w_gu[l].astype(BF16), sh_w_down[l].astype(BF16),
                  gt2, g_post2)
```

```python
import functools

import jax
import jax.numpy as jnp
from jax import lax
from jax.experimental import pallas as pl
from jax.experimental.pallas import tpu as pltpu

GRID_W = 64
NORM_EPS = 1e-6
RWKV_HEAD = 64
DECAY_LORA = 64
AAA_LORA = 64
GATE_LORA = 160
LNX_EPS = 64e-5
LRU_BLOCKS = 16
CONV_LEFT = 2
LRU_C = 8.0
N_GROUPS = 8
TOPK_GROUPS = 4
TOP_K = 6
ROUTED_SCALE = 2.5
DECAY_SCALE = 0.6065306597126334

LANE = 128
SUBLANE = 8
MXU_N = 256
MOE_ROWS = 512
KCH = 32
LRU_BLOCK_BYTES = 4 * 1024 * 1024
VMEM_LIMIT = 48 * 1024 * 1024
HP = lax.Precision.HIGHEST
F32 = jnp.float32
BF16 = jnp.bfloat16


def _cparams(*sem):
    return pltpu.CompilerParams(dimension_semantics=sem, vmem_limit_bytes=VMEM_LIMIT)


def _inproj_kernel(x_ref, scp_ref, sh_ref, g_ref, w_ref, o_ref, xm_ref):
    @pl.when(pl.program_id(1) == 0)
    def _():
        x = x_ref[...]
        ms = jnp.mean(x * x, axis=-1, keepdims=True)
        xm = x * lax.rsqrt(ms + NORM_EPS) * g_ref[...] * scp_ref[...] + sh_ref[...]
        xm_ref[...] = xm.reshape(xm_ref.shape).astype(BF16)

    o_ref[...] = jnp.dot(xm_ref[...], w_ref[...], preferred_element_type=F32)


def _inproj(x_tm, scp, sh, g, w, tt=128, tn=1024):
    t, bsz, d = x_tm.shape
    n = w.shape[1]
    tt = min(tt, t)
    assert t % tt == 0 and n % tn == 0
    return pl.pallas_call(
        _inproj_kernel,
        grid=(t // tt, n // tn),
        in_specs=[pl.BlockSpec((tt, bsz, d), lambda i, j: (i, 0, 0)),
                  pl.BlockSpec((bsz, d), lambda i, j: (0, 0)),
                  pl.BlockSpec((bsz, d), lambda i, j: (0, 0)),
                  pl.BlockSpec((1, d), lambda i, j: (0, 0)),
                  pl.BlockSpec((d, tn), lambda i, j: (0, j))],
        out_specs=pl.BlockSpec((tt * bsz, tn), lambda i, j: (i, j)),
        out_shape=jax.ShapeDtypeStruct((t * bsz, n), F32),
        scratch_shapes=[pltpu.VMEM((tt * bsz, d), BF16)],
        compiler_params=_cparams("parallel", "arbitrary"),
        name="in_proj",
    )(x_tm, scp, sh, g, w)


def _lru_kernel(z_ref, cw_ref, cb_ref, gw_ref, gb_ref, sp_ref, h0_ref, h_ref, hf_ref,
                a_ref, bx_ref):
    d = pl.program_id(0)
    w = pl.program_id(2)
    nr, _, bsz, cq = z_ref.shape
    taps = cw_ref.shape[0]

    @pl.when(w == 0)
    def _():
        hf_ref[0] = h0_ref[0]

    zx = z_ref[:, 0]
    xc = jnp.zeros_like(zx) + cb_ref[...]
    for j in range(taps):
        off = j - CONV_LEFT
        if off < 0:
            sh = jnp.concatenate([jnp.zeros((-off, bsz, cq), F32), zx[:nr + off]], axis=0)
        elif off > 0:
            sh = jnp.concatenate([zx[off:], jnp.zeros((off, bsz, cq), F32)], axis=0)
        else:
            sh = zx
        xc = xc + cw_ref[j:j + 1, :] * sh
    x2 = xc.reshape(nr * bsz, cq)
    xb = x2.astype(BF16)
    pre = []
    for g in range(2):
        cols = []
        for q in range(cq // MXU_N):
            cols.append(jnp.dot(xb[:, q * MXU_N:(q + 1) * MXU_N], gw_ref[0, g, q],
                                preferred_element_type=F32))
        pre.append(jnp.concatenate(cols, axis=1) + gb_ref[0, g:g + 1, :])
    rec = 0.5 * jnp.tanh(0.5 * pre[0]) + 0.5
    inp = 0.5 * jnp.tanh(0.5 * pre[1]) + 0.5
    log_a = -LRU_C * rec * sp_ref[0]
    a = jnp.exp(log_a)
    bx = jnp.sqrt(-jnp.tanh(log_a) * (a * a + 1.0)) * inp * x2
    a_ref[...] = a.reshape(nr, bsz, cq)
    bx_ref[...] = bx.reshape(nr, bsz, cq)

    def step(i, h):
        r = i + d * (nr - 1 - 2 * i)
        h = a_ref[r] * h + bx_ref[r]
        h_ref[0, r, 0] = h
        return h

    hf_ref[0] = lax.fori_loop(0, nr, step, hf_ref[0], unroll=8)


def _lru(z4, col_blk, conv_w, conv_b, gate_w, gate_b, sp_l, h0):
    nr, nw, bsz, _ = z4.shape
    c = conv_w.shape[1]
    cq = min(c, max(MXU_N, LRU_BLOCK_BYTES // (nr * bsz * 4)))
    nq = c // cq
    cb0 = col_blk * (c // cq)

    def zmap(d, q, w):
        return (0, w + d * (nw - 1 - 2 * w), 0, cb0 + q)

    def hmap(d, q, w):
        return (d, 0, w + d * (nw - 1 - 2 * w), 0, q)

    return pl.pallas_call(
        _lru_kernel,
        grid=(2, nq, nw),
        in_specs=[pl.BlockSpec((nr, 1, bsz, cq), zmap),
                  pl.BlockSpec((conv_w.shape[0], cq), lambda d, q, w: (0, q)),
                  pl.BlockSpec((1, cq), lambda d, q, w: (0, q)),
                  pl.BlockSpec((1, 2, cq // MXU_N, MXU_N, MXU_N), lambda d, q, w: (d, 0, q, 0, 0)),
                  pl.BlockSpec((1, 2, cq), lambda d, q, w: (d, 0, q)),
                  pl.BlockSpec((1, 1, cq), lambda d, q, w: (d, 0, q)),
                  pl.BlockSpec((1, bsz, cq), lambda d, q, w: (d, 0, q))],
        out_specs=[pl.BlockSpec((1, nr, 1, bsz, cq), hmap),
                   pl.BlockSpec((1, bsz, cq), lambda d, q, w: (d, 0, q))],
        out_shape=[jax.ShapeDtypeStruct((2, nr, nw, bsz, c), F32),
                   jax.ShapeDtypeStruct((2, bsz, c), F32)],
        scratch_shapes=[pltpu.VMEM((nr, bsz, cq), F32), pltpu.VMEM((nr, bsz, cq), F32)],
        compiler_params=_cparams("arbitrary", "arbitrary", "arbitrary"),
        name="lru_fused",
    )(z4, conv_w, conv_b, gate_w, gate_b, sp_l, h0)


def _to_pairs(q, out_ref, lead=()):
    rows, c = q.shape
    npairs = rows // (2 * SUBLANE)
    lane = lax.broadcasted_iota(jnp.int32, (SUBLANE, LANE), 1)
    low = lane < RWKV_HEAD
    for p in range(npairs):
        ev = q[(2 * p) * SUBLANE:(2 * p + 1) * SUBLANE]
        od = q[(2 * p + 1) * SUBLANE:(2 * p + 2) * SUBLANE]
        pieces = []
        for j in range(c // LANE):
            e_j = ev[:, j * LANE:(j + 1) * LANE]
            o_j = od[:, j * LANE:(j + 1) * LANE]
            pieces.append(jnp.where(low, e_j, pltpu.roll(o_j, RWKV_HEAD, 1)))
            pieces.append(jnp.where(low, pltpu.roll(e_j, RWKV_HEAD, 1), o_j))
        x = jnp.concatenate(pieces, axis=0)
        out_ref[lead + (p,)] = x.T


def _group_sum(x, ones_ref):
    hi = x.astype(BF16)
    lo = (x - hi.astype(F32)).astype(BF16)
    cols = []
    for q in range(x.shape[1] // MXU_N):
        sl = slice(q * MXU_N, (q + 1) * MXU_N)
        cols.append(jnp.dot(hi[:, sl], ones_ref[...], preferred_element_type=F32)
                    + jnp.dot(lo[:, sl], ones_ref[...], preferred_element_type=F32))
    return jnp.concatenate(cols, axis=1)


def _prep_kernel(period, z_ref, zp_ref, zn_ref, mu_ref, w0_ref, wup_ref, a0_ref, aup_ref,
                 kk_ref, ka_ref, rk_ref, gup_ref, ones_ref,
                 r_o, v_o, kn_o, w_o, kd_o, b_o, bv_o, g_o):
    i = pl.program_id(0)
    rows, _ = z_ref.shape
    tt = rows // SUBLANE
    cw = w0_ref.shape[1]
    z = z_ref[...]
    has_prev = ((i * tt) % period != 0).astype(F32)
    has_next = (((i + 1) * tt) % period != 0).astype(F32)
    zprev = jnp.concatenate([zp_ref[...] * has_prev, z[:rows - SUBLANE]], axis=0)
    znext = jnp.concatenate([z[SUBLANE:], zn_ref[...] * has_next], axis=0)
    zs = z + mu_ref[0:1, :] * (zprev - z) + mu_ref[1:2, :] * (znext - z)
    r = zs[:, 0:cw]
    k = zs[:, cw:2 * cw]
    v = zs[:, 2 * cw:3 * cw]
    off = 3 * cw
    wd = jnp.tanh(zs[:, off:off + LANE])
    ad = zs[:, off + LANE:off + 2 * LANE]
    gd = jax.nn.sigmoid(zs[:, off + 2 * LANE:off + 2 * LANE + MXU_N])
    wl = jnp.dot(wd.astype(BF16), wup_ref[...], preferred_element_type=F32)
    al = jnp.dot(ad.astype(BF16), aup_ref[...], preferred_element_type=F32)
    g_o[...] = jnp.dot(gd.astype(BF16), gup_ref[...], preferred_element_type=F32)
    kk = k * kk_ref[...]
    nrm = jnp.sqrt(_group_sum(kk * kk, ones_ref))
    kn = kk / jnp.maximum(nrm, 1e-12)
    _to_pairs(r, r_o)
    _to_pairs(v, v_o)
    _to_pairs(kn, kn_o)
    ksum = jnp.zeros_like(k)
    for d in range(2):
        xw = w0_ref[d:d + 1, :] + wl[:, d * cw:(d + 1) * cw]
        decay = jnp.exp(-DECAY_SCALE * jax.nn.sigmoid(xw))
        a = jax.nn.sigmoid(a0_ref[d:d + 1, :] + al[:, d * cw:(d + 1) * cw])
        kd = k * (1.0 + (a - 1.0) * ka_ref[...])
        ksum = ksum + kd
        _to_pairs(decay, w_o, (d,))
        _to_pairs(kd, kd_o, (d,))
        _to_pairs(kn * a, b_o, (d,))
    bonus = _group_sum(r * ksum * rk_ref[...], ones_ref)
    bv_o[...] = bonus * v


def _rwkv_prep(z, period, zcols, mu, w0, wup, a0, aup, k_k, k_a, r_k, gup, ones, tt=32):
    rows_total = z.shape[0]
    t = rows_total // SUBLANE
    cw = w0.shape[1]
    tt = min(tt, period)
    assert t % tt == 0 and period % tt == 0 and tt % 2 == 0
    rows = tt * SUBLANE
    nblk = t // tt
    const = lambda shape: pl.BlockSpec(shape, lambda i: (0,) * len(shape))
    pair = pl.BlockSpec((tt // 2, LANE, LANE), lambda i: (i, 0, 0))
    pair_d = pl.BlockSpec((2, tt // 2, LANE, LANE), lambda i: (0, i, 0, 0))
    nat = pl.BlockSpec((rows, cw), lambda i: (i, 0))
    sh_pair = jax.ShapeDtypeStruct((t // 2, LANE, LANE), F32)
    sh_pair_d = jax.ShapeDtypeStruct((2, t // 2, LANE, LANE), F32)
    sh_nat = jax.ShapeDtypeStruct((rows_total, cw), F32)
    return pl.pallas_call(
        functools.partial(_prep_kernel, period),
        grid=(nblk,),
        in_specs=[pl.BlockSpec((rows, zcols), lambda i: (i, 0)),
                  pl.BlockSpec((SUBLANE, zcols), lambda i: (jnp.maximum(i * tt - 1, 0), 0)),
                  pl.BlockSpec((SUBLANE, zcols), lambda i: (jnp.minimum((i + 1) * tt, t - 1), 0)),
                  const(mu.shape), const(w0.shape), const(wup.shape), const(a0.shape),
                  const(aup.shape), const(k_k.shape), const(k_a.shape), const(r_k.shape),
                  const(gup.shape), const(ones.shape)],
        out_specs=[pair, pair, pair, pair_d, pair_d, pair_d, nat, nat],
        out_shape=[sh_pair, sh_pair, sh_pair, sh_pair_d, sh_pair_d, sh_pair_d, sh_nat, sh_nat],
        compiler_params=_cparams("parallel"),
        name="rwkv_prep",
    )(z, z, z, mu, w0, wup, a0, aup, k_k, k_a, r_k, gup, ones)


def _scan_kernel(reverse, r_ref, w_ref, k_ref, v_ref, kk_ref, b_ref, s0_ref, *out_refs):
    y_ref = out_refs[0] if len(out_refs) == 2 else None
    s_ref = out_refs[-1]
    n = RWKV_HEAD
    npairs = r_ref.shape[0]

    @pl.when(pl.program_id(0) == 0)
    def _():
        s_ref[...] = s0_ref[...]

    def removal(p, base):
        def body(kb, acc):
            k0 = pl.multiple_of(kb * KCH, KCH)
            for j in range(KCH):
                acc = acc + s_ref[k0 + j] * kk_ref[p, pl.ds(base + k0 + j, 1), :]
            return acc

        return lax.fori_loop(0, n // KCH, body, jnp.zeros((n, LANE), F32))

    def one_step(p, base, pn, basen, acc):
        sa = -acc
        vv = v_ref[p, pl.ds(base, n), :]

        def body(kb, carry):
            y, nacc = carry
            k0 = pl.multiple_of(kb * KCH, KCH)
            for j in range(KCH):
                row = pl.ds(base + k0 + j, 1)
                sk = (s_ref[k0 + j] * w_ref[0, p, row, :]
                      + (sa * b_ref[0, p, row, :] + vv * k_ref[0, p, row, :]))
                s_ref[k0 + j] = sk
                if y_ref is not None:
                    y = y + sk * r_ref[p, row, :]
                nacc = nacc + sk * kk_ref[pn, pl.ds(basen + k0 + j, 1), :]
            return y, nacc

        zero = jnp.zeros((n, LANE), F32)
        y, nacc = lax.fori_loop(0, n // KCH, body, (zero, zero))
        if y_ref is not None:
            y_ref[p, pl.ds(base, n), :] = y
        return nacc

    first, second = ((n, 0) if reverse else (0, n))
    p_first = (npairs - 1) if reverse else 0

    def pair_step(i, acc):
        p = (npairs - 1 - i) if reverse else i
        pn = jnp.clip(p + (-1 if reverse else 1), 0, npairs - 1)
        acc = one_step(p, first, p, second, acc)
        return one_step(p, second, pn, first, acc)

    lax.fori_loop(0, npairs, pair_step, removal(p_first, first))


def _rwkv_scan(reverse, r, w, k, v, kk, b, s0, with_y=True, tbp=32):
    np_total = r.shape[0]
    tbp = min(tbp, np_total)
    assert np_total % tbp == 0
    nblk = np_total // tbp
    d = 1 if reverse else 0
    imap = (lambda i: (nblk - 1 - i, 0, 0)) if reverse else (lambda i: (i, 0, 0))
    imap_d = (lambda i: (d, nblk - 1 - i, 0, 0)) if reverse else (lambda i: (d, i, 0, 0))
    sp = pl.BlockSpec((tbp, LANE, LANE), imap)
    sp_d = pl.BlockSpec((1, tbp, LANE, LANE), imap_d)
    st = pl.BlockSpec((RWKV_HEAD, RWKV_HEAD, LANE), lambda i: (0, 0, 0))
    return pl.pallas_call(
        functools.partial(_scan_kernel, reverse),
        grid=(nblk,),
        in_specs=[sp, sp_d, sp_d, sp, sp, sp_d, st],
        out_specs=([sp] if with_y else []) + [st],
        out_shape=([jax.ShapeDtypeStruct(r.shape, F32)] if with_y else [])
        + [jax.ShapeDtypeStruct((RWKV_HEAD, RWKV_HEAD, LANE), F32)],
        compiler_params=_cparams("arbitrary"),
        name="rwkv_scan",
    )(r, w, k, v, kk, b, s0)


def _from_pairs(y_ref, nat_ref):
    npairs = y_ref.shape[0]
    lane = lax.broadcasted_iota(jnp.int32, (SUBLANE, LANE), 1)
    low = lane < RWKV_HEAD
    for p in range(npairs):
        xt = y_ref[p].T
        for j in range(nat_ref.shape[1] // LANE):
            h0 = xt[(2 * j) * SUBLANE:(2 * j + 1) * SUBLANE]
            h1 = xt[(2 * j + 1) * SUBLANE:(2 * j + 2) * SUBLANE]
            ev = jnp.where(low, h0, pltpu.roll(h1, RWKV_HEAD, 1))
            od = jnp.where(low, pltpu.roll(h0, RWKV_HEAD, 1), h1)
            nat_ref[(2 * p) * SUBLANE:(2 * p + 1) * SUBLANE, j * LANE:(j + 1) * LANE] = ev
            nat_ref[(2 * p + 1) * SUBLANE:(2 * p + 2) * SUBLANE, j * LANE:(j + 1) * LANE] = od


def _merge_kernel(y0_ref, y1_ref, bv_ref, go_ref, h0_ref, h1_ref, zg_ref, zgate_ref, x_ref,
                  lnx_ref, wa_ref, wb_ref, wo_ref, gt1_ref, gpost_ref, gpre_ref, scp_ref, sh_ref,
                  h_o, u_o, ub_o, yn_ref, nat_ref):
    n = RWKV_HEAD
    d = wa_ref.shape[1]
    npairs = y0_ref.shape[0]
    tt, bsz, _ = x_ref.shape
    for p in range(npairs):
        y = y0_ref[p] + y1_ref[p]
        for t2 in range(2):
            yh = y[t2 * n:(t2 + 1) * n]
            mu = jnp.mean(yh, axis=0, keepdims=True)
            var = jnp.mean(jnp.square(yh - mu), axis=0, keepdims=True)
            yn_ref[p, t2 * n:(t2 + 1) * n, :] = (yh - mu) * lax.rsqrt(var + LNX_EPS)
    _from_pairs(yn_ref, nat_ref)
    y_a = (nat_ref[...] * lnx_ref[0:1, :] + lnx_ref[1:2, :] + bv_ref[...]) * go_ref[...]
    pa = jnp.dot(y_a.astype(BF16), wa_ref[...], preferred_element_type=F32)
    y_b = jax.nn.gelu(zg_ref[...]) * (h0_ref[0] + h1_ref[0])
    pb = jnp.dot(y_b.astype(BF16), wb_ref[...], preferred_element_type=F32)
    gates = jax.nn.sigmoid(zgate_ref[...])
    mixed = gates[:, :d] * pa + gates[:, d:] * pb
    yo = jnp.dot(mixed.astype(BF16), wo_ref[...], preferred_element_type=F32).reshape(tt, bsz, d)
    yo = yo * lax.rsqrt(jnp.mean(yo * yo, axis=-1, keepdims=True) + NORM_EPS) * gpost_ref[...]
    h = x_ref[...] + gt1_ref[...] * yo
    h_o[...] = h
    u = h * lax.rsqrt(jnp.mean(h * h, axis=-1, keepdims=True) + NORM_EPS) * gpre_ref[...]
    u = (u * scp_ref[...] + sh_ref[...]).reshape(tt * bsz, d)
    u_o[...] = u
    ub_o[...] = u.astype(BF16)


def _merge(y0, y1, bv, g_out, h_lru, z, x_tm, lnx, wa, wb, wo, gt1, g_post, g_pre, scp, sh, tt=32):
    t, bsz, d = x_tm.shape
    rows = tt * bsz
    c = wa.shape[0]
    zcol = lambda blk, width: pl.BlockSpec((rows, width), lambda i: (i, blk))
    const = lambda shape: pl.BlockSpec(shape, lambda i: (0,) * len(shape))
    pair = pl.BlockSpec((tt // 2, LANE, LANE), lambda i: (i, 0, 0))
    nat = pl.BlockSpec((rows, c), lambda i: (i, 0))
    lg_blk = (3 * c + c + c) // c
    gate_blk = (lg_blk + 1) * c // (2 * d)
    return pl.pallas_call(
        _merge_kernel,
        grid=(t // tt,),
        in_specs=[pair, pair, nat, nat,
                  pl.BlockSpec((1, rows, c), lambda i: (0, i, 0)),
                  pl.BlockSpec((1, rows, c), lambda i: (1, i, 0)),
                  zcol(lg_blk, c), zcol(gate_blk, 2 * d),
                  pl.BlockSpec((tt, bsz, d), lambda i: (i, 0, 0)),
                  const(lnx.shape), const(wa.shape), const(wb.shape), const(wo.shape),
                  const(gt1.shape), const(g_post.shape), const(g_pre.shape),
                  const(scp.shape), const(sh.shape)],
        out_specs=[pl.BlockSpec((tt, bsz, d), lambda i: (i, 0, 0)),
                   pl.BlockSpec((rows, d), lambda i: (i, 0)),
                   pl.BlockSpec((rows, d), lambda i: (i, 0))],
        out_shape=[jax.ShapeDtypeStruct((t, bsz, d), F32),
                   jax.ShapeDtypeStruct((t * bsz, d), F32),
                   jax.ShapeDtypeStruct((t * bsz, d), BF16)],
        scratch_shapes=[pltpu.VMEM((tt // 2, LANE, LANE), F32), pltpu.VMEM((rows, c), F32)],
        compiler_params=_cparams("parallel"),
        name="merge",
    )(y0, y1, bv, g_out, h_lru, h_lru, z, z, x_tm, lnx, wa, wb, wo, gt1, g_post, g_pre, scp, sh)


def _experts_kernel(has_prev, blk_ref, eid_ref, lo_ref, hi_ref, x_ref, wgu_ref, wdn_ref, *rest):
    o_ref, wgu_b, wdn_b = rest[1:] if has_prev else rest
    i = pl.program_id(0)
    prev = jnp.maximum(i - 1, 0)
    ff = wdn_ref.shape[1]

    @pl.when(jnp.logical_or(i == 0, eid_ref[i] != eid_ref[prev]))
    def _():
        wgu_b[...] = wgu_ref[0].astype(BF16)
        wdn_b[...] = wdn_ref[0].astype(BF16)

    @pl.when(hi_ref[i] > lo_ref[i])
    def _():
        gu = jnp.dot(x_ref[...], wgu_b[...], preferred_element_type=F32)
        g = gu[:, :ff]
        u = gu[:, ff:]
        act = (g * jax.nn.sigmoid(g) * u).astype(BF16)
        res = jnp.dot(act, wdn_b[...], preferred_element_type=F32)
        row = lax.broadcasted_iota(jnp.int32, (res.shape[0], 1), 0)
        mine = jnp.logical_and(row >= lo_ref[i], row < hi_ref[i])
        first = jnp.logical_or(i == 0, blk_ref[i] != blk_ref[prev])

        @pl.when(first)
        def _():
            o_ref[...] = jnp.where(mine, res, 0.0).astype(o_ref.dtype)

        @pl.when(jnp.logical_not(first))
        def _():
            o_ref[...] = jnp.where(mine, res, o_ref[...].astype(F32)).astype(o_ref.dtype)


def _grouped_experts(xs, items, w_gu, w_down, n_rows, blk_off, prev_out=None):
    d = xs.shape[1]
    _, _, ff2 = w_gu.shape
    ff = ff2 // 2
    n_items = items[0].shape[0]
    in_specs = [pl.BlockSpec((MOE_ROWS, d), lambda i, blk, eid, lo, hi: (blk[i], 0)),
                pl.BlockSpec((1, d, ff2), lambda i, blk, eid, lo, hi: (eid[i], 0, 0)),
                pl.BlockSpec((1, ff, d), lambda i, blk, eid, lo, hi: (eid[i], 0, 0))]
    args = [*items, xs, w_gu, w_down]
    aliases = {}
    if prev_out is not None:
        in_specs.append(pl.BlockSpec(memory_space=pl.ANY))
        aliases = {len(args): 0}
        args.append(prev_out)
    grid_spec = pltpu.PrefetchScalarGridSpec(
        num_scalar_prefetch=4,
        grid=(n_items,),
        in_specs=in_specs,
        out_specs=pl.BlockSpec((MOE_ROWS, d), lambda i, blk, eid, lo, hi: (blk[i] + blk_off, 0)),
        scratch_shapes=[pltpu.VMEM((d, ff2), BF16), pltpu.VMEM((ff, d), BF16)],
    )
    return pl.pallas_call(
        functools.partial(_experts_kernel, prev_out is not None),
        grid_spec=grid_spec,
        out_shape=jax.ShapeDtypeStruct((n_rows, d), BF16),
        input_output_aliases=aliases,
        compiler_params=_cparams("arbitrary"),
        name="moe_experts",
    )(*args)


def _expert_items(start, end, r0, r1):
    n_e = start.shape[0]
    n_blocks = (r1 - r0) // MOE_ROWS
    n_items = n_blocks + n_e
    start = jnp.clip(start, r0, r1) - r0
    end = jnp.clip(end, r0, r1) - r0
    first_blk = start // MOE_ROWS
    per_e = jnp.where(end > start, (end - 1) // MOE_ROWS - first_blk + 1, 0)
    item_end = jnp.cumsum(per_e)
    w = jnp.arange(n_items, dtype=jnp.int32)
    eid = jnp.minimum(jnp.sum((item_end[None, :] <= w[:, None]).astype(jnp.int32), axis=1), n_e - 1)
    onehot = (eid[:, None] == jnp.arange(n_e)[None, :]).astype(jnp.int32)
    pick = lambda v: jnp.sum(onehot * v[None, :], axis=1)
    valid = w < item_end[-1]
    blk = jnp.where(valid, pick(first_blk) + w - (pick(item_end) - pick(per_e)), n_blocks - 1)
    lo = jnp.clip(pick(start) - blk * MOE_ROWS, 0, MOE_ROWS)
    hi = jnp.where(valid, jnp.clip(pick(end) - blk * MOE_ROWS, 0, MOE_ROWS), lo)
    i32 = lambda v: v.astype(jnp.int32)
    return i32(blk), i32(eid), i32(lo), i32(hi)


def _argmax_pick(cur, iota, big, axis):
    mx = jnp.max(cur, axis=axis, keepdims=True)
    ix = jnp.min(jnp.where(cur == mx, iota, big), axis=axis, keepdims=True)
    return iota == ix


def _router_kernel(u_ref, wt_ref, bias_ref, tri_ref, gate_o, idx_o, rank_o, cnt_o):
    n_e = wt_ref.shape[0]
    tr = u_ref.shape[0]
    per_g = n_e // N_GROUPS
    neg = -jnp.inf

    @pl.when(pl.program_id(0) == 0)
    def _():
        cnt_o[...] = jnp.zeros_like(cnt_o)

    logits = lax.dot_general(wt_ref[...], u_ref[...], (((1,), (1,)), ((), ())),
                             precision=HP, preferred_element_type=F32)
    scores = jax.nn.sigmoid(logits)
    biased = scores + bias_ref[...]
    b3 = biased.reshape(N_GROUPS, per_g, tr)
    ie = lax.broadcasted_iota(jnp.int32, b3.shape, 1)
    p1 = _argmax_pick(b3, ie, per_g, 1)
    m1 = jnp.max(b3, axis=1)
    m2 = jnp.max(jnp.where(p1, neg, b3), axis=1)
    cur = m1 + m2
    ig = lax.broadcasted_iota(jnp.int32, cur.shape, 0)
    sel_g = jnp.zeros(cur.shape, F32)
    for _ in range(TOPK_GROUPS):
        pick = _argmax_pick(cur, ig, N_GROUPS, 0)
        sel_g = jnp.where(pick, 1.0, sel_g)
        cur = jnp.where(pick, neg, cur)
    cur = jnp.where(sel_g[:, None, :] > 0.5, b3, neg).reshape(n_e, tr)
    i_e = lax.broadcasted_iota(jnp.int32, cur.shape, 0)
    i_ef = i_e.astype(F32)
    picks, top_s = [], []
    sel_f = jnp.zeros(cur.shape, F32)
    for _ in range(TOP_K):
        pick = _argmax_pick(cur, i_e, n_e, 0)
        picks.append(pick)
        top_s.append(jnp.sum(jnp.where(pick, scores, 0.0), axis=0, keepdims=True))
        sel_f = jnp.where(pick, 1.0, sel_f)
        cur = jnp.where(pick, neg, cur)
    denom = top_s[0]
    for s in top_s[1:]:
        denom = denom + s
    rank = jnp.dot(sel_f.astype(BF16), tri_ref[...], preferred_element_type=F32) + cnt_o[:, 0:1]
    cnt_o[...] = cnt_o[...] + jnp.sum(sel_f, axis=1, keepdims=True)
    zrow = jnp.zeros((SUBLANE - TOP_K, tr), F32)
    gate_o[...] = jnp.concatenate([s / denom * ROUTED_SCALE for s in top_s] + [zrow], axis=0)
    idx_rows = [jnp.sum(jnp.where(p, i_ef, 0.0), axis=0, keepdims=True) for p in picks]
    idx_o[...] = jnp.concatenate(idx_rows + [zrow], axis=0).astype(jnp.int32)
    rank_rows = [jnp.sum(jnp.where(p, rank, 0.0), axis=0, keepdims=True) for p in picks]
    rank_o[...] = jnp.concatenate(rank_rows + [zrow], axis=0).astype(jnp.int32)


def _router(u, w_t, bias, tr=512):
    m, d = u.shape
    n_e = w_t.shape[0]
    tri = (jnp.arange(tr)[:, None] < jnp.arange(tr)[None, :]).astype(BF16)
    tok = pl.BlockSpec((SUBLANE, tr), lambda i: (0, i))
    return pl.pallas_call(
        _router_kernel,
        grid=(m // tr,),
        in_specs=[pl.BlockSpec((tr, d), lambda i: (i, 0)),
                  pl.BlockSpec((n_e, d), lambda i: (0, 0)),
                  pl.BlockSpec((n_e, 1), lambda i: (0, 0)),
                  pl.BlockSpec((tr, tr), lambda i: (0, 0))],
        out_specs=[tok, tok, tok, pl.BlockSpec((n_e, LANE), lambda i: (0, 0))],
        out_shape=[jax.ShapeDtypeStruct((SUBLANE, m), F32),
                   jax.ShapeDtypeStruct((SUBLANE, m), jnp.int32),
                   jax.ShapeDtypeStruct((SUBLANE, m), jnp.int32),
                   jax.ShapeDtypeStruct((n_e, LANE), F32)],
        compiler_params=_cparams("arbitrary"),
        name="router",
    )(u, w_t, bias, tri)


def _final_kernel(*refs):
    ye_refs = refs[:TOP_K]
    gate_ref, ub_ref, h_ref, wgu_ref, wdn_ref, gt_ref, g_ref, o_ref, rl_ref = refs[TOP_K:]
    tt, bsz, d = h_ref.shape
    ff = wdn_ref.shape[0]
    routed = jnp.zeros((tt * bsz, d), F32)
    for j in range(TOP_K):
        routed = routed + gate_ref[:, j:j + 1] * ye_refs[j][0].astype(F32)
    gu = jnp.dot(ub_ref[...], wgu_ref[...], preferred_element_type=F32)
    g = gu[:, :ff]
    act = (g * jax.nn.sigmoid(g) * gu[:, ff:]).astype(BF16)
    moe = (routed + jnp.dot(act, wdn_ref[...], preferred_element_type=F32)).reshape(tt, bsz, d)
    moe = moe * lax.rsqrt(jnp.mean(moe * moe, axis=-1, keepdims=True) + NORM_EPS) * g_ref[...]
    out = (h_ref[...] + gt_ref[...] * moe).reshape(tt * bsz, d)
    for j in range(d // LANE):
        rl_ref[j] = out[:, j * LANE:(j + 1) * LANE]
    for b in range(bsz):
        for j in range(d // LANE):
            o_ref[b, :, j * LANE:(j + 1) * LANE] = rl_ref[j, pl.ds(b, tt, stride=bsz), :]


def _final(ye, gate, ub, h, wgu, wdn, gt, g, tt=64):
    t, bsz, d = h.shape
    rows = tt * bsz
    const = lambda shape: pl.BlockSpec(shape, lambda i: (0,) * len(shape))
    return pl.pallas_call(
        _final_kernel,
        grid=(t // tt,),
        in_specs=[pl.BlockSpec((1, rows, d), functools.partial(lambda j, i: (j, i, 0), j))
                  for j in range(TOP_K)] + [
                  pl.BlockSpec((rows, SUBLANE), lambda i: (i, 0)),
                  pl.BlockSpec((rows, d), lambda i: (i, 0)),
                  pl.BlockSpec((tt, bsz, d), lambda i: (i, 0, 0)),
                  const(wgu.shape), const(wdn.shape), const(gt.shape), const(g.shape)],
        out_specs=pl.BlockSpec((bsz, tt, d), lambda i: (0, i, 0)),
        out_shape=jax.ShapeDtypeStruct((bsz, t, d), F32),
        scratch_shapes=[pltpu.VMEM((d // LANE, rows, LANE), F32)],
        compiler_params=_cparams("parallel"),
        name="moe_final",
    )(*([ye] * TOP_K), gate, ub, h, wgu, wdn, gt, g)


def kernel(x, c, ctx, c_ctx, w_mod, b_mod, norm_g, w_in, shift_mu, rw_w0, rw_w_up, rw_a0, rw_a_up,
           rw_g_up, rw_k_k, rw_k_a, rw_r_k, rw_lnx, lru_conv_w, lru_conv_b, lru_gate_w, lru_gate_b,
           lru_l, w_branch_a, w_branch_b, w_out, router_w, router_b, ex_w_gu, ex_w_down,
           sh_w_gu, sh_w_down):
    l = 0
    bsz, t, d = x.shape
    tc = ctx.shape[1]
    rows_g = t // GRID_W
    m = bsz * t
    cw = rw_w0.shape[-1]
    cl = lru_l.shape[-1]
    nh = cw // RWKV_HEAD
    assert bsz == SUBLANE and cw == cl == d and 2 * nh * bsz // 2 == LANE
    rwkv_cols = 3 * cw + 2 * DECAY_LORA + 2 * AAA_LORA + GATE_LORA
    lora_cols = rwkv_cols - 3 * cw

    mod = jnp.dot(jax.nn.silu(c), w_mod[l], precision=HP) + b_mod[l]
    mod_c = jnp.dot(jax.nn.silu(c_ctx), w_mod[l], precision=HP) + b_mod[l]
    sh1, sc1, gt1, sh2, sc2, gt2 = jnp.split(mod, 6, axis=-1)
    csh1 = jnp.broadcast_to(mod_c[0:d], (bsz, d))
    csc1 = jnp.broadcast_to(mod_c[d:2 * d], (bsz, d))
    g_pre1, g_post1, g_pre2, g_post2 = [g[None, :] for g in norm_g[l]]

    wi = w_in[l]
    w_al = jnp.concatenate([wi[:, :rwkv_cols], jnp.zeros((d, cw - lora_cols), F32),
                            wi[:, rwkv_cols:]], axis=1).astype(BF16)
    zr_w = 4 * cw
    mu_al = jnp.concatenate([shift_mu[l], jnp.zeros((2, cw - lora_cols), F32)], axis=1)

    x_tm = jnp.transpose(x, (1, 0, 2))
    ctx_tm = jnp.transpose(ctx, (1, 0, 2))
    zc = _inproj(ctx_tm, 1.0 + csc1, csh1, g_pre1, w_al[:, :zr_w + cl])
    z = _inproj(x_tm, 1.0 + sc1, sh1, g_pre1, w_al, tn=2 * cw)

    blk = cl // LRU_BLOCKS
    per = MXU_N // blk
    gw = lru_gate_w[l].reshape(2, 2, LRU_BLOCKS // per, per, blk, blk)
    eye = jnp.eye(per, dtype=F32)
    gw_bd = jnp.einsum('dgqpjk,pr->dgqpjrk', gw, eye).reshape(2, 2, LRU_BLOCKS // per, MXU_N, MXU_N)
    gw_bd = gw_bd.astype(BF16)
    gb = lru_gate_b[l].reshape(2, 2, cl)
    sp_l = jax.nn.softplus(-lru_l[l])[:, None, :]
    lru_args = (lru_conv_w[l], lru_conv_b[l][None, :], gw_bd, gb, sp_l)
    _, hfin = _lru(zc.reshape(tc, 1, bsz, -1), 4, *lru_args, jnp.zeros((2, bsz, cl), F32))
    h_lru, _ = _lru(z.reshape(rows_g, GRID_W, bsz, -1), 4, *lru_args, hfin)
    h_lru = h_lru.reshape(2, m, cl)

    zeros_c = jnp.zeros((DECAY_LORA, cw), F32)
    wup2 = jnp.concatenate([jnp.concatenate([rw_w_up[l, 0], zeros_c], 1),
                            jnp.concatenate([zeros_c, rw_w_up[l, 1]], 1)], axis=0)
    aup2 = jnp.concatenate([jnp.concatenate([rw_a_up[l, 0], zeros_c], 1),
                            jnp.concatenate([zeros_c, rw_a_up[l, 1]], 1)], axis=0)
    gup = jnp.concatenate([rw_g_up[l], jnp.zeros((MXU_N - GATE_LORA, cw), F32)], axis=0)
    hid = jnp.arange(MXU_N) // RWKV_HEAD
    ones_blk = (hid[:, None] == hid[None, :]).astype(BF16)
    prep_args = (mu_al, rw_w0[l], wup2.astype(BF16), rw_a0[l], aup2.astype(BF16),
                 rw_k_k[l][None, :], rw_k_a[l][None, :], rw_r_k[l].reshape(1, cw),
                 gup.astype(BF16), ones_blk)
    pc = _rwkv_prep(zc, tc, zr_w, *prep_args)
    pz = _rwkv_prep(z, GRID_W, zr_w, *prep_args)
    s_zero = jnp.zeros((RWKV_HEAD, RWKV_HEAD, LANE), F32)
    ys = []
    for rev in (False, True):
        (s_c,) = _rwkv_scan(rev, pc[0], pc[3], pc[4], pc[1], pc[2], pc[5], s_zero, with_y=False)
        y_d, _ = _rwkv_scan(rev, pz[0], pz[3], pz[4], pz[1], pz[2], pz[5], s_c)
        ys.append(y_d)

    h, u2, u2b = _merge(ys[0], ys[1], pz[6], pz[7], h_lru, z, x_tm, rw_lnx[l],
                        w_branch_a[l].astype(BF16), w_branch_b[l].astype(BF16), w_out[l].astype(BF16),
                        gt1, g_post1, g_pre2, 1.0 + sc2, sh2)

    n_e = router_w.shape[-1]
    gate6, idx6, rank6, cnt = _router(u2, router_w[l].T, router_b[l][:, None])
    n_assign = m * TOP_K
    counts = cnt[:, 0].astype(jnp.int32)
    end = jnp.cumsum(counts)
    start = end - counts
    flat_e = idx6[:TOP_K].T.reshape(-1)
    bits = (n_assign - 1).bit_length()
    assert n_e << bits < 2 ** 31
    key = jnp.sort(flat_e * (1 << bits) + jnp.arange(n_assign, dtype=jnp.int32))
    sorted_tok = (key & ((1 << bits) - 1)) // TOP_K
    half = n_assign // 2
    assert half % MOE_ROWS == 0 and t % 2 == 0
    ys_e = None
    for r0, r1 in ((0, half), (half, n_assign)):
        items = _expert_items(start, end, r0, r1)
        ys_e = _grouped_experts(u2b[sorted_tok[r0:r1]], items, ex_w_gu[l], ex_w_down[l],
                                n_assign, r0 // MOE_ROWS, ys_e)
    sel = idx6[:TOP_K, :, None] == jnp.arange(n_e, dtype=jnp.int32)[None, None, :]
    pos = jnp.sum(jnp.where(sel, start[None, None, :], 0), axis=-1) + rank6[:TOP_K]
    ye = ys_e[pos.reshape(-1)].reshape(TOP_K, m, d)
    return _final(ye, gate6.T, u2b, h, sh_w_gu[l].astype(BF16), sh_w_down[l].astype(BF16),
                  gt2, g_post2)
```
